```python
import math
import jax, jax.numpy as jnp
from jax import lax
import numpy as np

D_MODEL = 2048
BATCH = 1
SEQ = 16384
DEPTH = 1

CHUNK = 64
EPS = 1e-6
H_A = 8
HD_A = 128
H_IDX = 16
D_IDX = 64
TOPK_MAX = 256
QBLOCK = 128
N_T5_BUCKETS = 32
T5_MAX_DIST = 1024
H_B = 8
HD_B = 128
N_LEFT_CHUNKS = 8
N_BAND = N_LEFT_CHUNKS + 1
REL_CLIP = 128
W_A = H_A * HD_A
W_B = H_B * HD_B
W_IDX_Q = H_IDX * D_IDX
SPLITS = [W_A, W_A, W_A, W_IDX_Q, D_IDX, H_IDX, W_B, W_B, W_B, D_MODEL, D_MODEL]
IN_COLS = sum(SPLITS)
N_GROUPS = 8
EXP_PER_GROUP = 8
N_EXPERTS = N_GROUPS * EXP_PER_GROUP
TOPK_IN_GROUP = 2
D_FF_E = D_MODEL // 4
MOE_BLOCK = 128

kernel_name = "hybrid_dsa_chunkband_hmoe_block"


def rmsnorm(x, g):
    xf = x.astype(jnp.float32)
    y = xf * lax.rsqrt(jnp.mean(xf * xf, axis=-1, keepdims=True) + EPS)
    return y.astype(x.dtype) * g


def t5_bucket(rel):
    nb = N_T5_BUCKETS // 2
    ret = (rel > 0).astype(jnp.int32) * nb
    n = jnp.abs(rel)
    max_exact = nb // 2
    nf = jnp.maximum(n, 1).astype(jnp.float32)
    large = max_exact + (jnp.log(nf / max_exact) / math.log(T5_MAX_DIST / max_exact)
                         * (nb - max_exact)).astype(jnp.int32)
    large = jnp.minimum(large, nb - 1)
    return ret + jnp.where(n < max_exact, n, large)


def dsa_branch(q, k, v, qi, ki, wi, t5_table):
    Bn, S = q.shape[0], q.shape[1]
    n_sel = min(TOPK_MAX, S // 4)
    nqb = S // QBLOCK
    key_pos = jnp.arange(S, dtype=jnp.int32)
    scale = 1.0 / math.sqrt(HD_A)
    w_scale = (H_IDX ** -0.5) * (D_IDX ** -0.5)

    def block(b):
        start = b * QBLOCK
        q_b = lax.dynamic_slice_in_dim(q, start, QBLOCK, axis=1)
        qi_b = lax.dynamic_slice_in_dim(qi, start, QBLOCK, axis=1)
        wi_b = lax.dynamic_slice_in_dim(wi, start, QBLOCK, axis=1)
        t = start + jnp.arange(QBLOCK, dtype=jnp.int32)
        limit = (t // CHUNK + 1) * CHUNK
        act = jax.nn.relu(jnp.einsum('bqhd,bsd->bqhs', qi_b, ki).astype(jnp.float32))
        score = jnp.einsum('bqhs,bqh->bqs', act, wi_b.astype(jnp.float32) * w_scale)
        vis = key_pos[None, :] < limit[:, None]
        score = jnp.where(vis[None], score, -jnp.inf)
        _, idx = lax.top_k(score, n_sel)
        valid = idx < limit[None, :, None]
        k_g = jax.vmap(lambda kk, ii: kk[ii])(k, idx)
        v_g = jax.vmap(lambda vv, ii: vv[ii])(v, idx)
        logits = jnp.einsum('bqhd,bqkhd->bhqk', q_b, k_g).astype(jnp.float32) * scale
        bias = t5_table[t5_bucket(idx - t[None, :, None])]
        logits = logits + jnp.transpose(bias, (0, 3, 1, 2)).astype(jnp.float32)
        logits = jnp.where(valid[:, None], logits, -jnp.inf)
        p = jax.nn.softmax(logits, axis=-1).astype(v.dtype)
        return jnp.einsum('bhqk,bqkhd->bqhd', p, v_g)

    out = lax.map(block, jnp.arange(nqb, dtype=jnp.int32))
    return jnp.transpose(out, (1, 0, 2, 3, 4)).reshape(Bn, S, W_A)


def chunk_band_branch(q, k, v, rel_table):
    Bn, S = q.shape[0], q.shape[1]
    nc = S // CHUNK
    pad = N_LEFT_CHUNKS * CHUNK
    kp = jnp.pad(k, ((0, 0), (pad, 0), (0, 0), (0, 0))).reshape(Bn, nc + N_LEFT_CHUNKS, CHUNK, H_B, HD_B)
    vp = jnp.pad(v, ((0, 0), (pad, 0), (0, 0), (0, 0))).reshape(Bn, nc + N_LEFT_CHUNKS, CHUNK, H_B, HD_B)
    k_band = jnp.concatenate([kp[:, j:j + nc] for j in range(N_BAND)], axis=2)
    v_band = jnp.concatenate([vp[:, j:j + nc] for j in range(N_BAND)], axis=2)
    qc = q.reshape(Bn, nc, CHUNK, H_B, HD_B)
    logits = jnp.einsum('bcqhd,bckhd->bchqk', qc, k_band).astype(jnp.float32) / math.sqrt(HD_B)
    qi = jnp.arange(CHUNK, dtype=jnp.int32)
    kj = jnp.arange(N_BAND * CHUNK, dtype=jnp.int32)
    rel = qi[:, None] - (kj[None, :] - pad)
    bias = rel_table[:, jnp.clip(rel, -REL_CLIP, REL_CLIP) + REL_CLIP]
    key_chunk = jnp.arange(nc, dtype=jnp.int32)[:, None] - N_LEFT_CHUNKS + kj[None, :] // CHUNK
    valid = key_chunk >= 0
    logits = jnp.where(valid[None, :, None, None, :],
                       logits + bias[None, None].astype(jnp.float32), -jnp.inf)
    p = jax.nn.softmax(logits, axis=-1).astype(v.dtype)
    out = jnp.einsum('bchqk,bckhd->bcqhd', p, v_band)
    return out.reshape(Bn, S, W_B)


def hierarchical_moe(h, w_rg, b_rg, w_re, b_re, w1, w3, w2):
    Bn, S, D = h.shape
    T = Bn * S
    xt = h.reshape(T, D)
    g_logits = (xt @ w_rg).astype(jnp.float32) + b_rg.astype(jnp.float32)
    g_prob = jax.nn.softmax(g_logits, axis=-1)
    grp = jnp.argmax(g_logits, axis=-1).astype(jnp.int32)
    p_grp = jnp.take_along_axis(g_prob, grp[:, None], axis=-1)[:, 0]
    e_logits = ((xt @ w_re).astype(jnp.float32) + b_re.astype(jnp.float32)).reshape(T, N_GROUPS, EXP_PER_GROUP)
    e_in = jnp.take_along_axis(e_logits, grp[:, None, None], axis=1)[:, 0]
    top_v, top_i = lax.top_k(e_in, TOPK_IN_GROUP)
    p_in = jax.nn.softmax(top_v, axis=-1)
    expert = grp[:, None] * EXP_PER_GROUP + top_i.astype(jnp.int32)
    weight = p_grp[:, None] * p_in

    M = T * TOPK_IN_GROUP
    e_flat = expert.reshape(M)
    tok_flat = jnp.repeat(jnp.arange(T, dtype=jnp.int32), TOPK_IN_GROUP)
    w_flat = weight.reshape(M)
    order = jnp.argsort(e_flat, stable=True)
    e_s, tok_s, w_s = e_flat[order], tok_flat[order], w_flat[order]
    counts = jnp.bincount(e_flat, length=N_EXPERTS).astype(jnp.int32)
    start = jnp.cumsum(counts) - counts
    padded = ((counts + MOE_BLOCK - 1) // MOE_BLOCK) * MOE_BLOCK
    pend = jnp.cumsum(padded)
    pstart = pend - padded
    dest = pstart[e_s] + (jnp.arange(M, dtype=jnp.int32) - start[e_s])
    nb = (M + MOE_BLOCK - 1) // MOE_BLOCK + N_EXPERTS
    P = nb * MOE_BLOCK
    tok_buf = jnp.zeros((P,), jnp.int32).at[dest].set(tok_s)
    w_buf = jnp.zeros((P,), h.dtype).at[dest].set(w_s.astype(h.dtype))
    blk_e = jnp.minimum(jnp.searchsorted(pend, jnp.arange(nb, dtype=jnp.int32) * MOE_BLOCK, side='right'),
                        N_EXPERTS - 1).astype(jnp.int32)

    def run(args):
        tok, wt, e = args
        xb = xt[tok]
        a = jax.nn.silu(xb @ w1[e]) * (xb @ w3[e])
        return (a @ w2[e]) * wt[:, None]

    y = lax.map(run, (tok_buf.reshape(nb, MOE_BLOCK), w_buf.reshape(nb, MOE_BLOCK), blk_e))
    out = jnp.zeros((T, D), h.dtype).at[tok_buf].add(y.reshape(P, D))
    return out.reshape(Bn, S, D)


def setup_inputs(seed: int = 0) -> dict:
    key = jax.random.key(seed)
    ks = jax.random.split(key, 24)
    f32 = jnp.float32
    D = D_MODEL

    def nrm(k, shape, scale):
        return jax.random.normal(k, shape, f32) * scale

    return {
        "x": nrm(ks[0], (BATCH, SEQ, D), 1.0),
        "c": nrm(ks[1], (BATCH, D), 1.0),
        "w_ada": nrm(ks[2], (DEPTH, D, 6 * D), D ** -0.5 * 0.5),
        "b_ada": nrm(ks[3], (DEPTH, 6 * D), 0.02),
        "norm_mix": 1.0 + nrm(ks[4], (DEPTH, D), 0.02),
        "w_in": nrm(ks[5], (DEPTH, D, IN_COLS), D ** -0.5),
        "t5_table": nrm(ks[6], (N_T5_BUCKETS, H_A), 0.3),
        "rel_table": nrm(ks[7], (DEPTH, H_B, 2 * REL_CLIP + 1), 0.3),
        "w_up_a": nrm(ks[8], (DEPTH, W_A, D), W_A ** -0.5),
        "w_up_b": nrm(ks[9], (DEPTH, W_B, D), W_B ** -0.5),
        "w_o": nrm(ks[10], (DEPTH, D, D), D ** -0.5),
        "norm_ffn": 1.0 + nrm(ks[11], (DEPTH, D), 0.02),
        "w_rg": nrm(ks[12], (DEPTH, D, N_GROUPS), D ** -0.5),
        "b_rg": nrm(ks[13], (DEPTH, N_GROUPS), 0.01),
        "w_re": nrm(ks[14], (DEPTH, D, N_EXPERTS), D ** -0.5),
        "b_re": nrm(ks[15], (DEPTH, N_EXPERTS), 0.01),
        "w1": nrm(ks[16], (DEPTH, N_EXPERTS, D, D_FF_E), D ** -0.5),
        "w3": nrm(ks[17], (DEPTH, N_EXPERTS, D, D_FF_E), D ** -0.5),
        "w2": nrm(ks[18], (DEPTH, N_EXPERTS, D_FF_E, D), D_FF_E ** -0.5),
        "norm_final": 1.0 + nrm(ks[19], (D,), 0.02),
    }


def reference(x, c, w_ada, b_ada, norm_mix, w_in, t5_table, rel_table, w_up_a, w_up_b,
              w_o, norm_ffn, w_rg, b_rg, w_re, b_re, w1, w3, w2, norm_final):
    Bn, S, D = x.shape
    split_pts = list(np.cumsum(SPLITS)[:-1])
    c_act = jax.nn.silu(c)
    for l in range(DEPTH):
        mod = c_act @ w_ada[l] + b_ada[l]
        sh1, sc1, g1, sh2, sc2, g2 = [m[:, None, :] for m in jnp.split(mod, 6, axis=-1)]

        h = rmsnorm(x, norm_mix[l]) * (1.0 + sc1) + sh1
        proj = h @ w_in[l]
        qa, ka, va, qi, ki, wi, qb, kb, vb, gate_a, gate_b = jnp.split(proj, split_pts, axis=-1)
        y_a = dsa_branch(qa.reshape(Bn, S, H_A, HD_A), ka.reshape(Bn, S, H_A, HD_A),
                         va.reshape(Bn, S, H_A, HD_A), qi.reshape(Bn, S, H_IDX, D_IDX),
                         ki, wi, t5_table)
        y_b = chunk_band_branch(qb.reshape(Bn, S, H_B, HD_B), kb.reshape(Bn, S, H_B, HD_B),
                                vb.reshape(Bn, S, H_B, HD_B), rel_table[l])
        merged = jax.nn.sigmoid(gate_a) * (y_a @ w_up_a[l]) + jax.nn.sigmoid(gate_b) * (y_b @ w_up_b[l])
        x = x + g1 * (merged @ w_o[l])

        h2 = rmsnorm(x, norm_ffn[l]) * (1.0 + sc2) + sh2
        x = x + g2 * hierarchical_moe(h2, w_rg[l], b_rg[l], w_re[l], b_re[l], w1[l], w3[l], w2[l])
    return rmsnorm(x, norm_final)
```

```python
import functools
import math

import numpy as np
import jax
import jax.numpy as jnp
from jax import lax
from jax.experimental import pallas as pl
from jax.experimental.pallas import tpu as pltpu

F32 = jnp.float32
BF16 = jnp.bfloat16
I32 = jnp.int32

CHUNK = 64
EPS = 1e-6
H_A, HD_A = 8, 128
H_IDX, D_IDX = 16, 64
TOPK_MAX = 256
N_T5_BUCKETS = 32
T5_MAX_DIST = 1024
H_B, HD_B = 8, 128
N_LEFT_CHUNKS = 8
REL_CLIP = 128
N_GROUPS = 8
EXP_PER_GROUP = 8
N_EXPERTS = N_GROUPS * EXP_PER_GROUP
TOPK_IN_GROUP = 2

W_A = H_A * HD_A
W_B = H_B * HD_B
W_IDX_Q = H_IDX * D_IDX

NEG = -1e30
INT_MIN = -(2 ** 31)

LANES = 128
VMEM_LIMIT = 56 * 1024 * 1024

SEL_TQ = 128
SEL_TK = 512
BAND_TQ = 256
MOE_BM = 256


def _params(sem, vmem=VMEM_LIMIT):
    return pltpu.CompilerParams(dimension_semantics=sem, vmem_limit_bytes=vmem)


def _ada_kernel(c_ref, w_ref, b_ref, o_ref, *, kc):
    d = w_ref.shape[0]
    tn = w_ref.shape[1]

    def body(k, acc):
        r0 = pl.multiple_of(k * kc, kc)
        cc = c_ref[pl.ds(r0, kc), :]
        ca = cc * jax.nn.sigmoid(cc)
        return acc + jnp.sum(w_ref[pl.ds(r0, kc), :] * ca, axis=0, keepdims=True)

    acc = lax.fori_loop(0, d // kc, body, jnp.zeros((1, tn), F32))
    o_ref[...] = acc + b_ref[...]


def _ada(c_col, w, b_row, tn=1024, kc=256):
    d, n = w.shape
    return pl.pallas_call(
        functools.partial(_ada_kernel, kc=kc),
        grid=(n // tn,),
        in_specs=[pl.BlockSpec((d, 1), lambda j: (0, 0)),
                  pl.BlockSpec((d, tn), lambda j: (0, j)),
                  pl.BlockSpec((1, tn), lambda j: (0, j))],
        out_specs=pl.BlockSpec((1, tn), lambda j: (0, j)),
        out_shape=jax.ShapeDtypeStruct((1, n), F32),
        compiler_params=_params(("arbitrary",)),
        name="ada",
    )(c_col, w, b_row)


def _rms_mod_kernel(x_ref, g_ref, sc_ref, sh_ref, o_ref):
    x = x_ref[...]
    y = x * lax.rsqrt(jnp.mean(x * x, axis=-1, keepdims=True) + EPS)
    o_ref[...] = (y * g_ref[...] * (1.0 + sc_ref[...]) + sh_ref[...]).astype(o_ref.dtype)


def _rms_mod(x2, g, sc, sh, out_dtype, tm=512):
    t, d = x2.shape
    row = pl.BlockSpec((1, d), lambda i: (0, 0))
    return pl.pallas_call(
        _rms_mod_kernel,
        grid=(t // tm,),
        in_specs=[pl.BlockSpec((tm, d), lambda i: (i, 0)), row, row, row],
        out_specs=pl.BlockSpec((tm, d), lambda i: (i, 0)),
        out_shape=jax.ShapeDtypeStruct((t, d), out_dtype),
        compiler_params=_params(("parallel",)),
        name="rms_mod",
    )(x2, g, sc, sh)


def _mm_kernel(a_ref, b_ref, o_ref):
    o_ref[...] = jnp.dot(a_ref[...], b_ref[...], preferred_element_type=F32).astype(o_ref.dtype)


def _matmul(a, b, out_dtype, tm, tn, name):
    m, k = a.shape
    n = b.shape[1]
    return pl.pallas_call(
        _mm_kernel,
        grid=(m // tm, n // tn),
        in_specs=[pl.BlockSpec((tm, k), lambda i, j: (i, 0)),
                  pl.BlockSpec((k, tn), lambda i, j: (0, j))],
        out_specs=pl.BlockSpec((tm, tn), lambda i, j: (i, j)),
        out_shape=jax.ShapeDtypeStruct((m, n), out_dtype),
        compiler_params=_params(("parallel", "arbitrary")),
        name=name,
    )(a, b)


def _sortable(x):
    bits = pltpu.bitcast(x, I32)
    return bits ^ ((bits >> 31) & 0x7FFFFFFF)


def _select_kernel(qi_ref, wi_ref, ki_ref, mask_ref, key_sc, qh_sc, *, n_sel):
    nkt, tq, tk = key_sc.shape
    i = pl.program_id(0)
    q0 = i * tq
    nvis = (q0 + tq + tk - 1) // tk

    for h in range(H_IDX):
        qh_sc[h] = qi_ref[:, h * D_IDX:(h + 1) * D_IDX]

    row = lax.broadcasted_iota(I32, (tq, 1), 0) + q0
    limit = (row // CHUNK + 1) * CHUNK
    w = wi_ref[...]

    def score_tile(kt, carry):
        k_t = ki_ref[kt]
        acc = jnp.zeros((tq, tk), F32)
        for h in range(H_IDX):
            s = lax.dot_general(qh_sc[h], k_t, (((1,), (1,)), ((), ())),
                                preferred_element_type=F32)
            acc = acc + w[:, h:h + 1] * jnp.maximum(s, 0.0)
        col = lax.broadcasted_iota(I32, (tq, tk), 1) + kt * tk
        key_sc[kt] = jnp.where(col < limit, _sortable(acc), INT_MIN)
        return carry

    lax.fori_loop(0, nvis, score_tile, 0)

    def count(pred):
        def body(kt, c):
            kk = key_sc[kt]
            col = lax.broadcasted_iota(I32, (tq, tk), 1) + kt * tk
            m = pred(kk, col).astype(I32)
            part = m[:, 0:LANES]
            for a in range(1, tk // LANES):
                part = part + m[:, a * LANES:(a + 1) * LANES]
            return c + part
        c = lax.fori_loop(0, nvis, body, jnp.zeros((tq, LANES), I32))
        return jnp.sum(c, axis=1, keepdims=True)

    def bit_body(b, y):
        cand_u = y | jnp.left_shift(jnp.int32(1), 31 - b)
        cnt = count(lambda kk, col: kk >= (cand_u ^ INT_MIN))
        return jnp.where(cnt >= n_sel, cand_u, y)

    y = lax.fori_loop(0, 32, bit_body, jnp.zeros((tq, 1), I32))
    thr = jnp.maximum(y ^ INT_MIN, INT_MIN + 1)

    n_gt = count(lambda kk, col: kk > thr)
    n_ge = count(lambda kk, col: kk >= thr)
    need = n_sel - n_gt
    has_excess = jnp.max(jnp.where(n_ge > n_sel, 1, 0)) > 0

    def tie_cut():
        nbits = max(1, int(math.ceil(math.log2(nkt * tk))))

        def cut_body(b, p):
            cand = p | jnp.left_shift(jnp.int32(1), nbits - 1 - b)
            cnt = count(lambda kk, col: (kk == thr) & (col < cand))
            return jnp.where(cnt < need, cand, p)

        p = lax.fori_loop(0, nbits, cut_body, jnp.zeros((tq, 1), I32))
        return p + 1

    cut = lax.cond(has_excess, tie_cut, lambda: jnp.full((tq, 1), nkt * tk, I32))
    cut = jnp.where(n_ge > n_sel, cut, nkt * tk)

    def write_tile(kt, carry):
        kk = key_sc[kt]
        col = lax.broadcasted_iota(I32, (tq, tk), 1) + kt * tk
        sel = (kk > thr) | ((kk == thr) & (col < cut))
        mask_ref[kt] = jnp.where(sel, 0.0, NEG).astype(mask_ref.dtype)
        return carry

    lax.fori_loop(0, nvis, write_tile, 0)

    def fill_tile(kt, carry):
        mask_ref[kt] = jnp.full((tq, tk), NEG, mask_ref.dtype)
        return carry

    lax.fori_loop(nvis, nkt, fill_tile, 0)


def _select(proj, wi, ki3, n_sel, qi_col_block):
    s = proj.shape[0]
    nkt, tk, _ = ki3.shape
    tq = SEL_TQ
    nqb = s // tq
    return pl.pallas_call(
        functools.partial(_select_kernel, n_sel=n_sel),
        grid=(nqb,),
        in_specs=[pl.BlockSpec((tq, W_IDX_Q), lambda i: (i, qi_col_block)),
                  pl.BlockSpec((tq, H_IDX), lambda i: (i, 0)),
                  pl.BlockSpec((nkt, tk, D_IDX), lambda i: (0, 0, 0))],
        out_specs=pl.BlockSpec((None, nkt, tq, tk), lambda i: (i, 0, 0, 0)),
        out_shape=jax.ShapeDtypeStruct((nqb, nkt, tq, tk), BF16),
        scratch_shapes=[pltpu.VMEM((nkt, tq, tk), I32),
                        pltpu.VMEM((H_IDX, tq, D_IDX), BF16)],
        compiler_params=_params(("arbitrary",)),
        name="dsa_select",
    )(proj, wi, ki3)


def _dsa_attn_kernel(qb_ref, kb_ref, first_ref, last_ref, slab_ref,
                     q_ref, k_ref, v_ref, mask_ref, band_ref, o_ref,
                     m_sc, l_sc, acc_sc):
    step = pl.program_id(0)
    tq = q_ref.shape[0]
    tk = k_ref.shape[0]
    nslab = tk // LANES

    @pl.when(first_ref[step] == 1)
    def _():
        m_sc[...] = jnp.full(m_sc.shape, NEG, F32)
        l_sc[...] = jnp.zeros(l_sc.shape, F32)
        acc_sc[...] = jnp.zeros(acc_sc.shape, F32)

    mask = mask_ref[...].astype(F32)
    slab0 = slab_ref[step]
    slab_step = jnp.where(slab0 == 0, 0, 1)

    for h in range(H_A):
        hs = slice(h * HD_A, (h + 1) * HD_A)
        s = lax.dot_general(q_ref[:, hs], k_ref[:, hs], (((1,), (1,)), ((), ())),
                            preferred_element_type=F32)
        bias = jnp.concatenate(
            [band_ref[h, slab0 + a * slab_step] for a in range(nslab)], axis=1)
        s = s + bias + mask
        m_old = m_sc[h]
        m_new = jnp.maximum(m_old, jnp.max(s, axis=1, keepdims=True))
        alpha = jnp.exp(m_old - m_new)
        p = jnp.exp(s - m_new)
        l_sc[h] = alpha * l_sc[h] + jnp.sum(p, axis=1, keepdims=True)
        m_sc[h] = m_new
        pv = jnp.dot(p.astype(BF16), v_ref[:, hs], preferred_element_type=F32)
        acc_sc[:, hs] = alpha * acc_sc[:, hs] + pv

    @pl.when(last_ref[step] == 1)
    def _():
        for h in range(H_A):
            hs = slice(h * HD_A, (h + 1) * HD_A)
            o_ref[:, hs] = (acc_sc[:, hs] / l_sc[h]).astype(o_ref.dtype)


def _t5_bucket_np(rel):
    nb = N_T5_BUCKETS // 2
    ret = (rel > 0).astype(np.int32) * nb
    n = np.abs(rel)
    max_exact = nb // 2
    nf = np.maximum(n, 1).astype(np.float32)
    large = max_exact + (np.log(nf / np.float32(max_exact)) / np.float32(math.log(T5_MAX_DIST / max_exact))
                         * np.float32(nb - max_exact)).astype(np.int32)
    large = np.minimum(large, nb - 1)
    return ret + np.where(n < max_exact, n, large)


def _dsa_plan(s, tq, tk):
    nqb = s // tq
    d_all = np.arange(-(s - 1), CHUNK, dtype=np.int64)
    b_all = _t5_bucket_np(d_all)
    far_bucket = int(b_all[0])
    varying = np.nonzero(b_all != far_bucket)[0]
    d_lo = int(d_all[varying[0]]) if len(varying) else 0
    o_min = min(int(math.ceil((d_lo - (tk - 1)) / LANES)) * LANES, 0)
    n_near_slabs = (-o_min + tk) // LANES
    qb, kb, first, last, slab = [], [], [], [], []
    for i in range(nqb):
        q0 = i * tq
        nvis = (q0 + tq + tk - 1) // tk
        for j in range(nvis):
            o = j * tk - q0
            qb.append(i)
            kb.append(j)
            first.append(1 if j == 0 else 0)
            last.append(1 if j == nvis - 1 else 0)
            slab.append(0 if o < o_min else 1 + (o - o_min) // LANES)
    r = np.arange(tq)[:, None]
    c = np.arange(LANES)[None, :]
    buckets = np.full((1 + n_near_slabs, tq, LANES), far_bucket, np.int32)
    for u in range(n_near_slabs):
        d = (o_min + u * LANES + c) - r
        buckets[1 + u] = _t5_bucket_np(np.clip(d, -(s - 1), None))
    plan = tuple(np.asarray(a, np.int32) for a in (qb, kb, first, last, slab))
    return plan, buckets


def _dsa_attn(proj, mask4, t5_table, q_col, k_col, v_col):
    s = proj.shape[0]
    tq, tk = SEL_TQ, SEL_TK
    (qb, kb, first, last, slab), buckets = _dsa_plan(s, tq, tk)
    nsteps = len(qb)
    band = jnp.transpose(t5_table, (1, 0))[:, buckets]
    nsl = band.shape[1]
    grid_spec = pltpu.PrefetchScalarGridSpec(
        num_scalar_prefetch=5,
        grid=(nsteps,),
        in_specs=[
            pl.BlockSpec((tq, W_A), lambda t, qb, kb, f, l, sl: (qb[t], q_col)),
            pl.BlockSpec((tk, W_A), lambda t, qb, kb, f, l, sl: (kb[t], k_col)),
            pl.BlockSpec((tk, W_A), lambda t, qb, kb, f, l, sl: (kb[t], v_col)),
            pl.BlockSpec((None, None, tq, tk), lambda t, qb, kb, f, l, sl: (qb[t], kb[t], 0, 0)),
            pl.BlockSpec((H_A, nsl, tq, LANES), lambda t, qb, kb, f, l, sl: (0, 0, 0, 0)),
        ],
        out_specs=pl.BlockSpec((tq, W_A), lambda t, qb, kb, f, l, sl: (qb[t], 0)),
        scratch_shapes=[pltpu.VMEM((H_A, tq, 1), F32),
                        pltpu.VMEM((H_A, tq, 1), F32),
                        pltpu.VMEM((tq, W_A), F32)],
    )
    return pl.pallas_call(
        _dsa_attn_kernel,
        grid_spec=grid_spec,
        out_shape=jax.ShapeDtypeStruct((s, W_A), BF16),
        compiler_params=_params(("arbitrary",)),
        name="dsa_attn",
    )(jnp.asarray(qb), jnp.asarray(kb), jnp.asarray(first), jnp.asarray(last), jnp.asarray(slab),
      proj, proj, proj, mask4, band)


def _band_kernel(q_ref, k0_ref, k1_ref, k2_ref, v0_ref, v1_ref, v2_ref, bias_ref, o_ref):
    i = pl.program_id(0)
    tq = q_ref.shape[0]
    wk = 3 * tq
    col = lax.broadcasted_iota(I32, (tq, wk), 1)
    start_mask = jnp.where(col + (i - 2) * tq >= 0, 0.0, NEG)
    for h in range(H_B):
        hs = slice(h * HD_B, (h + 1) * HD_B)
        kc = jnp.concatenate([k0_ref[:, hs], k1_ref[:, hs], k2_ref[:, hs]], axis=0)
        vc = jnp.concatenate([v0_ref[:, hs], v1_ref[:, hs], v2_ref[:, hs]], axis=0)
        s = lax.dot_general(q_ref[:, hs], kc, (((1,), (1,)), ((), ())),
                            preferred_element_type=F32)
        s = s + bias_ref[h] + start_mask
        m = jnp.max(s, axis=1, keepdims=True)
        p = jnp.exp(s - m)
        l = jnp.sum(p, axis=1, keepdims=True)
        pv = jnp.dot(p.astype(BF16), vc, preferred_element_type=F32)
        o_ref[:, hs] = (pv / l).astype(o_ref.dtype)


def _band_bias(rel_table, tq):
    assert 2 * tq >= N_LEFT_CHUNKS * CHUNK and tq % CHUNK == 0
    r = np.arange(tq)[:, None]
    c = np.arange(3 * tq)[None, :]
    rel = r - (c - 2 * tq)
    idx = np.clip(rel, -REL_CLIP, REL_CLIP) + REL_CLIP
    dchunk = (c - 2 * tq) // CHUNK - r // CHUNK
    in_band = (dchunk <= 0) & (dchunk >= -N_LEFT_CHUNKS)
    bias = rel_table[:, idx]
    return jnp.where(jnp.asarray(in_band)[None], bias, NEG).astype(F32)


def _band_attn(proj, rel_table, q_col, k_col, v_col):
    s = proj.shape[0]
    tq = BAND_TQ
    bias = _band_bias(rel_table, tq)

    def kv(col, back):
        return pl.BlockSpec((tq, W_B), lambda i: (jnp.maximum(i - back, 0), col))

    return pl.pallas_call(
        _band_kernel,
        grid=(s // tq,),
        in_specs=[pl.BlockSpec((tq, W_B), lambda i: (i, q_col)),
                  kv(k_col, 2), kv(k_col, 1), kv(k_col, 0),
                  kv(v_col, 2), kv(v_col, 1), kv(v_col, 0),
                  pl.BlockSpec((H_B, tq, 3 * tq), lambda i: (0, 0, 0))],
        out_specs=pl.BlockSpec((tq, W_B), lambda i: (i, 0)),
        out_shape=jax.ShapeDtypeStruct((s, W_B), BF16),
        compiler_params=_params(("parallel",)),
        name="band_attn",
    )(proj, proj, proj, proj, proj, proj, proj, bias)


def _merge_kernel(ya_ref, yb_ref, ga_ref, gb_ref, wa_ref, wb_ref, o_ref):
    ua = jnp.dot(ya_ref[...], wa_ref[...], preferred_element_type=F32)
    ub = jnp.dot(yb_ref[...], wb_ref[...], preferred_element_type=F32)
    o_ref[...] = (jax.nn.sigmoid(ga_ref[...]) * ua + jax.nn.sigmoid(gb_ref[...]) * ub).astype(o_ref.dtype)


def _merge(ya, yb, gates, wa, wb, tm=256):
    s, d = ya.shape[0], wa.shape[1]
    return pl.pallas_call(
        _merge_kernel,
        grid=(s // tm,),
        in_specs=[pl.BlockSpec((tm, W_A), lambda i: (i, 0)),
                  pl.BlockSpec((tm, W_B), lambda i: (i, 0)),
                  pl.BlockSpec((tm, d), lambda i: (i, 0)),
                  pl.BlockSpec((tm, d), lambda i: (i, 1)),
                  pl.BlockSpec((W_A, d), lambda i: (0, 0)),
                  pl.BlockSpec((W_B, d), lambda i: (0, 0))],
        out_specs=pl.BlockSpec((tm, d), lambda i: (i, 0)),
        out_shape=jax.ShapeDtypeStruct((s, d), BF16),
        compiler_params=_params(("parallel",)),
        name="merge",
    )(ya, yb, gates, gates, wa, wb)


def _post_kernel(x_ref, m_ref, wo_ref, g1_ref, gn_ref, sc_ref, sh_ref, wr_ref, br_ref,
                 x1_ref, h2_ref, lg_ref):
    x1 = x_ref[...] + g1_ref[...] * jnp.dot(m_ref[...], wo_ref[...], preferred_element_type=F32)
    x1_ref[...] = x1
    y = x1 * lax.rsqrt(jnp.mean(x1 * x1, axis=-1, keepdims=True) + EPS)
    h2 = y * gn_ref[...] * (1.0 + sc_ref[...]) + sh_ref[...]
    h2_ref[...] = h2
    lg_ref[...] = jnp.dot(h2, wr_ref[...], preferred_element_type=F32,
                          precision=lax.Precision.HIGHEST) + br_ref[...]


def _post(x2, merged, wo, g1, gn, sc, sh, w_router, b_router, tm=256):
    t, d = x2.shape
    nr = w_router.shape[1]
    row = pl.BlockSpec((1, d), lambda i: (0, 0))
    tile = pl.BlockSpec((tm, d), lambda i: (i, 0))
    return pl.pallas_call(
        _post_kernel,
        grid=(t // tm,),
        in_specs=[tile, tile, pl.BlockSpec((d, d), lambda i: (0, 0)), row, row, row, row,
                  pl.BlockSpec((d, nr), lambda i: (0, 0)), pl.BlockSpec((1, nr), lambda i: (0, 0))],
        out_specs=[tile, tile, pl.BlockSpec((tm, nr), lambda i: (i, 0))],
        out_shape=[jax.ShapeDtypeStruct((t, d), F32), jax.ShapeDtypeStruct((t, d), F32),
                   jax.ShapeDtypeStruct((t, nr), F32)],
        compiler_params=_params(("parallel",)),
        name="post",
    )(x2, merged, wo, g1, gn, sc, sh, w_router, b_router)


def _moe_kernel(tok_ref, dst_ref, be_ref, nu_ref,
                h_hbm, wrow_ref, w1_ref, w3_ref, w2_ref, z_hbm,
                xbuf, ybuf, gsem, ssem):
    b = pl.program_id(0)
    bm = xbuf.shape[0]
    base = b * bm

    def gather_copy(r, t):
        return pltpu.make_async_copy(h_hbm.at[pl.ds(t, 1), :], xbuf.at[pl.ds(r, 1), :], gsem)

    def scatter_copy(r, d):
        return pltpu.make_async_copy(ybuf.at[pl.ds(r, 1), :], z_hbm.at[pl.ds(d, 1), :], ssem)

    @pl.when(b < nu_ref[0])
    def _():
        def g_start(r, carry):
            gather_copy(r, tok_ref[base + r]).start()
            return carry
        lax.fori_loop(0, bm, g_start, 0)

        def g_wait(r, carry):
            gather_copy(r, 0).wait()
            return carry
        lax.fori_loop(0, bm, g_wait, 0)

        x = xbuf[...].astype(BF16)
        a1 = jnp.dot(x, w1_ref[0].astype(BF16), preferred_element_type=F32)
        a3 = jnp.dot(x, w3_ref[0].astype(BF16), preferred_element_type=F32)
        a = (a1 * jax.nn.sigmoid(a1)) * a3
        y = jnp.dot(a.astype(BF16), w2_ref[0].astype(BF16), preferred_element_type=F32)
        ybuf[...] = y * wrow_ref[...]

        def s_start(r, carry):
            d = dst_ref[base + r]

            @pl.when(d >= 0)
            def _():
                scatter_copy(r, d).start()
            return carry
        lax.fori_loop(0, bm, s_start, 0)

        def s_wait(r, carry):
            @pl.when(dst_ref[base + r] >= 0)
            def _():
                scatter_copy(r, 0).wait()
            return carry
        lax.fori_loop(0, bm, s_wait, 0)


def _moe(h2, tok_buf, dst_buf, blk_e, n_used, w_rows, w1, w3, w2):
    t, d = h2.shape
    dff = w1.shape[2]
    bm = MOE_BM
    nb = blk_e.shape[0]
    grid_spec = pltpu.PrefetchScalarGridSpec(
        num_scalar_prefetch=4,
        grid=(nb,),
        in_specs=[
            pl.BlockSpec(memory_space=pl.ANY),
            pl.BlockSpec((bm, 1), lambda b, tok, dst, be, nu: (b, 0)),
            pl.BlockSpec((1, d, dff), lambda b, tok, dst, be, nu: (be[b], 0, 0)),
            pl.BlockSpec((1, d, dff), lambda b, tok, dst, be, nu: (be[b], 0, 0)),
            pl.BlockSpec((1, dff, d), lambda b, tok, dst, be, nu: (be[b], 0, 0)),
        ],
        out_specs=pl.BlockSpec(memory_space=pl.ANY),
        scratch_shapes=[pltpu.VMEM((bm, d), F32), pltpu.VMEM((bm, d), F32),
                        pltpu.SemaphoreType.DMA(()), pltpu.SemaphoreType.DMA(())],
    )
    return pl.pallas_call(
        _moe_kernel,
        grid_spec=grid_spec,
        out_shape=jax.ShapeDtypeStruct((TOPK_IN_GROUP * t, d), F32),
        compiler_params=pltpu.CompilerParams(dimension_semantics=("arbitrary",),
                                             vmem_limit_bytes=VMEM_LIMIT, has_side_effects=True),
        name="moe",
    )(tok_buf, dst_buf, blk_e, n_used, h2, w_rows, w1, w3, w2)


def _final_kernel(x1_ref, z_ref, g2_ref, gn_ref, o_ref):
    d = x1_ref.shape[1]
    x2 = x1_ref[...] + g2_ref[...] * (z_ref[:, 0:d] + z_ref[:, d:2 * d])
    y = x2 * lax.rsqrt(jnp.mean(x2 * x2, axis=-1, keepdims=True) + EPS)
    o_ref[...] = y * gn_ref[...]


def _final(x1, z2, g2, gn, tm=256):
    t, d = x1.shape
    row = pl.BlockSpec((1, d), lambda i: (0, 0))
    return pl.pallas_call(
        _final_kernel,
        grid=(t // tm,),
        in_specs=[pl.BlockSpec((tm, d), lambda i: (i, 0)),
                  pl.BlockSpec((tm, 2 * d), lambda i: (i, 0)), row, row],
        out_specs=pl.BlockSpec((tm, d), lambda i: (i, 0)),
        out_shape=jax.ShapeDtypeStruct((t, d), F32),
        compiler_params=_params(("parallel",)),
        name="final",
    )(x1, z2, g2, gn)


def _route(logits, t):
    gl = logits[:, :N_GROUPS]
    el = logits[:, N_GROUPS:N_GROUPS + N_EXPERTS].reshape(t, N_GROUPS, EXP_PER_GROUP)
    g_prob = jax.nn.softmax(gl, axis=-1)
    grp = jnp.argmax(gl, axis=-1).astype(I32)
    p_grp = jnp.take_along_axis(g_prob, grp[:, None], axis=-1)[:, 0]
    e_in = jnp.take_along_axis(el, grp[:, None, None], axis=1)[:, 0]
    top_v, top_i = lax.top_k(e_in, TOPK_IN_GROUP)
    p_in = jax.nn.softmax(top_v, axis=-1)
    expert = grp[:, None] * EXP_PER_GROUP + top_i.astype(I32)
    weight = p_grp[:, None] * p_in

    bm = MOE_BM
    m = t * TOPK_IN_GROUP
    e_flat = expert.reshape(m)
    w_flat = weight.reshape(m)
    order = jnp.argsort(e_flat, stable=True).astype(I32)
    e_s = e_flat[order]
    counts = jnp.bincount(e_flat, length=N_EXPERTS).astype(I32)
    start = jnp.cumsum(counts) - counts
    padded = ((counts + bm - 1) // bm) * bm
    pend = jnp.cumsum(padded)
    pstart = pend - padded
    dest = pstart[e_s] + (jnp.arange(m, dtype=I32) - start[e_s])
    nb = m // bm + N_EXPERTS
    p = nb * bm
    tok_buf = jnp.zeros((p,), I32).at[dest].set(order // TOPK_IN_GROUP)
    dst_buf = jnp.full((p,), -1, I32).at[dest].set(order)
    w_rows = jnp.zeros((p,), F32).at[dest].set(w_flat[order]).reshape(p, 1)
    blk_e = jnp.minimum(jnp.searchsorted(pend, jnp.arange(nb, dtype=I32) * bm, side='right'),
                        N_EXPERTS - 1).astype(I32)
    n_used = (pend[-1] // bm).astype(I32).reshape(1)
    return tok_buf, dst_buf, blk_e, n_used, w_rows


def kernel(x, c, w_ada, b_ada, norm_mix, w_in, t5_table, rel_table, w_up_a, w_up_b, w_o, norm_ffn,
           w_rg, b_rg, w_re, b_re, w1, w3, w2, norm_final):
    bn, s, d = x.shape
    assert bn == 1 and w_ada.shape[0] == 1
    assert s % SEL_TK == 0 and s % BAND_TQ == 0 and s % MOE_BM == 0
    t = bn * s
    x2 = x.reshape(t, d)
    n_sel = min(TOPK_MAX, s // 4)

    mod = _ada(c.reshape(d, 1), w_ada[0], b_ada[0].reshape(1, 6 * d))
    sh1, sc1, g1, sh2, sc2, g2 = [mod[:, i * d:(i + 1) * d] for i in range(6)]

    h = _rms_mod(x2, norm_mix[0].reshape(1, d), sc1, sh1, BF16)

    cols = np.cumsum([0, W_A, W_A, W_A, W_IDX_Q, D_IDX, H_IDX, W_B, W_B, W_B, d, d])
    wsl = [w_in[0][:, cols[i]:cols[i + 1]] for i in range(11)]
    wqa, wka, wva, wqi, wki, wwi, wqb, wkb, wvb, wga, wgb = wsl
    w_main = jnp.concatenate([wqa * (1.0 / math.sqrt(HD_A)), wka, wva, wqi,
                              wqb * (1.0 / math.sqrt(HD_B)), wkb, wvb], axis=1).astype(BF16)
    w_gate = jnp.concatenate([wga, wgb], axis=1).astype(BF16)
    w_idx = jnp.concatenate([wki, wwi, jnp.zeros((d, LANES - D_IDX - H_IDX), F32)], axis=1).astype(BF16)

    proj = _matmul(h, w_main, BF16, 1024, 1024, "proj_main")
    gates = _matmul(h, w_gate, F32, 1024, 1024, "proj_gate")
    idx = _matmul(h, w_idx, F32, 1024, LANES, "proj_idx")
    ki3 = idx[:, :D_IDX].astype(BF16).reshape(s // SEL_TK, SEL_TK, D_IDX)
    wi = idx[:, D_IDX:D_IDX + H_IDX] * ((H_IDX ** -0.5) * (D_IDX ** -0.5))

    mask4 = _select(proj, wi, ki3, n_sel, qi_col_block=3)
    y_a = _dsa_attn(proj, mask4, t5_table, q_col=0, k_col=1, v_col=2)
    y_b = _band_attn(proj, rel_table[0], q_col=4, k_col=5, v_col=6)

    merged = _merge(y_a, y_b, gates, w_up_a[0].astype(BF16), w_up_b[0].astype(BF16))

    nr = LANES
    w_router = jnp.concatenate([w_rg[0], w_re[0], jnp.zeros((d, nr - N_GROUPS - N_EXPERTS), F32)], axis=1)
    b_router = jnp.concatenate([b_rg[0], b_re[0], jnp.zeros((nr - N_GROUPS - N_EXPERTS,), F32)]).reshape(1, nr)
    x1, h2, logits = _post(x2, merged, w_o[0].astype(BF16), g1, norm_ffn[0].reshape(1, d), sc2, sh2,
                           w_router, b_router)
    tok_buf, dst_buf, blk_e, n_used, w_rows = _route(logits, t)
    z = _moe(h2, tok_buf, dst_buf, blk_e, n_used, w_rows, w1[0], w3[0], w2[0])
    out = _final(x1, z.reshape(t, TOPK_IN_GROUP * d), g2, norm_final.reshape(1, d))
    return out.reshape(bn, s, d)
```

```python
import functools
import math

import numpy as np
import jax
import jax.numpy as jnp
from jax import lax
from jax.experimental import pallas as pl
from jax.experimental.pallas import tpu as pltpu

F32 = jnp.float32
BF16 = jnp.bfloat16
I32 = jnp.int32

CHUNK = 64
EPS = 1e-6
H_A, HD_A = 8, 128
H_IDX, D_IDX = 16, 64
TOPK_MAX = 256
N_T5_BUCKETS = 32
T5_MAX_DIST = 1024
H_B, HD_B = 8, 128
N_LEFT_CHUNKS = 8
REL_CLIP = 128
N_GROUPS = 8
EXP_PER_GROUP = 8
N_EXPERTS = N_GROUPS * EXP_PER_GROUP
TOPK_IN_GROUP = 2

W_A = H_A * HD_A
W_B = H_B * HD_B
W_IDX_Q = H_IDX * D_IDX

NEG = -1e30
INT_MIN = -(2 ** 31)
LOG2E = math.log2(math.e)

LANES = 128
VMEM_LIMIT = 56 * 1024 * 1024

SEL_TQ = 128
SEL_TK = 512
ATT_TQ = 256
BAND_TQ = 256
MOE_BM = 256


def _params(sem, vmem=VMEM_LIMIT):
    return pltpu.CompilerParams(dimension_semantics=sem, vmem_limit_bytes=vmem)


def _toeplitz(g, nrows, ncols):
    n = ncols + nrows - 1
    assert g.shape[-1] == n
    u = jnp.concatenate([g, jnp.zeros(g.shape[:-1] + (1,), g.dtype)], axis=-1)
    flat = jnp.tile(u, (1,) * (g.ndim - 1) + (nrows,))[..., :nrows * n]
    return flat.reshape(g.shape[:-1] + (nrows, n))[..., nrows - 1:]


def _ada_kernel(c_ref, w_ref, b_ref, o_ref, *, kc):
    d = w_ref.shape[0]
    tn = w_ref.shape[1]

    def body(k, acc):
        r0 = pl.multiple_of(k * kc, kc)
        cc = c_ref[pl.ds(r0, kc), :]
        ca = cc * jax.nn.sigmoid(cc)
        return acc + jnp.sum(w_ref[pl.ds(r0, kc), :] * ca, axis=0, keepdims=True)

    acc = lax.fori_loop(0, d // kc, body, jnp.zeros((1, tn), F32))
    o_ref[...] = acc + b_ref[...]


def _ada(c_col, w, b_row, tn=1024, kc=256):
    d, n = w.shape
    return pl.pallas_call(
        functools.partial(_ada_kernel, kc=kc),
        grid=(n // tn,),
        in_specs=[pl.BlockSpec((d, 1), lambda j: (0, 0)),
                  pl.BlockSpec((d, tn), lambda j: (0, j)),
                  pl.BlockSpec((1, tn), lambda j: (0, j))],
        out_specs=pl.BlockSpec((1, tn), lambda j: (0, j)),
        out_shape=jax.ShapeDtypeStruct((1, n), F32),
        compiler_params=_params(("arbitrary",)),
        name="ada",
    )(c_col, w, b_row)


def _rms_mod_kernel(x_ref, g_ref, sc_ref, sh_ref, o_ref):
    x = x_ref[...]
    y = x * lax.rsqrt(jnp.mean(x * x, axis=-1, keepdims=True) + EPS)
    o_ref[...] = (y * g_ref[...] * (1.0 + sc_ref[...]) + sh_ref[...]).astype(o_ref.dtype)


def _rms_mod(x2, g, sc, sh, out_dtype, tm=512):
    t, d = x2.shape
    row = pl.BlockSpec((1, d), lambda i: (0, 0))
    return pl.pallas_call(
        _rms_mod_kernel,
        grid=(t // tm,),
        in_specs=[pl.BlockSpec((tm, d), lambda i: (i, 0)), row, row, row],
        out_specs=pl.BlockSpec((tm, d), lambda i: (i, 0)),
        out_shape=jax.ShapeDtypeStruct((t, d), out_dtype),
        compiler_params=_params(("parallel",)),
        name="rms_mod",
    )(x2, g, sc, sh)


def _mm_kernel(a_ref, b_ref, o_ref):
    o_ref[...] = jnp.dot(a_ref[...], b_ref[...], preferred_element_type=F32).astype(o_ref.dtype)


def _matmul(a, b, out_dtype, tm, tn, name):
    m, k = a.shape
    n = b.shape[1]
    return pl.pallas_call(
        _mm_kernel,
        grid=(m // tm, n // tn),
        in_specs=[pl.BlockSpec((tm, k), lambda i, j: (i, 0)),
                  pl.BlockSpec((k, tn), lambda i, j: (0, j))],
        out_specs=pl.BlockSpec((tm, tn), lambda i, j: (i, j)),
        out_shape=jax.ShapeDtypeStruct((m, n), out_dtype),
        compiler_params=_params(("parallel", "arbitrary")),
        name=name,
    )(a, b)


def _sortable(x):
    bits = pltpu.bitcast(x, I32)
    return bits ^ ((bits >> 31) & 0x7FFFFFFF)


def _select_kernel(qi_ref, wi_ref, kit_ref, mask_ref, key_sc, qh_sc, wb_sc, *, n_sel):
    nkt, tq, tk = key_sc.shape
    nsub = tk // LANES
    ngrp = -(-n_sel // LANES)
    i = pl.program_id(0)
    q0 = i * tq
    nvis = (q0 + tq + tk - 1) // tk

    for h in range(H_IDX):
        qh_sc[h] = qi_ref[:, h * D_IDX:(h + 1) * D_IDX]
        wb_sc[h] = jnp.broadcast_to(wi_ref[:, h:h + 1], (tq, LANES))

    row = lax.broadcasted_iota(I32, (tq, 1), 0) + q0
    limit = (row // CHUNK + 1) * CHUNK
    lane = lax.broadcasted_iota(I32, (tq, LANES), 1)

    def score_tile(kt, gmax):
        k_t = kit_ref[kt]
        accs = [jnp.zeros((tq, LANES), F32) for _ in range(nsub)]
        for h in range(H_IDX):
            s = jnp.dot(qh_sc[h], k_t, preferred_element_type=F32)
            wb = wb_sc[h]
            for a in range(nsub):
                accs[a] = accs[a] + wb * jnp.maximum(s[:, a * LANES:(a + 1) * LANES], 0.0)
        gmax = list(gmax)
        for a in range(nsub):
            col = lane + (kt * tk + a * LANES)
            key = jnp.where(col < limit, _sortable(accs[a]), INT_MIN)
            key_sc[kt, :, a * LANES:(a + 1) * LANES] = key
            gmax[a % ngrp] = jnp.maximum(gmax[a % ngrp], key)
        return tuple(gmax)

    gmax = lax.fori_loop(0, nvis, score_tile,
                         tuple(jnp.full((tq, LANES), INT_MIN, I32) for _ in range(ngrp)))

    def count(pred):
        def body(kt, c):
            kk = key_sc[kt]
            col = lax.broadcasted_iota(I32, (tq, tk), 1) + kt * tk
            m = pred(kk, col).astype(I32)
            part = m[:, 0:LANES]
            for a in range(1, nsub):
                part = part + m[:, a * LANES:(a + 1) * LANES]
            return c + part
        c = lax.fori_loop(0, nvis, body, jnp.zeros((tq, LANES), I32))
        return jnp.sum(c, axis=1, keepdims=True)

    gmin, ghi = gmax[0], gmax[0]
    for g in gmax[1:]:
        gmin = jnp.minimum(gmin, g)
        ghi = jnp.maximum(ghi, g)
    lo0 = jnp.maximum(jnp.min(gmin, axis=1, keepdims=True), INT_MIN + 1)
    hi0 = jnp.max(ghi, axis=1, keepdims=True)
    few = limit < n_sel
    lo0 = jnp.where(few, INT_MIN + 1, lo0)
    hi0 = jnp.where(few, INT_MIN + 1, hi0)
    unknown = jnp.full((tq, 1), 2 ** 30, I32)

    def bis_cond(st):
        lo, hi, _, _ = st
        return jnp.max((lo < hi).astype(I32)) > 0

    def bis_body(st):
        lo, hi, c_lo, c_hi1 = st
        active = lo < hi
        mid = (lo | hi) - ((lo ^ hi) >> 1)
        c = count(lambda kk, col: kk >= mid)
        up = active & (c >= n_sel)
        dn = active & (c < n_sel)
        hit = active & (c == n_sel)
        lo = jnp.where(up, mid, lo)
        c_lo = jnp.where(up, c, c_lo)
        hi = jnp.where(dn, mid - 1, jnp.where(hit, mid, hi))
        c_hi1 = jnp.where(dn, c, c_hi1)
        return lo, hi, c_lo, c_hi1

    thr, _, n_ge, n_gt = lax.while_loop(bis_cond, bis_body,
                                        (lo0, hi0, unknown, jnp.zeros((tq, 1), I32)))

    excess = (n_ge > n_sel) & jnp.logical_not(few)
    need = n_sel - n_gt
    ncol = nkt * tk

    def tie_cut():
        nbits = max(1, int(math.ceil(math.log2(ncol))))

        def cut_body(b, p):
            cand = p | jnp.left_shift(jnp.int32(1), nbits - 1 - b)
            cnt = count(lambda kk, col: (kk == thr) & (col < cand))
            return jnp.where(cnt < need, cand, p)

        p = lax.fori_loop(0, nbits, cut_body, jnp.zeros((tq, 1), I32))
        return p + 1

    cut = lax.cond(jnp.max(excess.astype(I32)) > 0, tie_cut, lambda: jnp.full((tq, 1), ncol, I32))
    cut = jnp.where(excess, cut, ncol)

    def write_tile(kt, carry):
        kk = key_sc[kt]
        col = lax.broadcasted_iota(I32, (tq, tk), 1) + kt * tk
        sel = (kk > thr) | ((kk == thr) & (col < cut))
        mask_ref[kt] = jnp.where(sel, 0.0, NEG).astype(mask_ref.dtype)
        return carry

    lax.fori_loop(0, nvis, write_tile, 0)

    def fill_tile(kt, carry):
        mask_ref[kt] = jnp.full((tq, tk), NEG, mask_ref.dtype)
        return carry

    lax.fori_loop(nvis, nkt, fill_tile, 0)


def _select(proj, wi, kit3, n_sel, qi_col_block):
    s = proj.shape[0]
    nkt, _, tk = kit3.shape
    tq = SEL_TQ
    nqb = s // tq
    assert -(-n_sel // LANES) <= tk // LANES
    return pl.pallas_call(
        functools.partial(_select_kernel, n_sel=n_sel),
        grid=(nqb,),
        in_specs=[pl.BlockSpec((tq, W_IDX_Q), lambda i: (i, qi_col_block)),
                  pl.BlockSpec((tq, H_IDX), lambda i: (i, 0)),
                  pl.BlockSpec((nkt, D_IDX, tk), lambda i: (0, 0, 0))],
        out_specs=pl.BlockSpec((None, nkt, tq, tk), lambda i: (i, 0, 0, 0)),
        out_shape=jax.ShapeDtypeStruct((nqb, nkt, tq, tk), BF16),
        scratch_shapes=[pltpu.VMEM((nkt, tq, tk), I32),
                        pltpu.VMEM((H_IDX, tq, D_IDX), BF16),
                        pltpu.VMEM((H_IDX, tq, LANES), F32)],
        compiler_params=_params(("arbitrary",)),
        name="dsa_select",
    )(proj, wi, kit3)


def _dsa_attn_kernel(qb_ref, kb_ref, first_ref, last_ref, slab_ref,
                     q_ref, kt_ref, v_ref, mask_ref, band_ref, far_ref, o_ref,
                     m_sc, l_sc, acc_sc):
    step = pl.program_id(0)
    tk = kt_ref.shape[1]
    nslab = tk // LANES
    slab0 = slab_ref[step]

    @pl.when(first_ref[step] == 1)
    def _():
        m_sc[...] = jnp.full(m_sc.shape, NEG, F32)
        l_sc[...] = jnp.zeros(l_sc.shape, F32)
        acc_sc[...] = jnp.zeros(acc_sc.shape, F32)

    def tile(near):
        mask = jnp.concatenate([mask_ref[r] for r in range(mask_ref.shape[0])], axis=0).astype(F32)
        for h in range(H_A):
            hs = slice(h * HD_A, (h + 1) * HD_A)
            s = jnp.dot(q_ref[:, hs], kt_ref[hs, :], preferred_element_type=F32) + mask
            if near:
                s = s + jnp.concatenate([band_ref[h, slab0 + a] for a in range(nslab)], axis=1)
                shift = 0.0
            else:
                shift = far_ref[h]
            m_old = m_sc[h]
            m_new = jnp.maximum(m_old, jnp.max(s, axis=1, keepdims=True) + shift)
            alpha = jnp.exp2(m_old - m_new)
            p = jnp.exp2(s - (m_new - shift))
            l_sc[h] = alpha * l_sc[h] + jnp.sum(p, axis=1, keepdims=True)
            m_sc[h] = m_new
            pv = jnp.dot(p.astype(BF16), v_ref[:, hs], preferred_element_type=F32)
            acc_sc[:, hs] = alpha * acc_sc[:, hs] + pv

    @pl.when(slab0 >= 0)
    def _():
        tile(True)

    @pl.when(slab0 < 0)
    def _():
        tile(False)

    @pl.when(last_ref[step] == 1)
    def _():
        for h in range(H_A):
            hs = slice(h * HD_A, (h + 1) * HD_A)
            o_ref[:, hs] = (acc_sc[:, hs] / l_sc[h]).astype(o_ref.dtype)


def _t5_bucket_np(rel):
    nb = N_T5_BUCKETS // 2
    ret = (rel > 0).astype(np.int32) * nb
    n = np.abs(rel)
    max_exact = nb // 2
    nf = np.maximum(n, 1).astype(np.float32)
    large = max_exact + (np.log(nf / np.float32(max_exact)) / np.float32(math.log(T5_MAX_DIST / max_exact))
                         * np.float32(nb - max_exact)).astype(np.int32)
    large = np.minimum(large, nb - 1)
    return ret + np.where(n < max_exact, n, large)


def _dsa_plan(s, tq, tk):
    nqb = s // tq
    d_all = np.arange(-(s - 1), CHUNK, dtype=np.int64)
    b_all = _t5_bucket_np(d_all)
    far_bucket = int(b_all[0])
    varying = np.nonzero(b_all != far_bucket)[0]
    d_lo = int(d_all[varying[0]])
    o_min = min(int(math.ceil((d_lo - (tk - 1)) / LANES)) * LANES, 0)
    n_slabs = (-o_min + tk) // LANES
    qb, kb, first, last, slab = [], [], [], [], []
    for i in range(nqb):
        q0 = i * tq
        nvis = (q0 + tq + tk - 1) // tk
        for j in range(nvis):
            o = j * tk - q0
            qb.append(i)
            kb.append(j)
            first.append(1 if j == 0 else 0)
            last.append(1 if j == nvis - 1 else 0)
            slab.append(-1 if o < o_min else (o - o_min) // LANES)
    u = np.arange(n_slabs * LANES + tq - 1)
    g_bucket = _t5_bucket_np(np.clip(o_min + u - (tq - 1), -(s - 1), None))
    plan = tuple(np.asarray(a, np.int32) for a in (qb, kb, first, last, slab))
    return plan, g_bucket, far_bucket, n_slabs


def _dsa_attn(proj, kt, mask4, t5_table, q_col, v_col):
    s = proj.shape[0]
    tq, tk = ATT_TQ, SEL_TK
    (qb, kb, first, last, slab), g_bucket, far_bucket, n_slabs = _dsa_plan(s, tq, tk)
    nsteps = len(qb)
    t5l = t5_table * LOG2E
    g = jnp.transpose(t5l[g_bucket], (1, 0))
    band = _toeplitz(g, tq, n_slabs * LANES)
    band = jnp.transpose(band.reshape(H_A, tq, n_slabs, LANES), (0, 2, 1, 3))
    far = t5l[far_bucket]
    rq = tq // SEL_TQ
    grid_spec = pltpu.PrefetchScalarGridSpec(
        num_scalar_prefetch=5,
        grid=(nsteps,),
        in_specs=[
            pl.BlockSpec((tq, W_A), lambda t, qb, kb, f, l, sl: (qb[t], q_col)),
            pl.BlockSpec((W_A, tk), lambda t, qb, kb, f, l, sl: (0, kb[t])),
            pl.BlockSpec((tk, W_A), lambda t, qb, kb, f, l, sl: (kb[t], v_col)),
            pl.BlockSpec((rq, None, SEL_TQ, tk), lambda t, qb, kb, f, l, sl: (qb[t], kb[t], 0, 0)),
            pl.BlockSpec((H_A, n_slabs, tq, LANES), lambda t, qb, kb, f, l, sl: (0, 0, 0, 0)),
            pl.BlockSpec(memory_space=pltpu.SMEM),
        ],
        out_specs=pl.BlockSpec((tq, W_A), lambda t, qb, kb, f, l, sl: (qb[t], 0)),
        scratch_shapes=[pltpu.VMEM((H_A, tq, 1), F32),
                        pltpu.VMEM((H_A, tq, 1), F32),
                        pltpu.VMEM((tq, W_A), F32)],
    )
    return pl.pallas_call(
        _dsa_attn_kernel,
        grid_spec=grid_spec,
        out_shape=jax.ShapeDtypeStruct((s, W_A), BF16),
        compiler_params=_params(("arbitrary",)),
        name="dsa_attn",
    )(jnp.asarray(qb), jnp.asarray(kb), jnp.asarray(first), jnp.asarray(last), jnp.asarray(slab),
      proj, kt, proj, mask4, band, far)


def _band_kernel(q_ref, k0_ref, k1_ref, k2_ref, v0_ref, v1_ref, v2_ref, bias_ref, o_ref):
    i = pl.program_id(0)
    tq = q_ref.shape[0]
    wk = 3 * tq
    col = lax.broadcasted_iota(I32, (tq, wk), 1)
    start_mask = jnp.where(col + (i - 2) * tq >= 0, 0.0, NEG)
    for h in range(H_B):
        hs = slice(h * HD_B, (h + 1) * HD_B)
        kc = jnp.concatenate([k0_ref[hs, :], k1_ref[hs, :], k2_ref[hs, :]], axis=1)
        vc = jnp.concatenate([v0_ref[:, hs], v1_ref[:, hs], v2_ref[:, hs]], axis=0)
        s = jnp.dot(q_ref[:, hs], kc, preferred_element_type=F32) + bias_ref[h] + start_mask
        m = jnp.max(s, axis=1, keepdims=True)
        p = jnp.exp2(s - m)
        l = jnp.sum(p, axis=1, keepdims=True)
        pv = jnp.dot(p.astype(BF16), vc, preferred_element_type=F32)
        o_ref[:, hs] = (pv / l).astype(o_ref.dtype)


def _band_bias(rel_table, tq):
    assert 2 * tq >= N_LEFT_CHUNKS * CHUNK and tq % CHUNK == 0
    wk = 3 * tq
    x = np.arange(wk + tq - 1)
    idx = np.clip(2 * tq + (tq - 1) - x, -REL_CLIP, REL_CLIP) + REL_CLIP
    bias = _toeplitz(rel_table[:, idx] * LOG2E, tq, wk)
    r = np.arange(tq)[:, None]
    c = np.arange(wk)[None, :]
    dchunk = (c - 2 * tq) // CHUNK - r // CHUNK
    in_band = (dchunk <= 0) & (dchunk >= -N_LEFT_CHUNKS)
    return jnp.where(jnp.asarray(in_band)[None], bias, NEG).astype(F32)


def _band_attn(proj, kt, rel_table, q_col, v_col):
    s = proj.shape[0]
    tq = BAND_TQ
    bias = _band_bias(rel_table, tq)

    def kspec(back):
        return pl.BlockSpec((W_B, tq), lambda i: (0, jnp.maximum(i - back, 0)))

    def vspec(back):
        return pl.BlockSpec((tq, W_B), lambda i: (jnp.maximum(i - back, 0), v_col))

    return pl.pallas_call(
        _band_kernel,
        grid=(s // tq,),
        in_specs=[pl.BlockSpec((tq, W_B), lambda i: (i, q_col)),
                  kspec(2), kspec(1), kspec(0), vspec(2), vspec(1), vspec(0),
                  pl.BlockSpec((H_B, tq, 3 * tq), lambda i: (0, 0, 0))],
        out_specs=pl.BlockSpec((tq, W_B), lambda i: (i, 0)),
        out_shape=jax.ShapeDtypeStruct((s, W_B), BF16),
        compiler_params=_params(("parallel",)),
        name="band_attn",
    )(proj, kt, kt, kt, proj, proj, proj, bias)


def _merge_kernel(ya_ref, yb_ref, ga_ref, gb_ref, wa_ref, wb_ref, o_ref):
    ua = jnp.dot(ya_ref[...], wa_ref[...], preferred_element_type=F32)
    ub = jnp.dot(yb_ref[...], wb_ref[...], preferred_element_type=F32)
    o_ref[...] = (jax.nn.sigmoid(ga_ref[...]) * ua + jax.nn.sigmoid(gb_ref[...]) * ub).astype(o_ref.dtype)


def _merge(ya, yb, gates, wa, wb, tm=256):
    s, d = ya.shape[0], wa.shape[1]
    return pl.pallas_call(
        _merge_kernel,
        grid=(s // tm,),
        in_specs=[pl.BlockSpec((tm, W_A), lambda i: (i, 0)),
                  pl.BlockSpec((tm, W_B), lambda i: (i, 0)),
                  pl.BlockSpec((tm, d), lambda i: (i, 0)),
                  pl.BlockSpec((tm, d), lambda i: (i, 1)),
                  pl.BlockSpec((W_A, d), lambda i: (0, 0)),
                  pl.BlockSpec((W_B, d), lambda i: (0, 0))],
        out_specs=pl.BlockSpec((tm, d), lambda i: (i, 0)),
        out_shape=jax.ShapeDtypeStruct((s, d), BF16),
        compiler_params=_params(("parallel",)),
        name="merge",
    )(ya, yb, gates, gates, wa, wb)


def _post_kernel(x_ref, m_ref, wo_ref, g1_ref, gn_ref, sc_ref, sh_ref, wr_ref, br_ref,
                 x1_ref, h2_ref, lg_ref):
    x1 = x_ref[...] + g1_ref[...] * jnp.dot(m_ref[...], wo_ref[...], preferred_element_type=F32)
    x1_ref[...] = x1
    y = x1 * lax.rsqrt(jnp.mean(x1 * x1, axis=-1, keepdims=True) + EPS)
    h2 = y * gn_ref[...] * (1.0 + sc_ref[...]) + sh_ref[...]
    h2_ref[...] = h2
    lg_ref[...] = jnp.dot(h2, wr_ref[...], preferred_element_type=F32,
                          precision=lax.Precision.HIGHEST) + br_ref[...]


def _post(x2, merged, wo, g1, gn, sc, sh, w_router, b_router, tm=256):
    t, d = x2.shape
    nr = w_router.shape[1]
    row = pl.BlockSpec((1, d), lambda i: (0, 0))
    tile = pl.BlockSpec((tm, d), lambda i: (i, 0))
    return pl.pallas_call(
        _post_kernel,
        grid=(t // tm,),
        in_specs=[tile, tile, pl.BlockSpec((d, d), lambda i: (0, 0)), row, row, row, row,
                  pl.BlockSpec((d, nr), lambda i: (0, 0)), pl.BlockSpec((1, nr), lambda i: (0, 0))],
        out_specs=[tile, tile, pl.BlockSpec((tm, nr), lambda i: (i, 0))],
        out_shape=[jax.ShapeDtypeStruct((t, d), F32), jax.ShapeDtypeStruct((t, d), F32),
                   jax.ShapeDtypeStruct((t, nr), F32)],
        compiler_params=_params(("parallel",)),
        name="post",
    )(x2, merged, wo, g1, gn, sc, sh, w_router, b_router)


def _moe_kernel(tok_ref, dst_ref, be_ref, nu_ref,
                h_hbm, wrow_ref, w1_ref, w3_ref, w2_ref, z_hbm,
                xbuf, ybuf, w1b, w3b, w2b, gsem, ssem):
    b = pl.program_id(0)
    bm = xbuf.shape[1]
    nu = nu_ref[0]
    slot = b % 2

    def gather_copy(sl, r, t):
        return pltpu.make_async_copy(h_hbm.at[pl.ds(t, 1), :], xbuf.at[sl, pl.ds(r, 1), :], gsem.at[sl])

    def scatter_copy(sl, r, d):
        return pltpu.make_async_copy(ybuf.at[sl, pl.ds(r, 1), :], z_hbm.at[pl.ds(d, 1), :], ssem.at[sl])

    def gather_start(blk, sl):
        def body(r, carry):
            gather_copy(sl, r, tok_ref[blk * bm + r]).start()
            return carry
        lax.fori_loop(0, bm, body, 0, unroll=8)

    def gather_wait(sl):
        def body(r, carry):
            gather_copy(sl, r, 0).wait()
            return carry
        lax.fori_loop(0, bm, body, 0, unroll=8)

    def scatter_start(blk, sl):
        def body(r, carry):
            scatter_copy(sl, r, dst_ref[blk * bm + r]).start()
            return carry
        lax.fori_loop(0, bm, body, 0, unroll=8)

    def scatter_wait(sl):
        def body(r, carry):
            scatter_copy(sl, r, 0).wait()
            return carry
        lax.fori_loop(0, bm, body, 0, unroll=8)

    @pl.when(b < nu)
    def _():
        @pl.when(b == 0)
        def _():
            gather_start(0, 0)
            ybuf[...] = jnp.zeros(ybuf.shape, F32)
            n_real = z_hbm.shape[0] - 2 * bm
            for sl in range(2):
                spare = pltpu.make_async_copy(ybuf.at[sl], z_hbm.at[pl.ds(n_real + sl * bm, bm), :], ssem.at[sl])
                spare.start()
                spare.wait()

        @pl.when(b + 1 < nu)
        def _():
            gather_start(b + 1, 1 - slot)

        @pl.when((b == 0) | (be_ref[b] != be_ref[jnp.maximum(b - 1, 0)]))
        def _():
            w1b[...] = w1_ref[0].astype(BF16)
            w3b[...] = w3_ref[0].astype(BF16)
            w2b[...] = w2_ref[0].astype(BF16)

        gather_wait(slot)

        @pl.when(b >= 2)
        def _():
            scatter_wait(slot)

        x = xbuf[slot].astype(BF16)
        a1 = jnp.dot(x, w1b[...], preferred_element_type=F32)
        a3 = jnp.dot(x, w3b[...], preferred_element_type=F32)
        a = (a1 * jax.nn.sigmoid(a1)) * a3
        y = jnp.dot(a.astype(BF16), w2b[...], preferred_element_type=F32)
        ybuf[slot] = y * wrow_ref[...]
        scatter_start(b, slot)

        @pl.when(b == nu - 1)
        def _():
            scatter_wait(slot)

            @pl.when(b >= 1)
            def _():
                scatter_wait(1 - slot)


def _moe(h2, tok_buf, dst_buf, blk_e, n_used, w_rows, w1, w3, w2):
    t, d = h2.shape
    dff = w1.shape[2]
    bm = MOE_BM
    nb = blk_e.shape[0]
    grid_spec = pltpu.PrefetchScalarGridSpec(
        num_scalar_prefetch=4,
        grid=(nb,),
        in_specs=[
            pl.BlockSpec(memory_space=pl.ANY),
            pl.BlockSpec((bm, 1), lambda b, tok, dst, be, nu: (b, 0)),
            pl.BlockSpec((1, d, dff), lambda b, tok, dst, be, nu: (be[b], 0, 0)),
            pl.BlockSpec((1, d, dff), lambda b, tok, dst, be, nu: (be[b], 0, 0)),
            pl.BlockSpec((1, dff, d), lambda b, tok, dst, be, nu: (be[b], 0, 0)),
        ],
        out_specs=pl.BlockSpec(memory_space=pl.ANY),
        scratch_shapes=[pltpu.VMEM((2, bm, d), F32), pltpu.VMEM((2, bm, d), F32),
                        pltpu.VMEM((d, dff), BF16), pltpu.VMEM((d, dff), BF16), pltpu.VMEM((dff, d), BF16),
                        pltpu.SemaphoreType.DMA((2,)), pltpu.SemaphoreType.DMA((2,))],
    )
    return pl.pallas_call(
        _moe_kernel,
        grid_spec=grid_spec,
        out_shape=jax.ShapeDtypeStruct((TOPK_IN_GROUP * t + 2 * bm, d), F32),
        compiler_params=pltpu.CompilerParams(dimension_semantics=("arbitrary",),
                                             vmem_limit_bytes=VMEM_LIMIT, has_side_effects=True),
        name="moe",
    )(tok_buf, dst_buf, blk_e, n_used, h2, w_rows, w1, w3, w2)


def _final_kernel(x1_ref, z_ref, g2_ref, gn_ref, o_ref):
    d = x1_ref.shape[1]
    x2 = x1_ref[...] + g2_ref[...] * (z_ref[:, 0:d] + z_ref[:, d:2 * d])
    y = x2 * lax.rsqrt(jnp.mean(x2 * x2, axis=-1, keepdims=True) + EPS)
    o_ref[...] = y * gn_ref[...]


def _final(x1, z2, g2, gn, tm=256):
    t, d = x1.shape
    row = pl.BlockSpec((1, d), lambda i: (0, 0))
    return pl.pallas_call(
        _final_kernel,
        grid=(t // tm,),
        in_specs=[pl.BlockSpec((tm, d), lambda i: (i, 0)),
                  pl.BlockSpec((tm, 2 * d), lambda i: (i, 0)), row, row],
        out_specs=pl.BlockSpec((tm, d), lambda i: (i, 0)),
        out_shape=jax.ShapeDtypeStruct((t, d), F32),
        compiler_params=_params(("parallel",)),
        name="final",
    )(x1, z2, g2, gn)


def _route(logits, t):
    gl = logits[:, :N_GROUPS]
    el = logits[:, N_GROUPS:N_GROUPS + N_EXPERTS].reshape(t, N_GROUPS, EXP_PER_GROUP)
    g_prob = jax.nn.softmax(gl, axis=-1)
    grp = jnp.argmax(gl, axis=-1).astype(I32)
    p_grp = jnp.take_along_axis(g_prob, grp[:, None], axis=-1)[:, 0]
    e_in = jnp.take_along_axis(el, grp[:, None, None], axis=1)[:, 0]
    top_v, top_i = lax.top_k(e_in, TOPK_IN_GROUP)
    p_in = jax.nn.softmax(top_v, axis=-1)
    expert = grp[:, None] * EXP_PER_GROUP + top_i.astype(I32)
    weight = p_grp[:, None] * p_in

    bm = MOE_BM
    m = t * TOPK_IN_GROUP
    e_flat = expert.reshape(m)
    w_flat = weight.reshape(m)
    order = jnp.argsort(e_flat, stable=True).astype(I32)
    e_s = e_flat[order]
    counts = jnp.bincount(e_flat, length=N_EXPERTS).astype(I32)
    start = jnp.cumsum(counts) - counts
    padded = ((counts + bm - 1) // bm) * bm
    pend = jnp.cumsum(padded)
    pstart = pend - padded
    dest = pstart[e_s] + (jnp.arange(m, dtype=I32) - start[e_s])
    nb = m // bm + N_EXPERTS
    p = nb * bm
    tok_buf = jnp.zeros((p,), I32).at[dest].set(order // TOPK_IN_GROUP)
    spare = m + jnp.arange(p, dtype=I32) % (2 * bm)
    dst_buf = spare.at[dest].set(order)
    w_rows = jnp.zeros((p,), F32).at[dest].set(w_flat[order]).reshape(p, 1)
    blk_e = jnp.minimum(jnp.searchsorted(pend, jnp.arange(nb, dtype=I32) * bm, side='right'),
                        N_EXPERTS - 1).astype(I32)
    n_used = (pend[-1] // bm).astype(I32).reshape(1)
    return tok_buf, dst_buf, blk_e, n_used, w_rows


def kernel(x, c, w_ada, b_ada, norm_mix, w_in, t5_table, rel_table, w_up_a, w_up_b, w_o, norm_ffn,
           w_rg, b_rg, w_re, b_re, w1, w3, w2, norm_final):
    bn, s, d = x.shape
    assert bn == 1 and w_ada.shape[0] == 1
    assert s % 1024 == 0
    t = bn * s
    x2 = x.reshape(t, d)
    n_sel = min(TOPK_MAX, s // 4)

    mod = _ada(c.reshape(d, 1), w_ada[0], b_ada[0].reshape(1, 6 * d))
    sh1, sc1, g1, sh2, sc2, g2 = [mod[:, i * d:(i + 1) * d] for i in range(6)]

    h = _rms_mod(x2, norm_mix[0].reshape(1, d), sc1, sh1, BF16)

    cols = np.cumsum([0, W_A, W_A, W_A, W_IDX_Q, D_IDX, H_IDX, W_B, W_B, W_B, d, d])
    wsl = [w_in[0][:, cols[i]:cols[i + 1]] for i in range(11)]
    wqa, wka, wva, wqi, wki, wwi, wqb, wkb, wvb, wga, wgb = wsl
    w_main = jnp.concatenate([wqa * (LOG2E / math.sqrt(HD_A)), wka, wva, wqi,
                              wqb * (LOG2E / math.sqrt(HD_B)), wkb, wvb], axis=1).astype(BF16)
    w_gate = jnp.concatenate([wga, wgb], axis=1).astype(BF16)
    w_idx = jnp.concatenate([wki, wwi, jnp.zeros((d, LANES - D_IDX - H_IDX), F32)], axis=1).astype(BF16)

    proj = _matmul(h, w_main, BF16, 1024, 1024, "proj_main")
    gates = _matmul(h, w_gate, F32, 1024, 1024, "proj_gate")
    idx = _matmul(h, w_idx, F32, 1024, LANES, "proj_idx")
    nkt = s // SEL_TK
    kit3 = jnp.transpose(idx[:, :D_IDX].astype(BF16).reshape(nkt, SEL_TK, D_IDX), (0, 2, 1))
    wi = idx[:, D_IDX:D_IDX + H_IDX] * ((H_IDX ** -0.5) * (D_IDX ** -0.5))
    kt_a = jnp.transpose(proj[:, W_A:2 * W_A])
    kt_b = jnp.transpose(proj[:, 5 * W_A:6 * W_A])

    mask4 = _select(proj, wi, kit3, n_sel, qi_col_block=3)
    y_a = _dsa_attn(proj, kt_a, mask4, t5_table, q_col=0, v_col=2)
    y_b = _band_attn(proj, kt_b, rel_table[0], q_col=4, v_col=6)

    merged = _merge(y_a, y_b, gates, w_up_a[0].astype(BF16), w_up_b[0].astype(BF16))

    nr = LANES
    w_router = jnp.concatenate([w_rg[0], w_re[0], jnp.zeros((d, nr - N_GROUPS - N_EXPERTS), F32)], axis=1)
    b_router = jnp.concatenate([b_rg[0], b_re[0], jnp.zeros((nr - N_GROUPS - N_EXPERTS,), F32)]).reshape(1, nr)
    x1, h2, logits = _post(x2, merged, w_o[0].astype(BF16), g1, norm_ffn[0].reshape(1, d), sc2, sh2,
                           w_router, b_router)
    tok_buf, dst_buf, blk_e, n_used, w_rows = _route(logits, t)
    z = _moe(h2, tok_buf, dst_buf, blk_e, n_used, w_rows, w1[0], w3[0], w2[0])
    out = _final(x1, z.reshape(t + MOE_BM, TOPK_IN_GROUP * d), g2, norm_final.reshape(1, d))
    return out.reshape(bn, s, d)
```

```python
import functools
import math

import numpy as np
import jax
import jax.numpy as jnp
from jax import lax
from jax.experimental import pallas as pl
from jax.experimental.pallas import tpu as pltpu

F32 = jnp.float32
BF16 = jnp.bfloat16
I32 = jnp.int32

CHUNK = 64
EPS = 1e-6
H_A, HD_A = 8, 128
H_IDX, D_IDX = 16, 64
TOPK_MAX = 256
N_T5_BUCKETS = 32
T5_MAX_DIST = 1024
H_B, HD_B = 8, 128
N_LEFT_CHUNKS = 8
REL_CLIP = 128
N_GROUPS = 8
EXP_PER_GROUP = 8
N_EXPERTS = N_GROUPS * EXP_PER_GROUP
TOPK_IN_GROUP = 2

W_A = H_A * HD_A
W_B = H_B * HD_B
W_IDX_Q = H_IDX * D_IDX

NEG = -1e30
INT_MIN = -(2 ** 31)
KEY_FMAX = 0x7F7FFFFF
LOG2E = math.log2(math.e)

LANES = 128
VMEM_LIMIT = 56 * 1024 * 1024

SEL_TQ = 128
SEL_TK = 512
ATT_TQ = 256
BAND_TQ = 256
MOE_BM = 256


def _params(sem, vmem=VMEM_LIMIT):
    return pltpu.CompilerParams(dimension_semantics=sem, vmem_limit_bytes=vmem)


def _toeplitz(g, nrows, ncols):
    n = ncols + nrows - 1
    assert g.shape[-1] == n
    u = jnp.concatenate([g, jnp.zeros(g.shape[:-1] + (1,), g.dtype)], axis=-1)
    flat = jnp.tile(u, (1,) * (g.ndim - 1) + (nrows,))[..., :nrows * n]
    return flat.reshape(g.shape[:-1] + (nrows, n))[..., nrows - 1:]


def _ada_kernel(c_ref, w_ref, b_ref, o_ref, *, kc):
    d = w_ref.shape[0]
    tn = w_ref.shape[1]

    def body(k, acc):
        r0 = pl.multiple_of(k * kc, kc)
        cc = c_ref[pl.ds(r0, kc), :]
        ca = cc * jax.nn.sigmoid(cc)
        return acc + jnp.sum(w_ref[pl.ds(r0, kc), :] * ca, axis=0, keepdims=True)

    acc = lax.fori_loop(0, d // kc, body, jnp.zeros((1, tn), F32))
    o_ref[...] = acc + b_ref[...]


def _ada(c_col, w, b_row, tn=1024, kc=256):
    d, n = w.shape
    return pl.pallas_call(
        functools.partial(_ada_kernel, kc=kc),
        grid=(n // tn,),
        in_specs=[pl.BlockSpec((d, 1), lambda j: (0, 0)),
                  pl.BlockSpec((d, tn), lambda j: (0, j)),
                  pl.BlockSpec((1, tn), lambda j: (0, j))],
        out_specs=pl.BlockSpec((1, tn), lambda j: (0, j)),
        out_shape=jax.ShapeDtypeStruct((1, n), F32),
        compiler_params=_params(("arbitrary",)),
        name="ada",
    )(c_col, w, b_row)


def _rms_mod_kernel(x_ref, g_ref, sc_ref, sh_ref, o_ref):
    x = x_ref[...]
    y = x * lax.rsqrt(jnp.mean(x * x, axis=-1, keepdims=True) + EPS)
    o_ref[...] = (y * g_ref[...] * (1.0 + sc_ref[...]) + sh_ref[...]).astype(o_ref.dtype)


def _rms_mod(x2, g, sc, sh, out_dtype, tm=512):
    t, d = x2.shape
    row = pl.BlockSpec((1, d), lambda i: (0, 0))
    return pl.pallas_call(
        _rms_mod_kernel,
        grid=(t // tm,),
        in_specs=[pl.BlockSpec((tm, d), lambda i: (i, 0)), row, row, row],
        out_specs=pl.BlockSpec((tm, d), lambda i: (i, 0)),
        out_shape=jax.ShapeDtypeStruct((t, d), out_dtype),
        compiler_params=_params(("parallel",)),
        name="rms_mod",
    )(x2, g, sc, sh)


def _mm_kernel(a_ref, b_ref, o_ref):
    o_ref[...] = jnp.dot(a_ref[...], b_ref[...], preferred_element_type=F32).astype(o_ref.dtype)


def _matmul(a, b, out_dtype, tm, tn, name):
    m, k = a.shape
    n = b.shape[1]
    return pl.pallas_call(
        _mm_kernel,
        grid=(m // tm, n // tn),
        in_specs=[pl.BlockSpec((tm, k), lambda i, j: (i, 0)),
                  pl.BlockSpec((k, tn), lambda i, j: (0, j))],
        out_specs=pl.BlockSpec((tm, tn), lambda i, j: (i, j)),
        out_shape=jax.ShapeDtypeStruct((m, n), out_dtype),
        compiler_params=_params(("parallel", "arbitrary")),
        name=name,
    )(a, b)


def _sortable(x):
    bits = pltpu.bitcast(x, I32)
    return bits ^ ((bits >> 31) & 0x7FFFFFFF)


def _unsortable(k):
    return pltpu.bitcast(k ^ ((k >> 31) & 0x7FFFFFFF), F32)


def _select_kernel(qi_ref, wi_ref, kit_ref, mask_ref, key_sc, qh_sc, wb_sc, *, n_sel):
    nkt, tq, tk = key_sc.shape
    nsub = tk // LANES
    ngrp = -(-n_sel // LANES)
    i = pl.program_id(0)
    q0 = i * tq
    nvis = (q0 + tq + tk - 1) // tk

    for h in range(H_IDX):
        qh_sc[h] = qi_ref[:, h * D_IDX:(h + 1) * D_IDX]
        wb_sc[h] = jnp.broadcast_to(wi_ref[:, h:h + 1], (tq, LANES))

    row = lax.broadcasted_iota(I32, (tq, 1), 0) + q0
    limit = (row // CHUNK + 1) * CHUNK
    lane = lax.broadcasted_iota(I32, (tq, LANES), 1)

    def score_tile(kt, gmax):
        k_t = kit_ref[kt]
        accs = [jnp.zeros((tq, LANES), F32) for _ in range(nsub)]
        for h in range(H_IDX):
            s = jnp.dot(qh_sc[h], k_t, preferred_element_type=F32)
            wb = wb_sc[h]
            for a in range(nsub):
                accs[a] = accs[a] + wb * jnp.maximum(s[:, a * LANES:(a + 1) * LANES], 0.0)
        gmax = list(gmax)
        for a in range(nsub):
            col = lane + (kt * tk + a * LANES)
            key = jnp.where(col < limit, _sortable(accs[a]), INT_MIN)
            key_sc[kt, :, a * LANES:(a + 1) * LANES] = key
            gmax[a % ngrp] = jnp.maximum(gmax[a % ngrp], key)
        return tuple(gmax)

    gmax = lax.fori_loop(0, nvis, score_tile,
                         tuple(jnp.full((tq, LANES), INT_MIN, I32) for _ in range(ngrp)))

    def count(pred):
        def body(kt, c):
            kk = key_sc[kt]
            col = lax.broadcasted_iota(I32, (tq, tk), 1) + kt * tk
            m = pred(kk, col).astype(I32)
            part = m[:, 0:LANES]
            for a in range(1, nsub):
                part = part + m[:, a * LANES:(a + 1) * LANES]
            return c + part
        c = lax.fori_loop(0, nvis, body, jnp.zeros((tq, LANES), I32))
        return jnp.sum(c, axis=1, keepdims=True)

    gmin, ghi = gmax[0], gmax[0]
    for g in gmax[1:]:
        gmin = jnp.minimum(gmin, g)
        ghi = jnp.maximum(ghi, g)
    lo0 = jnp.maximum(jnp.min(gmin, axis=1, keepdims=True), INT_MIN + 1)
    hi0 = jnp.max(ghi, axis=1, keepdims=True)
    few = limit < n_sel
    lo0 = jnp.where(few, INT_MIN + 1, lo0)
    hi0 = jnp.where(few, INT_MIN + 1, hi0)
    unknown = jnp.full((tq, 1), 2 ** 30, I32)

    def bis_cond(st):
        lo, hi, _, _ = st
        return jnp.max((lo < hi).astype(I32)) > 0

    def bis_body(st):
        lo, hi, c_lo, c_hi1 = st
        active = lo < hi
        mid = (lo | hi) - ((lo ^ hi) >> 1)
        fin = lambda k: jnp.clip(k, -KEY_FMAX - 1, KEY_FMAX)
        vmid = _sortable(0.5 * _unsortable(fin(lo)) + 0.5 * _unsortable(fin(hi)))
        mid = jnp.where((vmid > lo) & (vmid <= hi), vmid, mid)
        c = count(lambda kk, col: kk >= mid)
        up = active & (c >= n_sel)
        dn = active & (c < n_sel)
        hit = active & (c == n_sel)
        lo = jnp.where(up, mid, lo)
        c_lo = jnp.where(up, c, c_lo)
        hi = jnp.where(dn, mid - 1, jnp.where(hit, mid, hi))
        c_hi1 = jnp.where(dn, c, c_hi1)
        return lo, hi, c_lo, c_hi1

    thr, _, n_ge, n_gt = lax.while_loop(bis_cond, bis_body,
                                        (lo0, hi0, unknown, jnp.zeros((tq, 1), I32)))

    excess = (n_ge > n_sel) & jnp.logical_not(few)
    need = n_sel - n_gt
    ncol = nkt * tk

    def tie_cut():
        nbits = max(1, int(math.ceil(math.log2(ncol))))

        def cut_body(b, p):
            cand = p | jnp.left_shift(jnp.int32(1), nbits - 1 - b)
            cnt = count(lambda kk, col: (kk == thr) & (col < cand))
            return jnp.where(cnt < need, cand, p)

        p = lax.fori_loop(0, nbits, cut_body, jnp.zeros((tq, 1), I32))
        return p + 1

    cut = lax.cond(jnp.max(excess.astype(I32)) > 0, tie_cut, lambda: jnp.full((tq, 1), ncol, I32))
    cut = jnp.where(excess, cut, ncol)

    def write_tile(kt, carry):
        kk = key_sc[kt]
        col = lax.broadcasted_iota(I32, (tq, tk), 1) + kt * tk
        sel = (kk > thr) | ((kk == thr) & (col < cut))
        mask_ref[kt] = jnp.where(sel, 0.0, NEG).astype(mask_ref.dtype)
        return carry

    lax.fori_loop(0, nvis, write_tile, 0)

    def fill_tile(kt, carry):
        mask_ref[kt] = jnp.full((tq, tk), NEG, mask_ref.dtype)
        return carry

    lax.fori_loop(nvis, nkt, fill_tile, 0)


def _select(proj, wi, kit3, n_sel, qi_col_block):
    s = proj.shape[0]
    nkt, _, tk = kit3.shape
    tq = SEL_TQ
    nqb = s // tq
    assert -(-n_sel // LANES) <= tk // LANES
    return pl.pallas_call(
        functools.partial(_select_kernel, n_sel=n_sel),
        grid=(nqb,),
        in_specs=[pl.BlockSpec((tq, W_IDX_Q), lambda i: (i, qi_col_block)),
                  pl.BlockSpec((tq, H_IDX), lambda i: (i, 0)),
                  pl.BlockSpec((nkt, D_IDX, tk), lambda i: (0, 0, 0))],
        out_specs=pl.BlockSpec((None, nkt, tq, tk), lambda i: (i, 0, 0, 0)),
        out_shape=jax.ShapeDtypeStruct((nqb, nkt, tq, tk), BF16),
        scratch_shapes=[pltpu.VMEM((nkt, tq, tk), I32),
                        pltpu.VMEM((H_IDX, tq, D_IDX), BF16),
                        pltpu.VMEM((H_IDX, tq, LANES), F32)],
        compiler_params=_params(("arbitrary",)),
        name="dsa_select",
    )(proj, wi, kit3)


def _dsa_attn_kernel(qb_ref, kb_ref, first_ref, last_ref, slab_ref,
                     q_ref, kt_ref, v_ref, mask_ref, band_ref, far_ref, o_ref,
                     m_sc, l_sc, acc_sc):
    step = pl.program_id(0)
    tk = kt_ref.shape[1]
    nslab = tk // LANES
    slab0 = slab_ref[step]

    @pl.when(first_ref[step] == 1)
    def _():
        m_sc[...] = jnp.full(m_sc.shape, NEG, F32)
        l_sc[...] = jnp.zeros(l_sc.shape, F32)
        acc_sc[...] = jnp.zeros(acc_sc.shape, F32)

    def tile(near):
        mask = jnp.concatenate([mask_ref[r] for r in range(mask_ref.shape[0])], axis=0).astype(F32)
        ones = jnp.ones((tk, HD_A), BF16)
        for h in range(H_A):
            hs = slice(h * HD_A, (h + 1) * HD_A)
            s = jnp.dot(q_ref[:, hs], kt_ref[hs, :], preferred_element_type=F32) + mask
            if near:
                s = s + jnp.concatenate([band_ref[h, slab0 + a] for a in range(nslab)], axis=1)
                shift = 0.0
            else:
                shift = far_ref[h]
            smax = s[:, 0:LANES]
            for a in range(1, nslab):
                smax = jnp.maximum(smax, s[:, a * LANES:(a + 1) * LANES])
            m_old = m_sc[h]
            m_new = jnp.maximum(m_old, jnp.max(smax, axis=1, keepdims=True) + shift)
            alpha = jnp.exp2(m_old - m_new)
            mm = m_new - shift
            p = jnp.concatenate([jnp.exp2(s[:, a * LANES:(a + 1) * LANES] - mm) for a in range(nslab)],
                                axis=1).astype(BF16)
            pv = jnp.dot(p, jnp.concatenate([v_ref[:, hs], ones], axis=1), preferred_element_type=F32)
            acc_sc[:, hs] = alpha * acc_sc[:, hs] + pv[:, 0:HD_A]
            l_sc[h] = alpha * l_sc[h] + pv[:, HD_A:2 * HD_A]
            m_sc[h] = m_new

    @pl.when(slab0 >= 0)
    def _():
        tile(True)

    @pl.when(slab0 < 0)
    def _():
        tile(False)

    @pl.when(last_ref[step] == 1)
    def _():
        for h in range(H_A):
            hs = slice(h * HD_A, (h + 1) * HD_A)
            o_ref[:, hs] = (acc_sc[:, hs] / l_sc[h]).astype(o_ref.dtype)


def _t5_bucket_np(rel):
    nb = N_T5_BUCKETS // 2
    ret = (rel > 0).astype(np.int32) * nb
    n = np.abs(rel)
    max_exact = nb // 2
    nf = np.maximum(n, 1).astype(np.float32)
    large = max_exact + (np.log(nf / np.float32(max_exact)) / np.float32(math.log(T5_MAX_DIST / max_exact))
                         * np.float32(nb - max_exact)).astype(np.int32)
    large = np.minimum(large, nb - 1)
    return ret + np.where(n < max_exact, n, large)


def _dsa_plan(s, tq, tk):
    nqb = s // tq
    d_all = np.arange(-(s - 1), CHUNK, dtype=np.int64)
    b_all = _t5_bucket_np(d_all)
    far_bucket = int(b_all[0])
    varying = np.nonzero(b_all != far_bucket)[0]
    d_lo = int(d_all[varying[0]])
    o_min = min(int(math.ceil((d_lo - (tk - 1)) / LANES)) * LANES, 0)
    n_slabs = (-o_min + tk) // LANES
    qb, kb, first, last, slab = [], [], [], [], []
    for i in range(nqb):
        q0 = i * tq
        nvis = (q0 + tq + tk - 1) // tk
        for j in range(nvis):
            o = j * tk - q0
            qb.append(i)
            kb.append(j)
            first.append(1 if j == 0 else 0)
            last.append(1 if j == nvis - 1 else 0)
            slab.append(-1 if o < o_min else (o - o_min) // LANES)
    u = np.arange(n_slabs * LANES + tq - 1)
    g_bucket = _t5_bucket_np(np.clip(o_min + u - (tq - 1), -(s - 1), None))
    plan = tuple(np.asarray(a, np.int32) for a in (qb, kb, first, last, slab))
    return plan, g_bucket, far_bucket, n_slabs


def _dsa_attn(proj, kt, mask4, t5_table, q_col, v_col):
    s = proj.shape[0]
    tq, tk = ATT_TQ, SEL_TK
    (qb, kb, first, last, slab), g_bucket, far_bucket, n_slabs = _dsa_plan(s, tq, tk)
    nsteps = len(qb)
    t5l = t5_table * LOG2E
    g = jnp.transpose(t5l[g_bucket], (1, 0))
    band = _toeplitz(g, tq, n_slabs * LANES)
    band = jnp.transpose(band.reshape(H_A, tq, n_slabs, LANES), (0, 2, 1, 3))
    far = t5l[far_bucket]
    rq = tq // SEL_TQ
    grid_spec = pltpu.PrefetchScalarGridSpec(
        num_scalar_prefetch=5,
        grid=(nsteps,),
        in_specs=[
            pl.BlockSpec((tq, W_A), lambda t, qb, kb, f, l, sl: (qb[t], q_col)),
            pl.BlockSpec((W_A, tk), lambda t, qb, kb, f, l, sl: (0, kb[t])),
            pl.BlockSpec((tk, W_A), lambda t, qb, kb, f, l, sl: (kb[t], v_col)),
            pl.BlockSpec((rq, None, SEL_TQ, tk), lambda t, qb, kb, f, l, sl: (qb[t], kb[t], 0, 0)),
            pl.BlockSpec((H_A, n_slabs, tq, LANES), lambda t, qb, kb, f, l, sl: (0, 0, 0, 0)),
            pl.BlockSpec(memory_space=pltpu.SMEM),
        ],
        out_specs=pl.BlockSpec((tq, W_A), lambda t, qb, kb, f, l, sl: (qb[t], 0)),
        scratch_shapes=[pltpu.VMEM((H_A, tq, LANES), F32),
                        pltpu.VMEM((H_A, tq, LANES), F32),
                        pltpu.VMEM((tq, W_A), F32)],
    )
    return pl.pallas_call(
        _dsa_attn_kernel,
        grid_spec=grid_spec,
        out_shape=jax.ShapeDtypeStruct((s, W_A), BF16),
        compiler_params=_params(("arbitrary",)),
        name="dsa_attn",
    )(jnp.asarray(qb), jnp.asarray(kb), jnp.asarray(first), jnp.asarray(last), jnp.asarray(slab),
      proj, kt, proj, mask4, band, far)


def _band_kernel(q_ref, k0_ref, k1_ref, k2_ref, v0_ref, v1_ref, v2_ref, bias_ref, o_ref):
    i = pl.program_id(0)
    tq = q_ref.shape[0]
    wk = 3 * tq
    col = lax.broadcasted_iota(I32, (tq, wk), 1)
    start_mask = jnp.where(col + (i - 2) * tq >= 0, 0.0, NEG)
    for h in range(H_B):
        hs = slice(h * HD_B, (h + 1) * HD_B)
        kc = jnp.concatenate([k0_ref[hs, :], k1_ref[hs, :], k2_ref[hs, :]], axis=1)
        vc = jnp.concatenate([v0_ref[:, hs], v1_ref[:, hs], v2_ref[:, hs]], axis=0)
        s = jnp.dot(q_ref[:, hs], kc, preferred_element_type=F32) + bias_ref[h] + start_mask
        m = jnp.max(s, axis=1, keepdims=True)
        p = jnp.exp2(s - m)
        l = jnp.sum(p, axis=1, keepdims=True)
        pv = jnp.dot(p.astype(BF16), vc, preferred_element_type=F32)
        o_ref[:, hs] = (pv / l).astype(o_ref.dtype)


def _band_bias(rel_table, tq):
    assert 2 * tq >= N_LEFT_CHUNKS * CHUNK and tq % CHUNK == 0
    wk = 3 * tq
    x = np.arange(wk + tq - 1)
    idx = np.clip(2 * tq + (tq - 1) - x, -REL_CLIP, REL_CLIP) + REL_CLIP
    bias = _toeplitz(rel_table[:, idx] * LOG2E, tq, wk)
    r = np.arange(tq)[:, None]
    c = np.arange(wk)[None, :]
    dchunk = (c - 2 * tq) // CHUNK - r // CHUNK
    in_band = (dchunk <= 0) & (dchunk >= -N_LEFT_CHUNKS)
    return jnp.where(jnp.asarray(in_band)[None], bias, NEG).astype(F32)


def _band_attn(proj, kt, rel_table, q_col, v_col):
    s = proj.shape[0]
    tq = BAND_TQ
    bias = _band_bias(rel_table, tq)

    def kspec(back):
        return pl.BlockSpec((W_B, tq), lambda i: (0, jnp.maximum(i - back, 0)))

    def vspec(back):
        return pl.BlockSpec((tq, W_B), lambda i: (jnp.maximum(i - back, 0), v_col))

    return pl.pallas_call(
        _band_kernel,
        grid=(s // tq,),
        in_specs=[pl.BlockSpec((tq, W_B), lambda i: (i, q_col)),
                  kspec(2), kspec(1), kspec(0), vspec(2), vspec(1), vspec(0),
                  pl.BlockSpec((H_B, tq, 3 * tq), lambda i: (0, 0, 0))],
        out_specs=pl.BlockSpec((tq, W_B), lambda i: (i, 0)),
        out_shape=jax.ShapeDtypeStruct((s, W_B), BF16),
        compiler_params=_params(("parallel",)),
        name="band_attn",
    )(proj, kt, kt, kt, proj, proj, proj, bias)


def _merge_kernel(ya_ref, yb_ref, ga_ref, gb_ref, wa_ref, wb_ref, o_ref):
    ua = jnp.dot(ya_ref[...], wa_ref[...], preferred_element_type=F32)
    ub = jnp.dot(yb_ref[...], wb_ref[...], preferred_element_type=F32)
    o_ref[...] = (jax.nn.sigmoid(ga_ref[...]) * ua + jax.nn.sigmoid(gb_ref[...]) * ub).astype(o_ref.dtype)


def _merge(ya, yb, gates, wa, wb, tm=256):
    s, d = ya.shape[0], wa.shape[1]
    return pl.pallas_call(
        _merge_kernel,
        grid=(s // tm,),
        in_specs=[pl.BlockSpec((tm, W_A), lambda i: (i, 0)),
                  pl.BlockSpec((tm, W_B), lambda i: (i, 0)),
                  pl.BlockSpec((tm, d), lambda i: (i, 0)),
                  pl.BlockSpec((tm, d), lambda i: (i, 1)),
                  pl.BlockSpec((W_A, d), lambda i: (0, 0)),
                  pl.BlockSpec((W_B, d), lambda i: (0, 0))],
        out_specs=pl.BlockSpec((tm, d), lambda i: (i, 0)),
        out_shape=jax.ShapeDtypeStruct((s, d), BF16),
        compiler_params=_params(("parallel",)),
        name="merge",
    )(ya, yb, gates, gates, wa, wb)


def _post_kernel(x_ref, m_ref, wo_ref, g1_ref, gn_ref, sc_ref, sh_ref, wr_ref, br_ref,
                 x1_ref, h2_ref, lg_ref):
    x1 = x_ref[...] + g1_ref[...] * jnp.dot(m_ref[...], wo_ref[...], preferred_element_type=F32)
    x1_ref[...] = x1
    y = x1 * lax.rsqrt(jnp.mean(x1 * x1, axis=-1, keepdims=True) + EPS)
    h2 = y * gn_ref[...] * (1.0 + sc_ref[...]) + sh_ref[...]
    h2_ref[...] = h2
    lg_ref[...] = jnp.dot(h2, wr_ref[...], preferred_element_type=F32,
                          precision=lax.Precision.HIGHEST) + br_ref[...]


def _post(x2, merged, wo, g1, gn, sc, sh, w_router, b_router, tm=256):
    t, d = x2.shape
    nr = w_router.shape[1]
    row = pl.BlockSpec((1, d), lambda i: (0, 0))
    tile = pl.BlockSpec((tm, d), lambda i: (i, 0))
    return pl.pallas_call(
        _post_kernel,
        grid=(t // tm,),
        in_specs=[tile, tile, pl.BlockSpec((d, d), lambda i: (0, 0)), row, row, row, row,
                  pl.BlockSpec((d, nr), lambda i: (0, 0)), pl.BlockSpec((1, nr), lambda i: (0, 0))],
        out_specs=[tile, tile, pl.BlockSpec((tm, nr), lambda i: (i, 0))],
        out_shape=[jax.ShapeDtypeStruct((t, d), F32), jax.ShapeDtypeStruct((t, d), F32),
                   jax.ShapeDtypeStruct((t, nr), F32)],
        compiler_params=_params(("parallel",)),
        name="post",
    )(x2, merged, wo, g1, gn, sc, sh, w_router, b_router)


def _moe_kernel(tok_ref, dst_ref, be_ref, nu_ref,
                h_hbm, wrow_ref, w1_ref, w3_ref, w2_ref, z_hbm,
                xbuf, ybuf, w1b, w3b, w2b, gsem, ssem):
    b = pl.program_id(0)
    bm = xbuf.shape[1]
    nu = nu_ref[0]
    slot = b % 2

    def gather_copy(sl, r, t):
        return pltpu.make_async_copy(h_hbm.at[pl.ds(t, 1), :], xbuf.at[sl, pl.ds(r, 1), :], gsem.at[sl])

    def scatter_copy(sl, r, d):
        return pltpu.make_async_copy(ybuf.at[sl, pl.ds(r, 1), :], z_hbm.at[pl.ds(d, 1), :], ssem.at[sl])

    def gather_start(blk, sl):
        def body(r, carry):
            gather_copy(sl, r, tok_ref[blk * bm + r]).start()
            return carry
        lax.fori_loop(0, bm, body, 0, unroll=8)

    def gather_wait(sl):
        def body(r, carry):
            gather_copy(sl, r, 0).wait()
            return carry
        lax.fori_loop(0, bm, body, 0, unroll=8)

    def scatter_start(blk, sl):
        def body(r, carry):
            scatter_copy(sl, r, dst_ref[blk * bm + r]).start()
            return carry
        lax.fori_loop(0, bm, body, 0, unroll=8)

    def scatter_wait(sl):
        def body(r, carry):
            scatter_copy(sl, r, 0).wait()
            return carry
        lax.fori_loop(0, bm, body, 0, unroll=8)

    @pl.when(b < nu)
    def _():
        @pl.when(b == 0)
        def _():
            gather_start(0, 0)
            ybuf[...] = jnp.zeros(ybuf.shape, F32)
            n_real = z_hbm.shape[0] - 2 * bm
            for sl in range(2):
                spare = pltpu.make_async_copy(ybuf.at[sl], z_hbm.at[pl.ds(n_real + sl * bm, bm), :], ssem.at[sl])
                spare.start()
                spare.wait()

        @pl.when(b + 1 < nu)
        def _():
            gather_start(b + 1, 1 - slot)

        @pl.when((b == 0) | (be_ref[b] != be_ref[jnp.maximum(b - 1, 0)]))
        def _():
            w1b[...] = w1_ref[0].astype(BF16)
            w3b[...] = w3_ref[0].astype(BF16)
            w2b[...] = w2_ref[0].astype(BF16)

        gather_wait(slot)

        @pl.when(b >= 2)
        def _():
            scatter_wait(slot)

        x = xbuf[slot].astype(BF16)
        a1 = jnp.dot(x, w1b[...], preferred_element_type=F32)
        a3 = jnp.dot(x, w3b[...], preferred_element_type=F32)
        a = (a1 * jax.nn.sigmoid(a1)) * a3
        y = jnp.dot(a.astype(BF16), w2b[...], preferred_element_type=F32)
        ybuf[slot] = y * wrow_ref[...]
        scatter_start(b, slot)

        @pl.when(b == nu - 1)
        def _():
            scatter_wait(slot)

            @pl.when(b >= 1)
            def _():
                scatter_wait(1 - slot)


def _moe(h2, tok_buf, dst_buf, blk_e, n_used, w_rows, w1, w3, w2):
    t, d = h2.shape
    dff = w1.shape[2]
    bm = MOE_BM
    nb = blk_e.shape[0]
    grid_spec = pltpu.PrefetchScalarGridSpec(
        num_scalar_prefetch=4,
        grid=(nb,),
        in_specs=[
            pl.BlockSpec(memory_space=pl.ANY),
            pl.BlockSpec((bm, 1), lambda b, tok, dst, be, nu: (b, 0)),
            pl.BlockSpec((1, d, dff), lambda b, tok, dst, be, nu: (be[b], 0, 0)),
            pl.BlockSpec((1, d, dff), lambda b, tok, dst, be, nu: (be[b], 0, 0)),
            pl.BlockSpec((1, dff, d), lambda b, tok, dst, be, nu: (be[b], 0, 0)),
        ],
        out_specs=pl.BlockSpec(memory_space=pl.ANY),
        scratch_shapes=[pltpu.VMEM((2, bm, d), F32), pltpu.VMEM((2, bm, d), F32),
                        pltpu.VMEM((d, dff), BF16), pltpu.VMEM((d, dff), BF16), pltpu.VMEM((dff, d), BF16),
                        pltpu.SemaphoreType.DMA((2,)), pltpu.SemaphoreType.DMA((2,))],
    )
    return pl.pallas_call(
        _moe_kernel,
        grid_spec=grid_spec,
        out_shape=jax.ShapeDtypeStruct((TOPK_IN_GROUP * t + 2 * bm, d), F32),
        compiler_params=pltpu.CompilerParams(dimension_semantics=("arbitrary",),
                                             vmem_limit_bytes=VMEM_LIMIT, has_side_effects=True),
        name="moe",
    )(tok_buf, dst_buf, blk_e, n_used, h2, w_rows, w1, w3, w2)


def _final_kernel(x1_ref, z0_ref, z1_ref, g2_ref, gn_ref, o_ref):
    x2 = x1_ref[...] + g2_ref[...] * (z0_ref[...] + z1_ref[...])
    y = x2 * lax.rsqrt(jnp.mean(x2 * x2, axis=-1, keepdims=True) + EPS)
    o_ref[...] = y * gn_ref[...]


def _final(x1, z, g2, gn, tm=256):
    t, d = x1.shape
    row = pl.BlockSpec((1, d), lambda i: (0, 0))
    return pl.pallas_call(
        _final_kernel,
        grid=(t // tm,),
        in_specs=[pl.BlockSpec((tm, d), lambda i: (i, 0)),
                  pl.BlockSpec((tm, d), lambda i: (i, 0)),
                  pl.BlockSpec((tm, d), lambda i: (t // tm + i, 0)), row, row],
        out_specs=pl.BlockSpec((tm, d), lambda i: (i, 0)),
        out_shape=jax.ShapeDtypeStruct((t, d), F32),
        compiler_params=_params(("parallel",)),
        name="final",
    )(x1, z, z, g2, gn)


def _route(logits, t):
    gl = logits[:, :N_GROUPS]
    el = logits[:, N_GROUPS:N_GROUPS + N_EXPERTS].reshape(t, N_GROUPS, EXP_PER_GROUP)
    g_prob = jax.nn.softmax(gl, axis=-1)
    grp = jnp.argmax(gl, axis=-1).astype(I32)
    p_grp = jnp.take_along_axis(g_prob, grp[:, None], axis=-1)[:, 0]
    e_in = jnp.take_along_axis(el, grp[:, None, None], axis=1)[:, 0]
    top_v, top_i = lax.top_k(e_in, TOPK_IN_GROUP)
    p_in = jax.nn.softmax(top_v, axis=-1)
    expert = grp[:, None] * EXP_PER_GROUP + top_i.astype(I32)
    weight = p_grp[:, None] * p_in

    bm = MOE_BM
    m = t * TOPK_IN_GROUP
    e_flat = expert.reshape(m)
    w_flat = weight.reshape(m)
    order = jnp.argsort(e_flat, stable=True).astype(I32)
    counts = jnp.bincount(e_flat, length=N_EXPERTS).astype(I32)
    start = jnp.cumsum(counts) - counts
    padded = ((counts + bm - 1) // bm) * bm
    pend = jnp.cumsum(padded)
    pstart = pend - padded
    nb = m // bm + N_EXPERTS
    blk_e = jnp.minimum(jnp.searchsorted(pend, jnp.arange(nb, dtype=I32) * bm, side='right'),
                        N_EXPERTS - 1).astype(I32)
    n_used = (pend[-1] // bm).astype(I32).reshape(1)
    pos = jnp.arange(nb * bm, dtype=I32).reshape(nb, bm)
    src = (start[blk_e] - pstart[blk_e])[:, None] + pos
    valid = (src < (start + counts)[blk_e][:, None]) & (pos < pend[-1])
    a = order[jnp.clip(src, 0, m - 1)]
    tok = a // TOPK_IN_GROUP
    tok_buf = jnp.where(valid, tok, 0).reshape(-1)
    spare = m + pos % (2 * bm)
    dst_buf = jnp.where(valid, (a % TOPK_IN_GROUP) * t + tok, spare).reshape(-1)
    w_rows = jnp.where(valid, w_flat[a], 0.0).reshape(-1, 1)
    return tok_buf, dst_buf, blk_e, n_used, w_rows


def kernel(x, c, w_ada, b_ada, norm_mix, w_in, t5_table, rel_table, w_up_a, w_up_b, w_o, norm_ffn,
           w_rg, b_rg, w_re, b_re, w1, w3, w2, norm_final):
    bn, s, d = x.shape
    assert bn == 1 and w_ada.shape[0] == 1
    assert s % 1024 == 0
    t = bn * s
    x2 = x.reshape(t, d)
    n_sel = min(TOPK_MAX, s // 4)

    mod = _ada(c.reshape(d, 1), w_ada[0], b_ada[0].reshape(1, 6 * d))
    sh1, sc1, g1, sh2, sc2, g2 = [mod[:, i * d:(i + 1) * d] for i in range(6)]

    h = _rms_mod(x2, norm_mix[0].reshape(1, d), sc1, sh1, BF16)

    cols = np.cumsum([0, W_A, W_A, W_A, W_IDX_Q, D_IDX, H_IDX, W_B, W_B, W_B, d, d])
    wsl = [w_in[0][:, cols[i]:cols[i + 1]] for i in range(11)]
    wqa, wka, wva, wqi, wki, wwi, wqb, wkb, wvb, wga, wgb = wsl
    w_main = jnp.concatenate([wqa * (LOG2E / math.sqrt(HD_A)), wka, wva, wqi,
                              wqb * (LOG2E / math.sqrt(HD_B)), wkb, wvb], axis=1).astype(BF16)
    w_gate = jnp.concatenate([wga, wgb], axis=1).astype(BF16)
    w_idx = jnp.concatenate([wki, wwi, jnp.zeros((d, LANES - D_IDX - H_IDX), F32)], axis=1).astype(BF16)

    proj = _matmul(h, w_main, BF16, 1024, 1024, "proj_main")
    gates = _matmul(h, w_gate, F32, 1024, 1024, "proj_gate")
    idx = _matmul(h, w_idx, F32, 1024, LANES, "proj_idx")
    nkt = s // SEL_TK
    kit3 = jnp.transpose(idx[:, :D_IDX].astype(BF16).reshape(nkt, SEL_TK, D_IDX), (0, 2, 1))
    wi = idx[:, D_IDX:D_IDX + H_IDX] * ((H_IDX ** -0.5) * (D_IDX ** -0.5))
    kt_a = jnp.transpose(proj[:, W_A:2 * W_A])
    kt_b = jnp.transpose(proj[:, 5 * W_A:6 * W_A])

    mask4 = _select(proj, wi, kit3, n_sel, qi_col_block=3)
    y_a = _dsa_attn(proj, kt_a, mask4, t5_table, q_col=0, v_col=2)
    y_b = _band_attn(proj, kt_b, rel_table[0], q_col=4, v_col=6)

    merged = _merge(y_a, y_b, gates, w_up_a[0].astype(BF16), w_up_b[0].astype(BF16))

    nr = LANES
    w_router = jnp.concatenate([w_rg[0], w_re[0], jnp.zeros((d, nr - N_GROUPS - N_EXPERTS), F32)], axis=1)
    b_router = jnp.concatenate([b_rg[0], b_re[0], jnp.zeros((nr - N_GROUPS - N_EXPERTS,), F32)]).reshape(1, nr)
    x1, h2, logits = _post(x2, merged, w_o[0].astype(BF16), g1, norm_ffn[0].reshape(1, d), sc2, sh2,
                           w_router, b_router)
    tok_buf, dst_buf, blk_e, n_used, w_rows = _route(logits, t)
    z = _moe(h2, tok_buf, dst_buf, blk_e, n_used, w_rows, w1[0], w3[0], w2[0])
    out = _final(x1, z, g2, norm_final.reshape(1, d))
    return out.reshape(bn, s, d)
```

```python
import functools
import math

import numpy as np
import jax
import jax.numpy as jnp
from jax import lax
from jax.experimental import pallas as pl
from jax.experimental.pallas import tpu as pltpu

F32 = jnp.float32
BF16 = jnp.bfloat16
I32 = jnp.int32

CHUNK = 64
EPS = 1e-6
H_A, HD_A = 8, 128
H_IDX, D_IDX = 16, 64
TOPK_MAX = 256
N_T5_BUCKETS = 32
T5_MAX_DIST = 1024
H_B, HD_B = 8, 128
N_LEFT_CHUNKS = 8
REL_CLIP = 128
N_GROUPS = 8
EXP_PER_GROUP = 8
N_EXPERTS = N_GROUPS * EXP_PER_GROUP
TOPK_IN_GROUP = 2

W_A = H_A * HD_A
W_B = H_B * HD_B
W_IDX_Q = H_IDX * D_IDX

NEG = -1e30
INT_MIN = -(2 ** 31)
KEY_FMAX = 0x7F7FFFFF
LOG2E = math.log2(math.e)

LANES = 128
VMEM_LIMIT = 56 * 1024 * 1024

SEL_TQ = 128
SEL_TK = 512
ATT_TQ = 256
BAND_TQ = 256
MOE_BM = 256


def _params(sem, vmem=VMEM_LIMIT):
    return pltpu.CompilerParams(dimension_semantics=sem, vmem_limit_bytes=vmem)


def _toeplitz(g, nrows, ncols):
    n = ncols + nrows - 1
    assert g.shape[-1] == n
    u = jnp.concatenate([g, jnp.zeros(g.shape[:-1] + (1,), g.dtype)], axis=-1)
    flat = jnp.tile(u, (1,) * (g.ndim - 1) + (nrows,))[..., :nrows * n]
    return flat.reshape(g.shape[:-1] + (nrows, n))[..., nrows - 1:]


def _ada_kernel(c_ref, w_ref, b_ref, o_ref, *, kc):
    d = w_ref.shape[0]
    tn = w_ref.shape[1]

    def body(k, acc):
        r0 = pl.multiple_of(k * kc, kc)
        cc = c_ref[pl.ds(r0, kc), :]
        ca = cc * jax.nn.sigmoid(cc)
        return acc + jnp.sum(w_ref[pl.ds(r0, kc), :] * ca, axis=0, keepdims=True)

    acc = lax.fori_loop(0, d // kc, body, jnp.zeros((1, tn), F32))
    o_ref[...] = acc + b_ref[...]


def _ada(c_col, w, b_row, tn=1024, kc=256):
    d, n = w.shape
    return pl.pallas_call(
        functools.partial(_ada_kernel, kc=kc),
        grid=(n // tn,),
        in_specs=[pl.BlockSpec((d, 1), lambda j: (0, 0)),
                  pl.BlockSpec((d, tn), lambda j: (0, j)),
                  pl.BlockSpec((1, tn), lambda j: (0, j))],
        out_specs=pl.BlockSpec((1, tn), lambda j: (0, j)),
        out_shape=jax.ShapeDtypeStruct((1, n), F32),
        compiler_params=_params(("arbitrary",)),
        name="ada",
    )(c_col, w, b_row)


def _rms_mod_kernel(x_ref, g_ref, sc_ref, sh_ref, o_ref):
    x = x_ref[...]
    y = x * lax.rsqrt(jnp.mean(x * x, axis=-1, keepdims=True) + EPS)
    o_ref[...] = (y * g_ref[...] * (1.0 + sc_ref[...]) + sh_ref[...]).astype(o_ref.dtype)


def _rms_mod(x2, g, sc, sh, out_dtype, tm=512):
    t, d = x2.shape
    row = pl.BlockSpec((1, d), lambda i: (0, 0))
    return pl.pallas_call(
        _rms_mod_kernel,
        grid=(t // tm,),
        in_specs=[pl.BlockSpec((tm, d), lambda i: (i, 0)), row, row, row],
        out_specs=pl.BlockSpec((tm, d), lambda i: (i, 0)),
        out_shape=jax.ShapeDtypeStruct((t, d), out_dtype),
        compiler_params=_params(("parallel",)),
        name="rms_mod",
    )(x2, g, sc, sh)


def _mm_kernel(a_ref, b_ref, o_ref):
    o_ref[...] = jnp.dot(a_ref[...], b_ref[...], preferred_element_type=F32).astype(o_ref.dtype)


def _matmul(a, b, out_dtype, tm, tn, name):
    m, k = a.shape
    n = b.shape[1]
    return pl.pallas_call(
        _mm_kernel,
        grid=(m // tm, n // tn),
        in_specs=[pl.BlockSpec((tm, k), lambda i, j: (i, 0)),
                  pl.BlockSpec((k, tn), lambda i, j: (0, j))],
        out_specs=pl.BlockSpec((tm, tn), lambda i, j: (i, j)),
        out_shape=jax.ShapeDtypeStruct((m, n), out_dtype),
        compiler_params=_params(("parallel", "arbitrary")),
        name=name,
    )(a, b)


def _sortable(x):
    bits = pltpu.bitcast(x, I32)
    return bits ^ ((bits >> 31) & 0x7FFFFFFF)


def _unsortable(k):
    return pltpu.bitcast(k ^ ((k >> 31) & 0x7FFFFFFF), F32)


def _select_kernel(qi_ref, wi_ref, kit_ref, mask_ref, key_sc, qh_sc, wb_sc, *, n_sel):
    nkt, tq, tk = key_sc.shape
    nsub = tk // LANES
    ngrp = -(-n_sel // LANES)
    i = pl.program_id(0)
    q0 = i * tq
    nvis = (q0 + tq + tk - 1) // tk

    for h in range(H_IDX):
        qh_sc[h * tq:(h + 1) * tq, :] = qi_ref[:, h * D_IDX:(h + 1) * D_IDX]
        wb_sc[h] = jnp.broadcast_to(wi_ref[:, h:h + 1], (tq, LANES))

    row = lax.broadcasted_iota(I32, (tq, 1), 0) + q0
    limit = (row // CHUNK + 1) * CHUNK
    lane = lax.broadcasted_iota(I32, (tq, LANES), 1)

    def score_tile(kt, gmax):
        k_t = kit_ref[kt]
        accs = [jnp.zeros((tq, LANES), F32) for _ in range(nsub)]
        s_all = jnp.dot(qh_sc[...], k_t, preferred_element_type=F32)
        for h in range(H_IDX):
            s = s_all[h * tq:(h + 1) * tq, :]
            wb = wb_sc[h]
            for a in range(nsub):
                accs[a] = accs[a] + wb * jnp.maximum(s[:, a * LANES:(a + 1) * LANES], 0.0)
        gmax = list(gmax)
        for a in range(nsub):
            col = lane + (kt * tk + a * LANES)
            key = jnp.where(col < limit, _sortable(accs[a]), INT_MIN)
            key_sc[kt, :, a * LANES:(a + 1) * LANES] = key
            gmax[a % ngrp] = jnp.maximum(gmax[a % ngrp], key)
        return tuple(gmax)

    gmax = lax.fori_loop(0, nvis, score_tile,
                         tuple(jnp.full((tq, LANES), INT_MIN, I32) for _ in range(ngrp)))

    def count(pred):
        def body(kt, c):
            for a in range(nsub):
                kk = key_sc[kt, :, a * LANES:(a + 1) * LANES]
                c = c + pred(kk, lane + (kt * tk + a * LANES)).astype(I32)
            return c
        c = lax.fori_loop(0, nvis, body, jnp.zeros((tq, LANES), I32))
        return jnp.sum(c, axis=1, keepdims=True)

    gmin, ghi = gmax[0], gmax[0]
    for g in gmax[1:]:
        gmin = jnp.minimum(gmin, g)
        ghi = jnp.maximum(ghi, g)
    lo0 = jnp.maximum(jnp.min(gmin, axis=1, keepdims=True), INT_MIN + 1)
    hi0 = jnp.max(ghi, axis=1, keepdims=True)
    few = limit < n_sel
    lo0 = jnp.where(few, INT_MIN + 1, lo0)
    hi0 = jnp.where(few, INT_MIN + 1, hi0)
    unknown = jnp.full((tq, 1), 2 ** 30, I32)

    def bis_cond(st):
        lo, hi, _, _ = st
        return jnp.max((lo < hi).astype(I32)) > 0

    def bis_body(st):
        lo, hi, c_lo, c_hi1 = st
        active = lo < hi
        mid = (lo | hi) - ((lo ^ hi) >> 1)
        fin = lambda k: jnp.clip(k, -KEY_FMAX - 1, KEY_FMAX)
        vmid = _sortable(0.5 * _unsortable(fin(lo)) + 0.5 * _unsortable(fin(hi)))
        mid = jnp.where((vmid > lo) & (vmid <= hi), vmid, mid)
        c = count(lambda kk, col: kk >= mid)
        up = active & (c >= n_sel)
        dn = active & (c < n_sel)
        hit = active & (c == n_sel)
        lo = jnp.where(up, mid, lo)
        c_lo = jnp.where(up, c, c_lo)
        hi = jnp.where(dn, mid - 1, jnp.where(hit, mid, hi))
        c_hi1 = jnp.where(dn, c, c_hi1)
        return lo, hi, c_lo, c_hi1

    thr, _, n_ge, n_gt = lax.while_loop(bis_cond, bis_body,
                                        (lo0, hi0, unknown, jnp.zeros((tq, 1), I32)))

    excess = (n_ge > n_sel) & jnp.logical_not(few)
    need = n_sel - n_gt
    ncol = nkt * tk

    def tie_cut():
        nbits = max(1, int(math.ceil(math.log2(ncol))))

        def cut_body(b, p):
            cand = p | jnp.left_shift(jnp.int32(1), nbits - 1 - b)
            cnt = count(lambda kk, col: (kk == thr) & (col < cand))
            return jnp.where(cnt < need, cand, p)

        p = lax.fori_loop(0, nbits, cut_body, jnp.zeros((tq, 1), I32))
        return p + 1

    cut = lax.cond(jnp.max(excess.astype(I32)) > 0, tie_cut, lambda: jnp.full((tq, 1), ncol, I32))
    cut = jnp.where(excess, cut, ncol)

    def write_tile(kt, carry):
        kk = key_sc[kt]
        col = lax.broadcasted_iota(I32, (tq, tk), 1) + kt * tk
        sel = (kk > thr) | ((kk == thr) & (col < cut))
        mask_ref[kt] = jnp.where(sel, 0.0, NEG).astype(mask_ref.dtype)
        return carry

    lax.fori_loop(0, nvis, write_tile, 0)

    def fill_tile(kt, carry):
        mask_ref[kt] = jnp.full((tq, tk), NEG, mask_ref.dtype)
        return carry

    lax.fori_loop(nvis, nkt, fill_tile, 0)


def _select(proj, wi, kit3, n_sel, qi_col_block):
    s = proj.shape[0]
    nkt, _, tk = kit3.shape
    tq = SEL_TQ
    nqb = s // tq
    assert -(-n_sel // LANES) <= tk // LANES
    return pl.pallas_call(
        functools.partial(_select_kernel, n_sel=n_sel),
        grid=(nqb,),
        in_specs=[pl.BlockSpec((tq, W_IDX_Q), lambda i: (i, qi_col_block)),
                  pl.BlockSpec((tq, H_IDX), lambda i: (i, 0)),
                  pl.BlockSpec((nkt, D_IDX, tk), lambda i: (0, 0, 0))],
        out_specs=pl.BlockSpec((None, nkt, tq, tk), lambda i: (i, 0, 0, 0)),
        out_shape=jax.ShapeDtypeStruct((nqb, nkt, tq, tk), BF16),
        scratch_shapes=[pltpu.VMEM((nkt, tq, tk), I32),
                        pltpu.VMEM((H_IDX * tq, D_IDX), BF16),
                        pltpu.VMEM((H_IDX, tq, LANES), F32)],
        compiler_params=_params(("arbitrary",)),
        name="dsa_select",
    )(proj, wi, kit3)


def _dsa_attn_kernel(qb_ref, kb_ref, first_ref, last_ref, slab_ref,
                     q_ref, kt_ref, v_ref, mask_ref, band_ref, far_ref, o_ref,
                     m_sc, l_sc, acc_sc):
    step = pl.program_id(0)
    tk = kt_ref.shape[1]
    nslab = tk // LANES
    slab0 = slab_ref[step]

    @pl.when(first_ref[step] == 1)
    def _():
        m_sc[...] = jnp.full(m_sc.shape, NEG, F32)
        l_sc[...] = jnp.zeros(l_sc.shape, F32)
        acc_sc[...] = jnp.zeros(acc_sc.shape, F32)

    def tile(near):
        mask = jnp.concatenate([mask_ref[r] for r in range(mask_ref.shape[0])], axis=0).astype(F32)
        ones = jnp.ones((tk, HD_A), BF16)
        for h in range(H_A):
            hs = slice(h * HD_A, (h + 1) * HD_A)
            s = jnp.dot(q_ref[:, hs], kt_ref[hs, :], preferred_element_type=F32) + mask
            if near:
                s = s + jnp.concatenate([band_ref[h, slab0 + a] for a in range(nslab)], axis=1)
                shift = 0.0
            else:
                shift = far_ref[h]
            smax = s[:, 0:LANES]
            for a in range(1, nslab):
                smax = jnp.maximum(smax, s[:, a * LANES:(a + 1) * LANES])
            m_old = m_sc[h]
            m_new = jnp.maximum(m_old, jnp.max(smax, axis=1, keepdims=True) + shift)
            alpha = jnp.exp2(m_old - m_new)
            mm = m_new - shift
            p = jnp.concatenate([jnp.exp2(s[:, a * LANES:(a + 1) * LANES] - mm) for a in range(nslab)],
                                axis=1).astype(BF16)
            pv = jnp.dot(p, jnp.concatenate([v_ref[:, hs], ones], axis=1), preferred_element_type=F32)
            acc_sc[:, hs] = alpha * acc_sc[:, hs] + pv[:, 0:HD_A]
            l_sc[h] = alpha * l_sc[h] + pv[:, HD_A:2 * HD_A]
            m_sc[h] = m_new

    @pl.when(slab0 >= 0)
    def _():
        tile(True)

    @pl.when(slab0 < 0)
    def _():
        tile(False)

    @pl.when(last_ref[step] == 1)
    def _():
        for h in range(H_A):
            hs = slice(h * HD_A, (h + 1) * HD_A)
            o_ref[:, hs] = (acc_sc[:, hs] / l_sc[h]).astype(o_ref.dtype)


def _t5_bucket_np(rel):
    nb = N_T5_BUCKETS // 2
    ret = (rel > 0).astype(np.int32) * nb
    n = np.abs(rel)
    max_exact = nb // 2
    nf = np.maximum(n, 1).astype(np.float32)
    large = max_exact + (np.log(nf / np.float32(max_exact)) / np.float32(math.log(T5_MAX_DIST / max_exact))
                         * np.float32(nb - max_exact)).astype(np.int32)
    large = np.minimum(large, nb - 1)
    return ret + np.where(n < max_exact, n, large)


def _dsa_plan(s, tq, tk):
    nqb = s // tq
    d_all = np.arange(-(s - 1), CHUNK, dtype=np.int64)
    b_all = _t5_bucket_np(d_all)
    far_bucket = int(b_all[0])
    varying = np.nonzero(b_all != far_bucket)[0]
    d_lo = int(d_all[varying[0]])
    o_min = min(int(math.ceil((d_lo - (tk - 1)) / LANES)) * LANES, 0)
    n_slabs = (-o_min + tk) // LANES
    qb, kb, first, last, slab = [], [], [], [], []
    for i in range(nqb):
        q0 = i * tq
        nvis = (q0 + tq + tk - 1) // tk
        for j in range(nvis):
            o = j * tk - q0
            qb.append(i)
            kb.append(j)
            first.append(1 if j == 0 else 0)
            last.append(1 if j == nvis - 1 else 0)
            slab.append(-1 if o < o_min else (o - o_min) // LANES)
    u = np.arange(n_slabs * LANES + tq - 1)
    g_bucket = _t5_bucket_np(np.clip(o_min + u - (tq - 1), -(s - 1), None))
    plan = tuple(np.asarray(a, np.int32) for a in (qb, kb, first, last, slab))
    return plan, g_bucket, far_bucket, n_slabs


def _dsa_attn(proj, kt, mask4, t5_table, q_col, v_col):
    s = proj.shape[0]
    tq, tk = ATT_TQ, SEL_TK
    (qb, kb, first, last, slab), g_bucket, far_bucket, n_slabs = _dsa_plan(s, tq, tk)
    nsteps = len(qb)
    t5l = t5_table * LOG2E
    g = jnp.transpose(t5l[g_bucket], (1, 0))
    band = _toeplitz(g, tq, n_slabs * LANES)
    band = jnp.transpose(band.reshape(H_A, tq, n_slabs, LANES), (0, 2, 1, 3))
    far = t5l[far_bucket]
    rq = tq // SEL_TQ
    grid_spec = pltpu.PrefetchScalarGridSpec(
        num_scalar_prefetch=5,
        grid=(nsteps,),
        in_specs=[
            pl.BlockSpec((tq, W_A), lambda t, qb, kb, f, l, sl: (qb[t], q_col)),
            pl.BlockSpec((W_A, tk), lambda t, qb, kb, f, l, sl: (0, kb[t])),
            pl.BlockSpec((tk, W_A), lambda t, qb, kb, f, l, sl: (kb[t], v_col)),
            pl.BlockSpec((rq, None, SEL_TQ, tk), lambda t, qb, kb, f, l, sl: (qb[t], kb[t], 0, 0)),
            pl.BlockSpec((H_A, n_slabs, tq, LANES), lambda t, qb, kb, f, l, sl: (0, 0, 0, 0)),
            pl.BlockSpec(memory_space=pltpu.SMEM),
        ],
        out_specs=pl.BlockSpec((tq, W_A), lambda t, qb, kb, f, l, sl: (qb[t], 0)),
        scratch_shapes=[pltpu.VMEM((H_A, tq, LANES), F32),
                        pltpu.VMEM((H_A, tq, LANES), F32),
                        pltpu.VMEM((tq, W_A), F32)],
    )
    return pl.pallas_call(
        _dsa_attn_kernel,
        grid_spec=grid_spec,
        out_shape=jax.ShapeDtypeStruct((s, W_A), BF16),
        compiler_params=_params(("arbitrary",)),
        name="dsa_attn",
    )(jnp.asarray(qb), jnp.asarray(kb), jnp.asarray(first), jnp.asarray(last), jnp.asarray(slab),
      proj, kt, proj, mask4, band, far)


def _band_kernel(q_ref, k0_ref, k1_ref, k2_ref, v0_ref, v1_ref, v2_ref, bias_ref, o_ref):
    i = pl.program_id(0)
    tq = q_ref.shape[0]
    wk = 3 * tq
    col = lax.broadcasted_iota(I32, (tq, wk), 1)
    start_mask = jnp.where(col + (i - 2) * tq >= 0, 0.0, NEG)
    for h in range(H_B):
        hs = slice(h * HD_B, (h + 1) * HD_B)
        kc = jnp.concatenate([k0_ref[hs, :], k1_ref[hs, :], k2_ref[hs, :]], axis=1)
        vc = jnp.concatenate([v0_ref[:, hs], v1_ref[:, hs], v2_ref[:, hs]], axis=0)
        s = jnp.dot(q_ref[:, hs], kc, preferred_element_type=F32) + bias_ref[h] + start_mask
        m = jnp.max(s, axis=1, keepdims=True)
        p = jnp.exp2(s - m)
        l = jnp.sum(p, axis=1, keepdims=True)
        pv = jnp.dot(p.astype(BF16), vc, preferred_element_type=F32)
        o_ref[:, hs] = (pv / l).astype(o_ref.dtype)


def _band_bias(rel_table, tq):
    assert 2 * tq >= N_LEFT_CHUNKS * CHUNK and tq % CHUNK == 0
    wk = 3 * tq
    x = np.arange(wk + tq - 1)
    idx = np.clip(2 * tq + (tq - 1) - x, -REL_CLIP, REL_CLIP) + REL_CLIP
    bias = _toeplitz(rel_table[:, idx] * LOG2E, tq, wk)
    r = np.arange(tq)[:, None]
    c = np.arange(wk)[None, :]
    dchunk = (c - 2 * tq) // CHUNK - r // CHUNK
    in_band = (dchunk <= 0) & (dchunk >= -N_LEFT_CHUNKS)
    return jnp.where(jnp.asarray(in_band)[None], bias, NEG).astype(F32)


def _band_attn(proj, kt, rel_table, q_col, v_col):
    s = proj.shape[0]
    tq = BAND_TQ
    bias = _band_bias(rel_table, tq)

    def kspec(back):
        return pl.BlockSpec((W_B, tq), lambda i: (0, jnp.maximum(i - back, 0)))

    def vspec(back):
        return pl.BlockSpec((tq, W_B), lambda i: (jnp.maximum(i - back, 0), v_col))

    return pl.pallas_call(
        _band_kernel,
        grid=(s // tq,),
        in_specs=[pl.BlockSpec((tq, W_B), lambda i: (i, q_col)),
                  kspec(2), kspec(1), kspec(0), vspec(2), vspec(1), vspec(0),
                  pl.BlockSpec((H_B, tq, 3 * tq), lambda i: (0, 0, 0))],
        out_specs=pl.BlockSpec((tq, W_B), lambda i: (i, 0)),
        out_shape=jax.ShapeDtypeStruct((s, W_B), BF16),
        compiler_params=_params(("parallel",)),
        name="band_attn",
    )(proj, kt, kt, kt, proj, proj, proj, bias)


def _merge_kernel(ya_ref, yb_ref, ga_ref, gb_ref, wa_ref, wb_ref, o_ref):
    ua = jnp.dot(ya_ref[...], wa_ref[...], preferred_element_type=F32)
    ub = jnp.dot(yb_ref[...], wb_ref[...], preferred_element_type=F32)
    o_ref[...] = (jax.nn.sigmoid(ga_ref[...]) * ua + jax.nn.sigmoid(gb_ref[...]) * ub).astype(o_ref.dtype)


def _merge(ya, yb, gates, wa, wb, tm=256):
    s, d = ya.shape[0], wa.shape[1]
    return pl.pallas_call(
        _merge_kernel,
        grid=(s // tm,),
        in_specs=[pl.BlockSpec((tm, W_A), lambda i: (i, 0)),
                  pl.BlockSpec((tm, W_B), lambda i: (i, 0)),
                  pl.BlockSpec((tm, d), lambda i: (i, 0)),
                  pl.BlockSpec((tm, d), lambda i: (i, 1)),
                  pl.BlockSpec((W_A, d), lambda i: (0, 0)),
                  pl.BlockSpec((W_B, d), lambda i: (0, 0))],
        out_specs=pl.BlockSpec((tm, d), lambda i: (i, 0)),
        out_shape=jax.ShapeDtypeStruct((s, d), BF16),
        compiler_params=_params(("parallel",)),
        name="merge",
    )(ya, yb, gates, gates, wa, wb)


def _post_kernel(x_ref, m_ref, wo_ref, g1_ref, gn_ref, sc_ref, sh_ref, x1_ref, h2_ref):
    x1 = x_ref[...] + g1_ref[...] * jnp.dot(m_ref[...], wo_ref[...], preferred_element_type=F32)
    x1_ref[...] = x1
    y = x1 * lax.rsqrt(jnp.mean(x1 * x1, axis=-1, keepdims=True) + EPS)
    h2_ref[...] = y * gn_ref[...] * (1.0 + sc_ref[...]) + sh_ref[...]


def _post(x2, merged, wo, g1, gn, sc, sh, tm=256):
    t, d = x2.shape
    row = pl.BlockSpec((1, d), lambda i: (0, 0))
    tile = pl.BlockSpec((tm, d), lambda i: (i, 0))
    return pl.pallas_call(
        _post_kernel,
        grid=(t // tm,),
        in_specs=[tile, tile, pl.BlockSpec((d, d), lambda i: (0, 0)), row, row, row, row],
        out_specs=[tile, tile],
        out_shape=[jax.ShapeDtypeStruct((t, d), F32), jax.ShapeDtypeStruct((t, d), F32)],
        compiler_params=_params(("parallel",)),
        name="post",
    )(x2, merged, wo, g1, gn, sc, sh)


def _router_kernel(h_ref, wr_ref, br_ref, lg_ref):
    lg_ref[...] = jnp.dot(h_ref[...], wr_ref[...], preferred_element_type=F32,
                          precision=lax.Precision.HIGHEST) + br_ref[...]


def _router(h2, w_router, b_router, tm=512):
    t, d = h2.shape
    nr = w_router.shape[1]
    return pl.pallas_call(
        _router_kernel,
        grid=(t // tm,),
        in_specs=[pl.BlockSpec((tm, d), lambda i: (i, 0)),
                  pl.BlockSpec((d, nr), lambda i: (0, 0)), pl.BlockSpec((1, nr), lambda i: (0, 0))],
        out_specs=pl.BlockSpec((tm, nr), lambda i: (i, 0)),
        out_shape=jax.ShapeDtypeStruct((t, nr), F32),
        compiler_params=_params(("parallel",)),
        name="router",
    )(h2, w_router, b_router)


def _moe_kernel(tok_ref, dst_ref, be_ref, nu_ref,
                h_hbm, wrow_ref, w1_ref, w3_ref, w2_ref, z_hbm,
                x0, x1, y0, y1, w1b, w3b, w2b, gsem, ssem):
    b = pl.program_id(0)
    bm = x0.shape[0]
    nu = nu_ref[0]
    xs, ys = (x0, x1), (y0, y1)

    def gather_copy(sl, r, t):
        return pltpu.make_async_copy(h_hbm.at[pl.ds(t, 1), :], xs[sl].at[pl.ds(r, 1), :], gsem.at[sl])

    def scatter_copy(sl, r, d):
        return pltpu.make_async_copy(ys[sl].at[pl.ds(r, 1), :], z_hbm.at[pl.ds(d, 1), :], ssem.at[sl])

    def gather_start(blk, sl):
        for r in range(bm):
            gather_copy(sl, r, tok_ref[blk * bm + r]).start()

    def scatter_start(blk, sl):
        for r in range(bm):
            scatter_copy(sl, r, dst_ref[(blk + 2) * bm + r]).start()

    def gather_wait(sl):
        for r in range(bm):
            gather_copy(sl, r, 0).wait()

    def scatter_wait(sl):
        for r in range(bm):
            scatter_copy(sl, r, 0).wait()

    @pl.when(b == 0)
    def _():
        y0[...] = jnp.zeros(y0.shape, F32)
        y1[...] = jnp.zeros(y1.shape, F32)
        scatter_start(-2, 0)
        gather_start(0, 0)

    @pl.when((b < nu) & ((b == 0) | (be_ref[b] != be_ref[jnp.maximum(b - 1, 0)])))
    def _():
        w1b[...] = w1_ref[0].astype(BF16)
        w3b[...] = w3_ref[0].astype(BF16)
        w2b[...] = w2_ref[0].astype(BF16)

    def main(sl):
        gather_wait(sl)
        scatter_wait(sl)
        gather_start(b + 1, 1 - sl)
        scatter_start(b - 1, 1 - sl)
        x = xs[sl][...].astype(BF16)
        a1 = jnp.dot(x, w1b[...], preferred_element_type=F32)
        a3 = jnp.dot(x, w3b[...], preferred_element_type=F32)
        a = (a1 * jax.nn.sigmoid(a1)) * a3
        y = jnp.dot(a.astype(BF16), w2b[...], preferred_element_type=F32)
        ys[sl][...] = y * wrow_ref[...]

    def drain(sl):
        gather_wait(sl)
        scatter_wait(sl)
        scatter_start(b - 1, 1 - sl)
        scatter_wait(1 - sl)

    for sl in range(2):
        @pl.when((b < nu) & (b % 2 == sl))
        def _():
            main(sl)

        @pl.when((b == nu) & (b % 2 == sl))
        def _():
            drain(sl)


def _moe(h2, tok_buf, dst_buf, blk_e, n_used, w_rows, w1, w3, w2):
    t, d = h2.shape
    dff = w1.shape[2]
    bm = MOE_BM
    nsteps = blk_e.shape[0]
    assert tok_buf.shape[0] == nsteps * bm and dst_buf.shape[0] == (nsteps + 2) * bm
    grid_spec = pltpu.PrefetchScalarGridSpec(
        num_scalar_prefetch=4,
        grid=(nsteps,),
        in_specs=[
            pl.BlockSpec(memory_space=pl.ANY),
            pl.BlockSpec((bm, 1), lambda b, tok, dst, be, nu: (b, 0)),
            pl.BlockSpec((1, d, dff), lambda b, tok, dst, be, nu: (be[b], 0, 0)),
            pl.BlockSpec((1, d, dff), lambda b, tok, dst, be, nu: (be[b], 0, 0)),
            pl.BlockSpec((1, dff, d), lambda b, tok, dst, be, nu: (be[b], 0, 0)),
        ],
        out_specs=pl.BlockSpec(memory_space=pl.ANY),
        scratch_shapes=[pltpu.VMEM((bm, d), F32), pltpu.VMEM((bm, d), F32),
                        pltpu.VMEM((bm, d), F32), pltpu.VMEM((bm, d), F32),
                        pltpu.VMEM((d, dff), BF16), pltpu.VMEM((d, dff), BF16), pltpu.VMEM((dff, d), BF16),
                        pltpu.SemaphoreType.DMA((2,)), pltpu.SemaphoreType.DMA((2,))],
    )
    return pl.pallas_call(
        _moe_kernel,
        grid_spec=grid_spec,
        out_shape=jax.ShapeDtypeStruct((TOPK_IN_GROUP * t + 2 * bm, d), F32),
        compiler_params=pltpu.CompilerParams(dimension_semantics=("arbitrary",),
                                             vmem_limit_bytes=VMEM_LIMIT, has_side_effects=True),
        name="moe",
    )(tok_buf, dst_buf, blk_e, n_used, h2, w_rows, w1, w3, w2)


def _final_kernel(x1_ref, z0_ref, z1_ref, g2_ref, gn_ref, o_ref):
    x2 = x1_ref[...] + g2_ref[...] * (z0_ref[...] + z1_ref[...])
    y = x2 * lax.rsqrt(jnp.mean(x2 * x2, axis=-1, keepdims=True) + EPS)
    o_ref[...] = y * gn_ref[...]


def _final(x1, z, g2, gn, tm=256):
    t, d = x1.shape
    row = pl.BlockSpec((1, d), lambda i: (0, 0))
    return pl.pallas_call(
        _final_kernel,
        grid=(t // tm,),
        in_specs=[pl.BlockSpec((tm, d), lambda i: (i, 0)),
                  pl.BlockSpec((tm, d), lambda i: (i, 0)),
                  pl.BlockSpec((tm, d), lambda i: (t // tm + i, 0)), row, row],
        out_specs=pl.BlockSpec((tm, d), lambda i: (i, 0)),
        out_shape=jax.ShapeDtypeStruct((t, d), F32),
        compiler_params=_params(("parallel",)),
        name="final",
    )(x1, z, z, g2, gn)


def _route(logits, t):
    gl = logits[:, :N_GROUPS]
    el = logits[:, N_GROUPS:N_GROUPS + N_EXPERTS].reshape(t, N_GROUPS, EXP_PER_GROUP)
    g_prob = jax.nn.softmax(gl, axis=-1)
    grp = jnp.argmax(gl, axis=-1).astype(I32)
    p_grp = jnp.take_along_axis(g_prob, grp[:, None], axis=-1)[:, 0]
    e_in = jnp.take_along_axis(el, grp[:, None, None], axis=1)[:, 0]
    top_v, top_i = lax.top_k(e_in, TOPK_IN_GROUP)
    p_in = jax.nn.softmax(top_v, axis=-1)
    expert = grp[:, None] * EXP_PER_GROUP + top_i.astype(I32)
    weight = p_grp[:, None] * p_in

    bm = MOE_BM
    m = t * TOPK_IN_GROUP
    e_flat = expert.reshape(m)
    w_flat = weight.reshape(m)
    order = jnp.argsort(e_flat, stable=True).astype(I32)
    counts = jnp.bincount(e_flat, length=N_EXPERTS).astype(I32)
    start = jnp.cumsum(counts) - counts
    padded = ((counts + bm - 1) // bm) * bm
    pend = jnp.cumsum(padded)
    pstart = pend - padded
    nb = m // bm + N_EXPERTS + 1
    blk_e = jnp.minimum(jnp.searchsorted(pend, jnp.arange(nb, dtype=I32) * bm, side='right'),
                        N_EXPERTS - 1).astype(I32)
    n_used = (pend[-1] // bm).astype(I32).reshape(1)
    pos = jnp.arange(nb * bm, dtype=I32).reshape(nb, bm)
    src = (start[blk_e] - pstart[blk_e])[:, None] + pos
    valid = (src < (start + counts)[blk_e][:, None]) & (pos < pend[-1])
    a = order[jnp.clip(src, 0, m - 1)]
    tok = a // TOPK_IN_GROUP
    tok_buf = jnp.where(valid, tok, 0).reshape(-1)
    spare = m + pos % (2 * bm)
    dst_buf = jnp.where(valid, (a % TOPK_IN_GROUP) * t + tok, spare).reshape(-1)
    dst_buf = jnp.concatenate([m + jnp.arange(2 * bm, dtype=I32), dst_buf])
    w_rows = jnp.where(valid, w_flat[a], 0.0).reshape(-1, 1)
    return tok_buf, dst_buf, blk_e, n_used, w_rows


def kernel(x, c, w_ada, b_ada, norm_mix, w_in, t5_table, rel_table, w_up_a, w_up_b, w_o, norm_ffn,
           w_rg, b_rg, w_re, b_re, w1, w3, w2, norm_final):
    bn, s, d = x.shape
    assert bn == 1 and w_ada.shape[0] == 1
    assert s % 1024 == 0
    t = bn * s
    x2 = x.reshape(t, d)
    n_sel = min(TOPK_MAX, s // 4)

    mod = _ada(c.reshape(d, 1), w_ada[0], b_ada[0].reshape(1, 6 * d))
    sh1, sc1, g1, sh2, sc2, g2 = [mod[:, i * d:(i + 1) * d] for i in range(6)]

    h = _rms_mod(x2, norm_mix[0].reshape(1, d), sc1, sh1, BF16)

    cols = np.cumsum([0, W_A, W_A, W_A, W_IDX_Q, D_IDX, H_IDX, W_B, W_B, W_B, d, d])
    wsl = [w_in[0][:, cols[i]:cols[i + 1]] for i in range(11)]
    wqa, wka, wva, wqi, wki, wwi, wqb, wkb, wvb, wga, wgb = wsl
    w_main = jnp.concatenate([wqa * (LOG2E / math.sqrt(HD_A)), wka, wva, wqi,
                              wqb * (LOG2E / math.sqrt(HD_B)), wkb, wvb], axis=1).astype(BF16)
    w_gate = jnp.concatenate([wga, wgb], axis=1).astype(BF16)
    w_idx = jnp.concatenate([wki, wwi, jnp.zeros((d, LANES - D_IDX - H_IDX), F32)], axis=1).astype(BF16)

    proj = _matmul(h, w_main, BF16, 1024, 1024, "proj_main")
    gates = _matmul(h, w_gate, F32, 1024, 1024, "proj_gate")
    idx = _matmul(h, w_idx, F32, 1024, LANES, "proj_idx")
    nkt = s // SEL_TK
    kit3 = jnp.transpose(idx[:, :D_IDX].astype(BF16).reshape(nkt, SEL_TK, D_IDX), (0, 2, 1))
    wi = idx[:, D_IDX:D_IDX + H_IDX] * ((H_IDX ** -0.5) * (D_IDX ** -0.5))
    kt_a = jnp.transpose(proj[:, W_A:2 * W_A])
    kt_b = jnp.transpose(proj[:, 5 * W_A:6 * W_A])

    mask4 = _select(proj, wi, kit3, n_sel, qi_col_block=3)
    y_a = _dsa_attn(proj, kt_a, mask4, t5_table, q_col=0, v_col=2)
    y_b = _band_attn(proj, kt_b, rel_table[0], q_col=4, v_col=6)

    merged = _merge(y_a, y_b, gates, w_up_a[0].astype(BF16), w_up_b[0].astype(BF16))

    nr = LANES
    w_router = jnp.concatenate([w_rg[0], w_re[0], jnp.zeros((d, nr - N_GROUPS - N_EXPERTS), F32)], axis=1)
    b_router = jnp.concatenate([b_rg[0], b_re[0], jnp.zeros((nr - N_GROUPS - N_EXPERTS,), F32)]).reshape(1, nr)
    x1, h2 = _post(x2, merged, w_o[0].astype(BF16), g1, norm_ffn[0].reshape(1, d), sc2, sh2)
    logits = _router(h2, w_router, b_router)
    tok_buf, dst_buf, blk_e, n_used, w_rows = _route(logits, t)
    z = _moe(h2, tok_buf, dst_buf, blk_e, n_used, w_rows, w1[0], w3[0], w2[0])
    out = _final(x1, z, g2, norm_final.reshape(1, d))
    return out.reshape(bn, s, d)
```

```python
import functools
import math

import numpy as np
import jax
import jax.numpy as jnp
from jax import lax
from jax.experimental import pallas as pl
from jax.experimental.pallas import tpu as pltpu

F32 = jnp.float32
BF16 = jnp.bfloat16
I32 = jnp.int32

CHUNK = 64
EPS = 1e-6
H_A, HD_A = 8, 128
H_IDX, D_IDX = 16, 64
TOPK_MAX = 256
N_T5_BUCKETS = 32
T5_MAX_DIST = 1024
H_B, HD_B = 8, 128
N_LEFT_CHUNKS = 8
REL_CLIP = 128
N_GROUPS = 8
EXP_PER_GROUP = 8
N_EXPERTS = N_GROUPS * EXP_PER_GROUP
TOPK_IN_GROUP = 2

W_A = H_A * HD_A
W_B = H_B * HD_B
W_IDX_Q = H_IDX * D_IDX

NEG = -1e30
INT_MIN = -(2 ** 31)
KEY_FMAX = 0x7F7FFFFF
LOG2E = math.log2(math.e)

LANES = 128
VMEM_LIMIT = 56 * 1024 * 1024

SEL_TQ = 128
SEL_TK = 512
ATT_TQ = 256
BAND_TQ = 256
MOE_BM = 256


def _params(sem, vmem=VMEM_LIMIT):
    return pltpu.CompilerParams(dimension_semantics=sem, vmem_limit_bytes=vmem)


def _load_rows(ref, c):
    rows = ref.shape[0] // c
    return jnp.concatenate([ref[pl.ds(k, rows, stride=c), :] for k in range(c)], axis=1)


def _store_rows(ref, val):
    rows = val.shape[0]
    c = ref.shape[0] // rows
    for k in range(c):
        ref[pl.ds(k, rows, stride=c), :] = val[:, k * LANES:(k + 1) * LANES]


def _toeplitz(g, nrows, ncols):
    n = ncols + nrows - 1
    assert g.shape[-1] == n
    u = jnp.concatenate([g, jnp.zeros(g.shape[:-1] + (1,), g.dtype)], axis=-1)
    flat = jnp.tile(u, (1,) * (g.ndim - 1) + (nrows,))[..., :nrows * n]
    return flat.reshape(g.shape[:-1] + (nrows, n))[..., nrows - 1:]


def _ada_kernel(c_ref, w_ref, b_ref, o_ref, *, kc):
    d = w_ref.shape[0]
    tn = w_ref.shape[1]

    def body(k, acc):
        r0 = pl.multiple_of(k * kc, kc)
        cc = c_ref[pl.ds(r0, kc), :]
        ca = cc * jax.nn.sigmoid(cc)
        return acc + jnp.sum(w_ref[pl.ds(r0, kc), :] * ca, axis=0, keepdims=True)

    acc = lax.fori_loop(0, d // kc, body, jnp.zeros((1, tn), F32))
    o_ref[...] = acc + b_ref[...]


def _ada(c_col, w, b_row, tn=1024, kc=256):
    d, n = w.shape
    return pl.pallas_call(
        functools.partial(_ada_kernel, kc=kc),
        grid=(n // tn,),
        in_specs=[pl.BlockSpec((d, 1), lambda j: (0, 0)),
                  pl.BlockSpec((d, tn), lambda j: (0, j)),
                  pl.BlockSpec((1, tn), lambda j: (0, j))],
        out_specs=pl.BlockSpec((1, tn), lambda j: (0, j)),
        out_shape=jax.ShapeDtypeStruct((1, n), F32),
        compiler_params=_params(("arbitrary",)),
        name="ada",
    )(c_col, w, b_row)


def _rms_mod_kernel(x_ref, g_ref, sc_ref, sh_ref, o_ref):
    x = x_ref[...]
    y = x * lax.rsqrt(jnp.mean(x * x, axis=-1, keepdims=True) + EPS)
    o_ref[...] = (y * g_ref[...] * (1.0 + sc_ref[...]) + sh_ref[...]).astype(o_ref.dtype)


def _rms_mod(x2, g, sc, sh, out_dtype, tm=512):
    t, d = x2.shape
    row = pl.BlockSpec((1, d), lambda i: (0, 0))
    return pl.pallas_call(
        _rms_mod_kernel,
        grid=(t // tm,),
        in_specs=[pl.BlockSpec((tm, d), lambda i: (i, 0)), row, row, row],
        out_specs=pl.BlockSpec((tm, d), lambda i: (i, 0)),
        out_shape=jax.ShapeDtypeStruct((t, d), out_dtype),
        compiler_params=_params(("parallel",)),
        name="rms_mod",
    )(x2, g, sc, sh)


def _mm_kernel(a_ref, b_ref, o_ref):
    o_ref[...] = jnp.dot(a_ref[...], b_ref[...], preferred_element_type=F32).astype(o_ref.dtype)


def _matmul(a, b, out_dtype, tm, tn, name):
    m, k = a.shape
    n = b.shape[1]
    return pl.pallas_call(
        _mm_kernel,
        grid=(m // tm, n // tn),
        in_specs=[pl.BlockSpec((tm, k), lambda i, j: (i, 0)),
                  pl.BlockSpec((k, tn), lambda i, j: (0, j))],
        out_specs=pl.BlockSpec((tm, tn), lambda i, j: (i, j)),
        out_shape=jax.ShapeDtypeStruct((m, n), out_dtype),
        compiler_params=_params(("parallel", "arbitrary")),
        name=name,
    )(a, b)


def _sortable(x):
    bits = pltpu.bitcast(x, I32)
    return bits ^ ((bits >> 31) & 0x7FFFFFFF)


def _unsortable(k):
    return pltpu.bitcast(k ^ ((k >> 31) & 0x7FFFFFFF), F32)


def _select_kernel(qi_ref, wi_ref, kit_ref, mask_ref, key_sc, qh_sc, wb_sc, *, n_sel):
    nkt, tq, tk = key_sc.shape
    nsub = tk // LANES
    ngrp = -(-n_sel // LANES)
    i = pl.program_id(0)
    q0 = i * tq
    nvis = (q0 + tq + tk - 1) // tk

    for h in range(H_IDX):
        qh_sc[h * tq:(h + 1) * tq, :] = qi_ref[:, h * D_IDX:(h + 1) * D_IDX]
        wb_sc[h] = jnp.broadcast_to(wi_ref[:, h:h + 1], (tq, LANES))

    row = lax.broadcasted_iota(I32, (tq, 1), 0) + q0
    limit = (row // CHUNK + 1) * CHUNK
    lane = lax.broadcasted_iota(I32, (tq, LANES), 1)

    def score_tile(kt, gmax):
        k_t = kit_ref[kt]
        accs = [jnp.zeros((tq, LANES), F32) for _ in range(nsub)]
        s_all = jnp.dot(qh_sc[...], k_t, preferred_element_type=F32)
        for h in range(H_IDX):
            s = s_all[h * tq:(h + 1) * tq, :]
            wb = wb_sc[h]
            for a in range(nsub):
                accs[a] = accs[a] + wb * jnp.maximum(s[:, a * LANES:(a + 1) * LANES], 0.0)
        gmax = list(gmax)
        for a in range(nsub):
            col = lane + (kt * tk + a * LANES)
            key = jnp.where(col < limit, _sortable(accs[a]), INT_MIN)
            key_sc[kt, :, a * LANES:(a + 1) * LANES] = key
            gmax[a % ngrp] = jnp.maximum(gmax[a % ngrp], key)
        return tuple(gmax)

    gmax = lax.fori_loop(0, nvis, score_tile,
                         tuple(jnp.full((tq, LANES), INT_MIN, I32) for _ in range(ngrp)))

    def count(pred):
        def body(kt, c):
            for a in range(nsub):
                kk = key_sc[kt, :, a * LANES:(a + 1) * LANES]
                c = c + pred(kk, lane + (kt * tk + a * LANES)).astype(I32)
            return c
        c = lax.fori_loop(0, nvis, body, jnp.zeros((tq, LANES), I32))
        return jnp.sum(c, axis=1, keepdims=True)

    gmin, ghi = gmax[0], gmax[0]
    for g in gmax[1:]:
        gmin = jnp.minimum(gmin, g)
        ghi = jnp.maximum(ghi, g)
    lo0 = jnp.maximum(jnp.min(gmin, axis=1, keepdims=True), INT_MIN + 1)
    hi0 = jnp.max(ghi, axis=1, keepdims=True)
    few = limit < n_sel
    lo0 = jnp.where(few, INT_MIN + 1, lo0)
    hi0 = jnp.where(few, INT_MIN + 1, hi0)
    unknown = jnp.full((tq, 1), 2 ** 30, I32)

    def bis_cond(st):
        lo, hi, _, _ = st
        return jnp.max((lo < hi).astype(I32)) > 0

    def bis_body(st):
        lo, hi, c_lo, c_hi1 = st
        active = lo < hi
        mid = (lo | hi) - ((lo ^ hi) >> 1)
        fin = lambda k: jnp.clip(k, -KEY_FMAX - 1, KEY_FMAX)
        vmid = _sortable(0.5 * _unsortable(fin(lo)) + 0.5 * _unsortable(fin(hi)))
        mid = jnp.where((vmid > lo) & (vmid <= hi), vmid, mid)
        c = count(lambda kk, col: kk >= mid)
        up = active & (c >= n_sel)
        dn = active & (c < n_sel)
        hit = active & (c == n_sel)
        lo = jnp.where(up, mid, lo)
        c_lo = jnp.where(up, c, c_lo)
        hi = jnp.where(dn, mid - 1, jnp.where(hit, mid, hi))
        c_hi1 = jnp.where(dn, c, c_hi1)
        return lo, hi, c_lo, c_hi1

    thr, _, n_ge, n_gt = lax.while_loop(bis_cond, bis_body,
                                        (lo0, hi0, unknown, jnp.zeros((tq, 1), I32)))

    excess = (n_ge > n_sel) & jnp.logical_not(few)
    need = n_sel - n_gt
    ncol = nkt * tk

    def tie_cut():
        nbits = max(1, int(math.ceil(math.log2(ncol))))

        def cut_body(b, p):
            cand = p | jnp.left_shift(jnp.int32(1), nbits - 1 - b)
            cnt = count(lambda kk, col: (kk == thr) & (col < cand))
            return jnp.where(cnt < need, cand, p)

        p = lax.fori_loop(0, nbits, cut_body, jnp.zeros((tq, 1), I32))
        return p + 1

    cut = lax.cond(jnp.max(excess.astype(I32)) > 0, tie_cut, lambda: jnp.full((tq, 1), ncol, I32))
    cut = jnp.where(excess, cut, ncol)

    def write_tile(kt, carry):
        kk = key_sc[kt]
        col = lax.broadcasted_iota(I32, (tq, tk), 1) + kt * tk
        sel = (kk > thr) | ((kk == thr) & (col < cut))
        mask_ref[kt] = jnp.where(sel, 0.0, NEG).astype(mask_ref.dtype)
        return carry

    lax.fori_loop(0, nvis, write_tile, 0)

    def fill_tile(kt, carry):
        mask_ref[kt] = jnp.full((tq, tk), NEG, mask_ref.dtype)
        return carry

    lax.fori_loop(nvis, nkt, fill_tile, 0)


def _select(proj, wi, kit3, n_sel, qi_col_block):
    s = proj.shape[0]
    nkt, _, tk = kit3.shape
    tq = SEL_TQ
    nqb = s // tq
    assert -(-n_sel // LANES) <= tk // LANES
    return pl.pallas_call(
        functools.partial(_select_kernel, n_sel=n_sel),
        grid=(nqb,),
        in_specs=[pl.BlockSpec((tq, W_IDX_Q), lambda i: (i, qi_col_block)),
                  pl.BlockSpec((tq, H_IDX), lambda i: (i, 0)),
                  pl.BlockSpec((nkt, D_IDX, tk), lambda i: (0, 0, 0))],
        out_specs=pl.BlockSpec((None, nkt, tq, tk), lambda i: (i, 0, 0, 0)),
        out_shape=jax.ShapeDtypeStruct((nqb, nkt, tq, tk), BF16),
        scratch_shapes=[pltpu.VMEM((nkt, tq, tk), I32),
                        pltpu.VMEM((H_IDX * tq, D_IDX), BF16),
                        pltpu.VMEM((H_IDX, tq, LANES), F32)],
        compiler_params=_params(("arbitrary",)),
        name="dsa_select",
    )(proj, wi, kit3)


def _dsa_attn_kernel(qb_ref, kb_ref, first_ref, last_ref, slab_ref,
                     q_ref, kt_ref, v_ref, mask_ref, band_ref, far_ref, o_ref,
                     m_sc, l_sc, acc_sc):
    step = pl.program_id(0)
    tk = kt_ref.shape[1]
    nslab = tk // LANES
    slab0 = slab_ref[step]

    @pl.when(first_ref[step] == 1)
    def _():
        m_sc[...] = jnp.full(m_sc.shape, NEG, F32)
        l_sc[...] = jnp.zeros(l_sc.shape, F32)
        acc_sc[...] = jnp.zeros(acc_sc.shape, F32)

    def tile(near):
        mask = jnp.concatenate([mask_ref[r] for r in range(mask_ref.shape[0])], axis=0).astype(F32)
        ones = jnp.ones((tk, HD_A), BF16)
        for h in range(H_A):
            hs = slice(h * HD_A, (h + 1) * HD_A)
            s = jnp.dot(q_ref[:, hs], kt_ref[hs, :], preferred_element_type=F32) + mask
            if near:
                s = s + jnp.concatenate([band_ref[h, slab0 + a] for a in range(nslab)], axis=1)
                shift = 0.0
            else:
                shift = far_ref[h]
            smax = s[:, 0:LANES]
            for a in range(1, nslab):
                smax = jnp.maximum(smax, s[:, a * LANES:(a + 1) * LANES])
            m_old = m_sc[h]
            m_new = jnp.maximum(m_old, jnp.max(smax, axis=1, keepdims=True) + shift)
            alpha = jnp.exp2(m_old - m_new)
            mm = m_new - shift
            p = jnp.concatenate([jnp.exp2(s[:, a * LANES:(a + 1) * LANES] - mm) for a in range(nslab)],
                                axis=1).astype(BF16)
            pv = jnp.dot(p, jnp.concatenate([v_ref[:, hs], ones], axis=1), preferred_element_type=F32)
            acc_sc[:, hs] = alpha * acc_sc[:, hs] + pv[:, 0:HD_A]
            l_sc[h] = alpha * l_sc[h] + pv[:, HD_A:2 * HD_A]
            m_sc[h] = m_new

    @pl.when(slab0 >= 0)
    def _():
        tile(True)

    @pl.when(slab0 < 0)
    def _():
        tile(False)

    @pl.when(last_ref[step] == 1)
    def _():
        for h in range(H_A):
            hs = slice(h * HD_A, (h + 1) * HD_A)
            o_ref[:, hs] = (acc_sc[:, hs] / l_sc[h]).astype(o_ref.dtype)


def _t5_bucket_np(rel):
    nb = N_T5_BUCKETS // 2
    ret = (rel > 0).astype(np.int32) * nb
    n = np.abs(rel)
    max_exact = nb // 2
    nf = np.maximum(n, 1).astype(np.float32)
    large = max_exact + (np.log(nf / np.float32(max_exact)) / np.float32(math.log(T5_MAX_DIST / max_exact))
                         * np.float32(nb - max_exact)).astype(np.int32)
    large = np.minimum(large, nb - 1)
    return ret + np.where(n < max_exact, n, large)


def _dsa_plan(s, tq, tk):
    nqb = s // tq
    d_all = np.arange(-(s - 1), CHUNK, dtype=np.int64)
    b_all = _t5_bucket_np(d_all)
    far_bucket = int(b_all[0])
    varying = np.nonzero(b_all != far_bucket)[0]
    d_lo = int(d_all[varying[0]])
    o_min = min(int(math.ceil((d_lo - (tk - 1)) / LANES)) * LANES, 0)
    n_slabs = (-o_min + tk) // LANES
    qb, kb, first, last, slab = [], [], [], [], []
    for i in range(nqb):
        q0 = i * tq
        nvis = (q0 + tq + tk - 1) // tk
        for j in range(nvis):
            o = j * tk - q0
            qb.append(i)
            kb.append(j)
            first.append(1 if j == 0 else 0)
            last.append(1 if j == nvis - 1 else 0)
            slab.append(-1 if o < o_min else (o - o_min) // LANES)
    u = np.arange(n_slabs * LANES + tq - 1)
    g_bucket = _t5_bucket_np(np.clip(o_min + u - (tq - 1), -(s - 1), None))
    plan = tuple(np.asarray(a, np.int32) for a in (qb, kb, first, last, slab))
    return plan, g_bucket, far_bucket, n_slabs


def _dsa_attn(proj, kt, mask4, t5_table, q_col, v_col):
    s = proj.shape[0]
    tq, tk = ATT_TQ, SEL_TK
    (qb, kb, first, last, slab), g_bucket, far_bucket, n_slabs = _dsa_plan(s, tq, tk)
    nsteps = len(qb)
    t5l = t5_table * LOG2E
    g = jnp.transpose(t5l[g_bucket], (1, 0))
    band = _toeplitz(g, tq, n_slabs * LANES)
    band = jnp.transpose(band.reshape(H_A, tq, n_slabs, LANES), (0, 2, 1, 3))
    far = t5l[far_bucket]
    rq = tq // SEL_TQ
    grid_spec = pltpu.PrefetchScalarGridSpec(
        num_scalar_prefetch=5,
        grid=(nsteps,),
        in_specs=[
            pl.BlockSpec((tq, W_A), lambda t, qb, kb, f, l, sl: (qb[t], q_col)),
            pl.BlockSpec((W_A, tk), lambda t, qb, kb, f, l, sl: (0, kb[t])),
            pl.BlockSpec((tk, W_A), lambda t, qb, kb, f, l, sl: (kb[t], v_col)),
            pl.BlockSpec((rq, None, SEL_TQ, tk), lambda t, qb, kb, f, l, sl: (qb[t], kb[t], 0, 0)),
            pl.BlockSpec((H_A, n_slabs, tq, LANES), lambda t, qb, kb, f, l, sl: (0, 0, 0, 0)),
            pl.BlockSpec(memory_space=pltpu.SMEM),
        ],
        out_specs=pl.BlockSpec((tq, W_A), lambda t, qb, kb, f, l, sl: (qb[t], 0)),
        scratch_shapes=[pltpu.VMEM((H_A, tq, LANES), F32),
                        pltpu.VMEM((H_A, tq, LANES), F32),
                        pltpu.VMEM((tq, W_A), F32)],
    )
    return pl.pallas_call(
        _dsa_attn_kernel,
        grid_spec=grid_spec,
        out_shape=jax.ShapeDtypeStruct((s, W_A), BF16),
        compiler_params=_params(("arbitrary",)),
        name="dsa_attn",
    )(jnp.asarray(qb), jnp.asarray(kb), jnp.asarray(first), jnp.asarray(last), jnp.asarray(slab),
      proj, kt, proj, mask4, band, far)


def _band_kernel(q_ref, k0_ref, k1_ref, k2_ref, v0_ref, v1_ref, v2_ref, bias_ref, o_ref):
    i = pl.program_id(0)
    tq = q_ref.shape[0]
    wk = 3 * tq
    col = lax.broadcasted_iota(I32, (tq, wk), 1)
    start_mask = jnp.where(col + (i - 2) * tq >= 0, 0.0, NEG)
    for h in range(H_B):
        hs = slice(h * HD_B, (h + 1) * HD_B)
        kc = jnp.concatenate([k0_ref[hs, :], k1_ref[hs, :], k2_ref[hs, :]], axis=1)
        vc = jnp.concatenate([v0_ref[:, hs], v1_ref[:, hs], v2_ref[:, hs]], axis=0)
        s = jnp.dot(q_ref[:, hs], kc, preferred_element_type=F32) + bias_ref[h] + start_mask
        m = jnp.max(s, axis=1, keepdims=True)
        p = jnp.exp2(s - m)
        l = jnp.sum(p, axis=1, keepdims=True)
        pv = jnp.dot(p.astype(BF16), vc, preferred_element_type=F32)
        o_ref[:, hs] = (pv / l).astype(o_ref.dtype)


def _band_bias(rel_table, tq):
    assert 2 * tq >= N_LEFT_CHUNKS * CHUNK and tq % CHUNK == 0
    wk = 3 * tq
    x = np.arange(wk + tq - 1)
    idx = np.clip(2 * tq + (tq - 1) - x, -REL_CLIP, REL_CLIP) + REL_CLIP
    bias = _toeplitz(rel_table[:, idx] * LOG2E, tq, wk)
    r = np.arange(tq)[:, None]
    c = np.arange(wk)[None, :]
    dchunk = (c - 2 * tq) // CHUNK - r // CHUNK
    in_band = (dchunk <= 0) & (dchunk >= -N_LEFT_CHUNKS)
    return jnp.where(jnp.asarray(in_band)[None], bias, NEG).astype(F32)


def _band_attn(proj, kt, rel_table, q_col, v_col):
    s = proj.shape[0]
    tq = BAND_TQ
    bias = _band_bias(rel_table, tq)

    def kspec(back):
        return pl.BlockSpec((W_B, tq), lambda i: (0, jnp.maximum(i - back, 0)))

    def vspec(back):
        return pl.BlockSpec((tq, W_B), lambda i: (jnp.maximum(i - back, 0), v_col))

    return pl.pallas_call(
        _band_kernel,
        grid=(s // tq,),
        in_specs=[pl.BlockSpec((tq, W_B), lambda i: (i, q_col)),
                  kspec(2), kspec(1), kspec(0), vspec(2), vspec(1), vspec(0),
                  pl.BlockSpec((H_B, tq, 3 * tq), lambda i: (0, 0, 0))],
        out_specs=pl.BlockSpec((tq, W_B), lambda i: (i, 0)),
        out_shape=jax.ShapeDtypeStruct((s, W_B), BF16),
        compiler_params=_params(("parallel",)),
        name="band_attn",
    )(proj, kt, kt, kt, proj, proj, proj, bias)


def _merge_kernel(ya_ref, yb_ref, ga_ref, gb_ref, wa_ref, wb_ref, o_ref):
    ua = jnp.dot(ya_ref[...], wa_ref[...], preferred_element_type=F32)
    ub = jnp.dot(yb_ref[...], wb_ref[...], preferred_element_type=F32)
    o_ref[...] = (jax.nn.sigmoid(ga_ref[...]) * ua + jax.nn.sigmoid(gb_ref[...]) * ub).astype(o_ref.dtype)


def _merge(ya, yb, gates, wa, wb, tm=256):
    s, d = ya.shape[0], wa.shape[1]
    return pl.pallas_call(
        _merge_kernel,
        grid=(s // tm,),
        in_specs=[pl.BlockSpec((tm, W_A), lambda i: (i, 0)),
                  pl.BlockSpec((tm, W_B), lambda i: (i, 0)),
                  pl.BlockSpec((tm, d), lambda i: (i, 0)),
                  pl.BlockSpec((tm, d), lambda i: (i, 1)),
                  pl.BlockSpec((W_A, d), lambda i: (0, 0)),
                  pl.BlockSpec((W_B, d), lambda i: (0, 0))],
        out_specs=pl.BlockSpec((tm, d), lambda i: (i, 0)),
        out_shape=jax.ShapeDtypeStruct((s, d), BF16),
        compiler_params=_params(("parallel",)),
        name="merge",
    )(ya, yb, gates, gates, wa, wb)


def _post_kernel(x_ref, m_ref, wo_ref, g1_ref, gn_ref, sc_ref, sh_ref, x1_ref, h2_ref):
    x1 = x_ref[...] + g1_ref[...] * jnp.dot(m_ref[...], wo_ref[...], preferred_element_type=F32)
    x1_ref[...] = x1
    y = x1 * lax.rsqrt(jnp.mean(x1 * x1, axis=-1, keepdims=True) + EPS)
    _store_rows(h2_ref, y * gn_ref[...] * (1.0 + sc_ref[...]) + sh_ref[...])


def _post(x2, merged, wo, g1, gn, sc, sh, tm=256):
    t, d = x2.shape
    row = pl.BlockSpec((1, d), lambda i: (0, 0))
    tile = pl.BlockSpec((tm, d), lambda i: (i, 0))
    return pl.pallas_call(
        _post_kernel,
        grid=(t // tm,),
        in_specs=[tile, tile, pl.BlockSpec((d, d), lambda i: (0, 0)), row, row, row, row],
        out_specs=[tile, pl.BlockSpec((tm * (d // LANES), LANES), lambda i: (i, 0))],
        out_shape=[jax.ShapeDtypeStruct((t, d), F32), jax.ShapeDtypeStruct((t * (d // LANES), LANES), F32)],
        compiler_params=_params(("parallel",)),
        name="post",
    )(x2, merged, wo, g1, gn, sc, sh)


def _router_kernel(h_ref, wr_ref, br_ref, lg_ref):
    lg_ref[...] = jnp.dot(_load_rows(h_ref, wr_ref.shape[0] // LANES), wr_ref[...], preferred_element_type=F32,
                          precision=lax.Precision.HIGHEST) + br_ref[...]


def _router(h2, w_router, b_router, tm=512):
    d, nr = w_router.shape
    t = h2.shape[0] // (d // LANES)
    return pl.pallas_call(
        _router_kernel,
        grid=(t // tm,),
        in_specs=[pl.BlockSpec((tm * (d // LANES), LANES), lambda i: (i, 0)),
                  pl.BlockSpec((d, nr), lambda i: (0, 0)), pl.BlockSpec((1, nr), lambda i: (0, 0))],
        out_specs=pl.BlockSpec((tm, nr), lambda i: (i, 0)),
        out_shape=jax.ShapeDtypeStruct((t, nr), F32),
        compiler_params=_params(("parallel",)),
        name="router",
    )(h2, w_router, b_router)


def _moe_kernel(tok_ref, dst_ref, be_ref, nu_ref,
                h_hbm, wrow_ref, w1_ref, w3_ref, w2_ref, z_hbm,
                x0, x1, y0, y1, w1b, w3b, w2b, gsem, ssem):
    b = pl.program_id(0)
    c = w1b.shape[0] // LANES
    bm = x0.shape[0] // c
    nu = nu_ref[0]
    xs, ys = (x0, x1), (y0, y1)

    def gather_copy(sl, r, t):
        return pltpu.make_async_copy(h_hbm.at[pl.ds(t * c, c), :], xs[sl].at[pl.ds(r * c, c), :], gsem.at[sl])

    def scatter_copy(sl, r, d):
        return pltpu.make_async_copy(ys[sl].at[pl.ds(r * c, c), :], z_hbm.at[pl.ds(d * c, c), :], ssem.at[sl])

    def gather_start(blk, sl):
        for r in range(bm):
            gather_copy(sl, r, tok_ref[blk * bm + r]).start()

    def scatter_start(blk, sl):
        for r in range(bm):
            scatter_copy(sl, r, dst_ref[(blk + 2) * bm + r]).start()

    def gather_wait(sl):
        for r in range(bm):
            gather_copy(sl, r, 0).wait()

    def scatter_wait(sl):
        for r in range(bm):
            scatter_copy(sl, r, 0).wait()

    @pl.when(b == 0)
    def _():
        y0[...] = jnp.zeros(y0.shape, F32)
        y1[...] = jnp.zeros(y1.shape, F32)
        scatter_start(-2, 0)
        gather_start(0, 0)

    @pl.when((b < nu) & ((b == 0) | (be_ref[b] != be_ref[jnp.maximum(b - 1, 0)])))
    def _():
        w1b[...] = w1_ref[0].astype(BF16)
        w3b[...] = w3_ref[0].astype(BF16)
        w2b[...] = w2_ref[0].astype(BF16)

    def main(sl):
        gather_wait(sl)
        scatter_wait(sl)
        gather_start(b + 1, 1 - sl)
        scatter_start(b - 1, 1 - sl)
        x = _load_rows(xs[sl], c).astype(BF16)
        a1 = jnp.dot(x, w1b[...], preferred_element_type=F32)
        a3 = jnp.dot(x, w3b[...], preferred_element_type=F32)
        a = (a1 * jax.nn.sigmoid(a1)) * a3
        y = jnp.dot(a.astype(BF16), w2b[...], preferred_element_type=F32)
        _store_rows(ys[sl], y * wrow_ref[...])

    def drain(sl):
        gather_wait(sl)
        scatter_wait(sl)
        scatter_start(b - 1, 1 - sl)
        scatter_wait(1 - sl)

    for sl in range(2):
        @pl.when((b < nu) & (b % 2 == sl))
        def _():
            main(sl)

        @pl.when((b == nu) & (b % 2 == sl))
        def _():
            drain(sl)


def _moe(h2, tok_buf, dst_buf, blk_e, n_used, w_rows, w1, w3, w2):
    d, dff = w1.shape[1:]
    c = d // LANES
    t = h2.shape[0] // c
    bm = MOE_BM
    nsteps = blk_e.shape[0]
    assert tok_buf.shape[0] == nsteps * bm and dst_buf.shape[0] == (nsteps + 2) * bm
    grid_spec = pltpu.PrefetchScalarGridSpec(
        num_scalar_prefetch=4,
        grid=(nsteps,),
        in_specs=[
            pl.BlockSpec(memory_space=pl.ANY),
            pl.BlockSpec((bm, 1), lambda b, tok, dst, be, nu: (b, 0)),
            pl.BlockSpec((1, d, dff), lambda b, tok, dst, be, nu: (be[b], 0, 0)),
            pl.BlockSpec((1, d, dff), lambda b, tok, dst, be, nu: (be[b], 0, 0)),
            pl.BlockSpec((1, dff, d), lambda b, tok, dst, be, nu: (be[b], 0, 0)),
        ],
        out_specs=pl.BlockSpec(memory_space=pl.ANY),
        scratch_shapes=[pltpu.VMEM((bm * c, LANES), F32), pltpu.VMEM((bm * c, LANES), F32),
                        pltpu.VMEM((bm * c, LANES), F32), pltpu.VMEM((bm * c, LANES), F32),
                        pltpu.VMEM((d, dff), BF16), pltpu.VMEM((d, dff), BF16), pltpu.VMEM((dff, d), BF16),
                        pltpu.SemaphoreType.DMA((2,)), pltpu.SemaphoreType.DMA((2,))],
    )
    return pl.pallas_call(
        _moe_kernel,
        grid_spec=grid_spec,
        out_shape=jax.ShapeDtypeStruct(((TOPK_IN_GROUP * t + 2 * bm) * c, LANES), F32),
        compiler_params=pltpu.CompilerParams(dimension_semantics=("arbitrary",),
                                             vmem_limit_bytes=VMEM_LIMIT, has_side_effects=True),
        name="moe",
    )(tok_buf, dst_buf, blk_e, n_used, h2, w_rows, w1, w3, w2)


def _final_kernel(x1_ref, z0_ref, z1_ref, g2_ref, gn_ref, o_ref):
    c = x1_ref.shape[1] // LANES
    x2 = x1_ref[...] + g2_ref[...] * (_load_rows(z0_ref, c) + _load_rows(z1_ref, c))
    y = x2 * lax.rsqrt(jnp.mean(x2 * x2, axis=-1, keepdims=True) + EPS)
    o_ref[...] = y * gn_ref[...]


def _final(x1, z, g2, gn, tm=256):
    t, d = x1.shape
    row = pl.BlockSpec((1, d), lambda i: (0, 0))
    return pl.pallas_call(
        _final_kernel,
        grid=(t // tm,),
        in_specs=[pl.BlockSpec((tm, d), lambda i: (i, 0)),
                  pl.BlockSpec((tm * (d // LANES), LANES), lambda i: (i, 0)),
                  pl.BlockSpec((tm * (d // LANES), LANES), lambda i: (t // tm + i, 0)), row, row],
        out_specs=pl.BlockSpec((tm, d), lambda i: (i, 0)),
        out_shape=jax.ShapeDtypeStruct((t, d), F32),
        compiler_params=_params(("parallel",)),
        name="final",
    )(x1, z, z, g2, gn)


def _route(logits, t):
    gl = logits[:, :N_GROUPS]
    el = logits[:, N_GROUPS:N_GROUPS + N_EXPERTS].reshape(t, N_GROUPS, EXP_PER_GROUP)
    g_prob = jax.nn.softmax(gl, axis=-1)
    grp = jnp.argmax(gl, axis=-1).astype(I32)
    p_grp = jnp.take_along_axis(g_prob, grp[:, None], axis=-1)[:, 0]
    e_in = jnp.take_along_axis(el, grp[:, None, None], axis=1)[:, 0]
    top_v, top_i = lax.top_k(e_in, TOPK_IN_GROUP)
    p_in = jax.nn.softmax(top_v, axis=-1)
    expert = grp[:, None] * EXP_PER_GROUP + top_i.astype(I32)
    weight = p_grp[:, None] * p_in

    bm = MOE_BM
    m = t * TOPK_IN_GROUP
    e_flat = expert.reshape(m)
    w_flat = weight.reshape(m)
    order = jnp.argsort(e_flat, stable=True).astype(I32)
    counts = jnp.bincount(e_flat, length=N_EXPERTS).astype(I32)
    start = jnp.cumsum(counts) - counts
    padded = ((counts + bm - 1) // bm) * bm
    pend = jnp.cumsum(padded)
    pstart = pend - padded
    nb = m // bm + N_EXPERTS + 1
    blk_e = jnp.minimum(jnp.searchsorted(pend, jnp.arange(nb, dtype=I32) * bm, side='right'),
                        N_EXPERTS - 1).astype(I32)
    n_used = (pend[-1] // bm).astype(I32).reshape(1)
    pos = jnp.arange(nb * bm, dtype=I32).reshape(nb, bm)
    src = (start[blk_e] - pstart[blk_e])[:, None] + pos
    valid = (src < (start + counts)[blk_e][:, None]) & (pos < pend[-1])
    a = order[jnp.clip(src, 0, m - 1)]
    tok = a // TOPK_IN_GROUP
    tok_buf = jnp.where(valid, tok, 0).reshape(-1)
    spare = m + pos % (2 * bm)
    dst_buf = jnp.where(valid, (a % TOPK_IN_GROUP) * t + tok, spare).reshape(-1)
    dst_buf = jnp.concatenate([m + jnp.arange(2 * bm, dtype=I32), dst_buf])
    w_rows = jnp.where(valid, w_flat[a], 0.0).reshape(-1, 1)
    return tok_buf, dst_buf, blk_e, n_used, w_rows


def kernel(x, c, w_ada, b_ada, norm_mix, w_in, t5_table, rel_table, w_up_a, w_up_b, w_o, norm_ffn,
           w_rg, b_rg, w_re, b_re, w1, w3, w2, norm_final):
    bn, s, d = x.shape
    assert bn == 1 and w_ada.shape[0] == 1
    assert s % 1024 == 0
    t = bn * s
    x2 = x.reshape(t, d)
    n_sel = min(TOPK_MAX, s // 4)

    mod = _ada(c.reshape(d, 1), w_ada[0], b_ada[0].reshape(1, 6 * d))
    sh1, sc1, g1, sh2, sc2, g2 = [mod[:, i * d:(i + 1) * d] for i in range(6)]

    h = _rms_mod(x2, norm_mix[0].reshape(1, d), sc1, sh1, BF16)

    cols = np.cumsum([0, W_A, W_A, W_A, W_IDX_Q, D_IDX, H_IDX, W_B, W_B, W_B, d, d])
    wsl = [w_in[0][:, cols[i]:cols[i + 1]] for i in range(11)]
    wqa, wka, wva, wqi, wki, wwi, wqb, wkb, wvb, wga, wgb = wsl
    w_main = jnp.concatenate([wqa * (LOG2E / math.sqrt(HD_A)), wka, wva, wqi,
                              wqb * (LOG2E / math.sqrt(HD_B)), wkb, wvb], axis=1).astype(BF16)
    w_gate = jnp.concatenate([wga, wgb], axis=1).astype(BF16)
    w_idx = jnp.concatenate([wki, wwi, jnp.zeros((d, LANES - D_IDX - H_IDX), F32)], axis=1).astype(BF16)

    proj = _matmul(h, w_main, BF16, 1024, 1024, "proj_main")
    gates = _matmul(h, w_gate, F32, 1024, 1024, "proj_gate")
    idx = _matmul(h, w_idx, F32, 1024, LANES, "proj_idx")
    nkt = s // SEL_TK
    kit3 = jnp.transpose(idx[:, :D_IDX].astype(BF16).reshape(nkt, SEL_TK, D_IDX), (0, 2, 1))
    wi = idx[:, D_IDX:D_IDX + H_IDX] * ((H_IDX ** -0.5) * (D_IDX ** -0.5))
    kt_a = jnp.transpose(proj[:, W_A:2 * W_A])
    kt_b = jnp.transpose(proj[:, 5 * W_A:6 * W_A])

    mask4 = _select(proj, wi, kit3, n_sel, qi_col_block=3)
    y_a = _dsa_attn(proj, kt_a, mask4, t5_table, q_col=0, v_col=2)
    y_b = _band_attn(proj, kt_b, rel_table[0], q_col=4, v_col=6)

    merged = _merge(y_a, y_b, gates, w_up_a[0].astype(BF16), w_up_b[0].astype(BF16))

    nr = LANES
    w_router = jnp.concatenate([w_rg[0], w_re[0], jnp.zeros((d, nr - N_GROUPS - N_EXPERTS), F32)], axis=1)
    b_router = jnp.concatenate([b_rg[0], b_re[0], jnp.zeros((nr - N_GROUPS - N_EXPERTS,), F32)]).reshape(1, nr)
    x1, h2 = _post(x2, merged, w_o[0].astype(BF16), g1, norm_ffn[0].reshape(1, d), sc2, sh2)
    logits = _router(h2, w_router, b_router)
    tok_buf, dst_buf, blk_e, n_used, w_rows = _route(logits, t)
    z = _moe(h2, tok_buf, dst_buf, blk_e, n_used, w_rows, w1[0], w3[0], w2[0])
    out = _final(x1, z, g2, norm_final.reshape(1, d))
    return out.reshape(bn, s, d)
```

```python
import functools
import math

import numpy as np
import jax
import jax.numpy as jnp
from jax import lax
from jax.experimental import pallas as pl
from jax.experimental.pallas import tpu as pltpu

F32 = jnp.float32
BF16 = jnp.bfloat16
I32 = jnp.int32

CHUNK = 64
EPS = 1e-6
H_A, HD_A = 8, 128
H_IDX, D_IDX = 16, 64
TOPK_MAX = 256
N_T5_BUCKETS = 32
T5_MAX_DIST = 1024
H_B, HD_B = 8, 128
N_LEFT_CHUNKS = 8
REL_CLIP = 128
N_GROUPS = 8
EXP_PER_GROUP = 8
N_EXPERTS = N_GROUPS * EXP_PER_GROUP
TOPK_IN_GROUP = 2

W_A = H_A * HD_A
W_B = H_B * HD_B
W_IDX_Q = H_IDX * D_IDX

NEG = -1e30
INT_MIN = -(2 ** 31)
KEY_FMAX = 0x7F7FFFFF
LOG2E = math.log2(math.e)

LANES = 128
VMEM_LIMIT = 56 * 1024 * 1024

SEL_TQ = 128
SEL_TK = 512
ATT_TQ = 256
BAND_TQ = 256
MOE_BM = 256


def _params(sem, vmem=VMEM_LIMIT):
    return pltpu.CompilerParams(dimension_semantics=sem, vmem_limit_bytes=vmem)


def _load_rows(ref, c):
    rows = ref.shape[0] // c
    return jnp.concatenate([ref[pl.ds(k, rows, stride=c), :] for k in range(c)], axis=1)


def _store_rows(ref, val):
    rows = val.shape[0]
    c = ref.shape[0] // rows
    for k in range(c):
        ref[pl.ds(k, rows, stride=c), :] = val[:, k * LANES:(k + 1) * LANES]


def _toeplitz(g, nrows, ncols):
    n = ncols + nrows - 1
    assert g.shape[-1] == n
    u = jnp.concatenate([g, jnp.zeros(g.shape[:-1] + (1,), g.dtype)], axis=-1)
    flat = jnp.tile(u, (1,) * (g.ndim - 1) + (nrows,))[..., :nrows * n]
    return flat.reshape(g.shape[:-1] + (nrows, n))[..., nrows - 1:]


def _ada_kernel(c_ref, w_ref, b_ref, o_ref, *, kc):
    d = w_ref.shape[0]
    tn = w_ref.shape[1]

    def body(k, acc):
        r0 = pl.multiple_of(k * kc, kc)
        cc = c_ref[pl.ds(r0, kc), :]
        ca = cc * jax.nn.sigmoid(cc)
        return acc + jnp.sum(w_ref[pl.ds(r0, kc), :] * ca, axis=0, keepdims=True)

    acc = lax.fori_loop(0, d // kc, body, jnp.zeros((1, tn), F32))
    o_ref[...] = acc + b_ref[...]


def _ada(c_col, w, b_row, tn=1024, kc=256):
    d, n = w.shape
    return pl.pallas_call(
        functools.partial(_ada_kernel, kc=kc),
        grid=(n // tn,),
        in_specs=[pl.BlockSpec((d, 1), lambda j: (0, 0)),
                  pl.BlockSpec((d, tn), lambda j: (0, j)),
                  pl.BlockSpec((1, tn), lambda j: (0, j))],
        out_specs=pl.BlockSpec((1, tn), lambda j: (0, j)),
        out_shape=jax.ShapeDtypeStruct((1, n), F32),
        compiler_params=_params(("arbitrary",)),
        name="ada",
    )(c_col, w, b_row)


def _rms_mod_kernel(x_ref, g_ref, sc_ref, sh_ref, o_ref):
    x = x_ref[...]
    y = x * lax.rsqrt(jnp.mean(x * x, axis=-1, keepdims=True) + EPS)
    o_ref[...] = (y * g_ref[...] * (1.0 + sc_ref[...]) + sh_ref[...]).astype(o_ref.dtype)


def _rms_mod(x2, g, sc, sh, out_dtype, tm=512):
    t, d = x2.shape
    row = pl.BlockSpec((1, d), lambda i: (0, 0))
    return pl.pallas_call(
        _rms_mod_kernel,
        grid=(t // tm,),
        in_specs=[pl.BlockSpec((tm, d), lambda i: (i, 0)), row, row, row],
        out_specs=pl.BlockSpec((tm, d), lambda i: (i, 0)),
        out_shape=jax.ShapeDtypeStruct((t, d), out_dtype),
        compiler_params=_params(("parallel",)),
        name="rms_mod",
    )(x2, g, sc, sh)


def _mm_kernel(a_ref, b_ref, o_ref):
    o_ref[...] = jnp.dot(a_ref[...], b_ref[...], preferred_element_type=F32).astype(o_ref.dtype)


def _matmul(a, b, out_dtype, tm, tn, name):
    m, k = a.shape
    n = b.shape[1]
    return pl.pallas_call(
        _mm_kernel,
        grid=(m // tm, n // tn),
        in_specs=[pl.BlockSpec((tm, k), lambda i, j: (i, 0)),
                  pl.BlockSpec((k, tn), lambda i, j: (0, j))],
        out_specs=pl.BlockSpec((tm, tn), lambda i, j: (i, j)),
        out_shape=jax.ShapeDtypeStruct((m, n), out_dtype),
        compiler_params=_params(("parallel", "arbitrary")),
        name=name,
    )(a, b)


def _sortable(x):
    bits = pltpu.bitcast(x, I32)
    return bits ^ ((bits >> 31) & 0x7FFFFFFF)


def _unsortable(k):
    return pltpu.bitcast(k ^ ((k >> 31) & 0x7FFFFFFF), F32)


def _select_kernel(qi_ref, wi_ref, kit_ref, mask_ref, key_sc, qh_sc, wb_sc, *, n_sel):
    nkt, tq, tk = key_sc.shape
    nsub = tk // LANES
    ngrp = -(-n_sel // LANES)
    i = pl.program_id(0)
    q0 = i * tq
    nvis = (q0 + tq + tk - 1) // tk

    for h in range(H_IDX):
        qh_sc[h * tq:(h + 1) * tq, :] = qi_ref[:, h * D_IDX:(h + 1) * D_IDX]
        wb_sc[h] = jnp.broadcast_to(wi_ref[:, h:h + 1], (tq, LANES))

    row = lax.broadcasted_iota(I32, (tq, 1), 0) + q0
    limit = (row // CHUNK + 1) * CHUNK
    lane = lax.broadcasted_iota(I32, (tq, LANES), 1)

    def score_tile(kt, gmax):
        k_t = kit_ref[kt]
        accs = [jnp.zeros((tq, LANES), F32) for _ in range(nsub)]
        s_all = jnp.dot(qh_sc[...], k_t, preferred_element_type=F32)
        for h in range(H_IDX):
            s = s_all[h * tq:(h + 1) * tq, :]
            wb = wb_sc[h]
            for a in range(nsub):
                accs[a] = accs[a] + wb * jnp.maximum(s[:, a * LANES:(a + 1) * LANES], 0.0)
        gmax = list(gmax)
        for a in range(nsub):
            col = lane + (kt * tk + a * LANES)
            key = jnp.where(col < limit, _sortable(accs[a]), INT_MIN)
            key_sc[kt, :, a * LANES:(a + 1) * LANES] = key
            gmax[a % ngrp] = jnp.maximum(gmax[a % ngrp], key)
        return tuple(gmax)

    gmax = lax.fori_loop(0, nvis, score_tile,
                         tuple(jnp.full((tq, LANES), INT_MIN, I32) for _ in range(ngrp)))

    def count(pred):
        def body(kt, c):
            for a in range(nsub):
                kk = key_sc[kt, :, a * LANES:(a + 1) * LANES]
                c = c + pred(kk, lane + (kt * tk + a * LANES)).astype(I32)
            return c
        c = lax.fori_loop(0, nvis, body, jnp.zeros((tq, LANES), I32))
        return jnp.sum(c, axis=1, keepdims=True)

    gmin, ghi = gmax[0], gmax[0]
    for g in gmax[1:]:
        gmin = jnp.minimum(gmin, g)
        ghi = jnp.maximum(ghi, g)
    lo0 = jnp.maximum(jnp.min(gmin, axis=1, keepdims=True), INT_MIN + 1)
    hi0 = jnp.max(ghi, axis=1, keepdims=True)
    few = limit < n_sel
    lo0 = jnp.where(few, INT_MIN + 1, lo0)
    hi0 = jnp.where(few, INT_MIN + 1, hi0)
    unknown = jnp.full((tq, 1), 2 ** 30, I32)

    def bis_cond(st):
        lo, hi, _, _ = st
        return jnp.max((lo < hi).astype(I32)) > 0

    def bis_body(st):
        lo, hi, c_lo, c_hi1 = st
        active = lo < hi
        mid = (lo | hi) - ((lo ^ hi) >> 1)
        fin = lambda k: jnp.clip(k, -KEY_FMAX - 1, KEY_FMAX)
        vmid = _sortable(0.5 * _unsortable(fin(lo)) + 0.5 * _unsortable(fin(hi)))
        mid = jnp.where((vmid > lo) & (vmid <= hi), vmid, mid)
        c = count(lambda kk, col: kk >= mid)
        up = active & (c >= n_sel)
        dn = active & (c < n_sel)
        hit = active & (c == n_sel)
        lo = jnp.where(up, mid, lo)
        c_lo = jnp.where(up, c, c_lo)
        hi = jnp.where(dn, mid - 1, jnp.where(hit, mid, hi))
        c_hi1 = jnp.where(dn, c, c_hi1)
        return lo, hi, c_lo, c_hi1

    thr, _, n_ge, n_gt = lax.while_loop(bis_cond, bis_body,
                                        (lo0, hi0, unknown, jnp.zeros((tq, 1), I32)))

    excess = (n_ge > n_sel) & jnp.logical_not(few)
    need = n_sel - n_gt
    ncol = nkt * tk

    def tie_cut():
        nbits = max(1, int(math.ceil(math.log2(ncol))))

        def cut_body(b, p):
            cand = p | jnp.left_shift(jnp.int32(1), nbits - 1 - b)
            cnt = count(lambda kk, col: (kk == thr) & (col < cand))
            return jnp.where(cnt < need, cand, p)

        p = lax.fori_loop(0, nbits, cut_body, jnp.zeros((tq, 1), I32))
        return p + 1

    cut = lax.cond(jnp.max(excess.astype(I32)) > 0, tie_cut, lambda: jnp.full((tq, 1), ncol, I32))
    cut = jnp.where(excess, cut, ncol)

    def write_tile(kt, carry):
        kk = key_sc[kt]
        col = lax.broadcasted_iota(I32, (tq, tk), 1) + kt * tk
        sel = (kk > thr) | ((kk == thr) & (col < cut))
        mask_ref[kt] = jnp.where(sel, 0.0, NEG).astype(mask_ref.dtype)
        return carry

    lax.fori_loop(0, nvis, write_tile, 0)

    def fill_tile(kt, carry):
        mask_ref[kt] = jnp.full((tq, tk), NEG, mask_ref.dtype)
        return carry

    lax.fori_loop(nvis, nkt, fill_tile, 0)


def _select(proj, wi, kit3, n_sel, qi_col_block):
    s = proj.shape[0]
    nkt, _, tk = kit3.shape
    tq = SEL_TQ
    nqb = s // tq
    assert -(-n_sel // LANES) <= tk // LANES
    return pl.pallas_call(
        functools.partial(_select_kernel, n_sel=n_sel),
        grid=(nqb,),
        in_specs=[pl.BlockSpec((tq, W_IDX_Q), lambda i: (i, qi_col_block)),
                  pl.BlockSpec((tq, H_IDX), lambda i: (i, 0)),
                  pl.BlockSpec((nkt, D_IDX, tk), lambda i: (0, 0, 0))],
        out_specs=pl.BlockSpec((None, nkt, tq, tk), lambda i: (i, 0, 0, 0)),
        out_shape=jax.ShapeDtypeStruct((nqb, nkt, tq, tk), BF16),
        scratch_shapes=[pltpu.VMEM((nkt, tq, tk), I32),
                        pltpu.VMEM((H_IDX * tq, D_IDX), BF16),
                        pltpu.VMEM((H_IDX, tq, LANES), F32)],
        compiler_params=_params(("arbitrary",)),
        name="dsa_select",
    )(proj, wi, kit3)


def _dsa_attn_kernel(qb_ref, kb_ref, first_ref, last_ref, slab_ref,
                     q_ref, kt_ref, v_ref, mask_ref, band_ref, far_ref, o_ref,
                     m_sc, l_sc, acc_sc):
    step = pl.program_id(0)
    tk = kt_ref.shape[1]
    nslab = tk // LANES
    slab0 = slab_ref[step]

    @pl.when(first_ref[step] == 1)
    def _():
        m_sc[...] = jnp.full(m_sc.shape, NEG, F32)
        l_sc[...] = jnp.zeros(l_sc.shape, F32)
        acc_sc[...] = jnp.zeros(acc_sc.shape, F32)

    def tile(near):
        mask = jnp.concatenate([mask_ref[r] for r in range(mask_ref.shape[0])], axis=0).astype(F32)
        ones = jnp.ones((tk, HD_A), BF16)
        for h in range(H_A):
            hs = slice(h * HD_A, (h + 1) * HD_A)
            s = jnp.dot(q_ref[:, hs], kt_ref[hs, :], preferred_element_type=F32) + mask
            if near:
                s = s + jnp.concatenate([band_ref[h, slab0 + a] for a in range(nslab)], axis=1)
                shift = 0.0
            else:
                shift = far_ref[h]
            smax = s[:, 0:LANES]
            for a in range(1, nslab):
                smax = jnp.maximum(smax, s[:, a * LANES:(a + 1) * LANES])
            m_old = m_sc[h]
            m_new = jnp.maximum(m_old, jnp.max(smax, axis=1, keepdims=True) + shift)
            alpha = jnp.exp2(m_old - m_new)
            mm = m_new - shift
            p = jnp.concatenate([jnp.exp2(s[:, a * LANES:(a + 1) * LANES] - mm) for a in range(nslab)],
                                axis=1).astype(BF16)
            pv = jnp.dot(p, jnp.concatenate([v_ref[:, hs], ones], axis=1), preferred_element_type=F32)
            acc_sc[:, hs] = alpha * acc_sc[:, hs] + pv[:, 0:HD_A]
            l_sc[h] = alpha * l_sc[h] + pv[:, HD_A:2 * HD_A]
            m_sc[h] = m_new

    @pl.when(slab0 >= 0)
    def _():
        tile(True)

    @pl.when(slab0 < 0)
    def _():
        tile(False)

    @pl.when(last_ref[step] == 1)
    def _():
        for h in range(H_A):
            hs = slice(h * HD_A, (h + 1) * HD_A)
            o_ref[:, hs] = (acc_sc[:, hs] / l_sc[h]).astype(o_ref.dtype)


def _t5_bucket_np(rel):
    nb = N_T5_BUCKETS // 2
    ret = (rel > 0).astype(np.int32) * nb
    n = np.abs(rel)
    max_exact = nb // 2
    nf = np.maximum(n, 1).astype(np.float32)
    large = max_exact + (np.log(nf / np.float32(max_exact)) / np.float32(math.log(T5_MAX_DIST / max_exact))
                         * np.float32(nb - max_exact)).astype(np.int32)
    large = np.minimum(large, nb - 1)
    return ret + np.where(n < max_exact, n, large)


def _dsa_plan(s, tq, tk):
    nqb = s // tq
    d_all = np.arange(-(s - 1), CHUNK, dtype=np.int64)
    b_all = _t5_bucket_np(d_all)
    far_bucket = int(b_all[0])
    varying = np.nonzero(b_all != far_bucket)[0]
    d_lo = int(d_all[varying[0]])
    o_min = min(int(math.ceil((d_lo - (tk - 1)) / LANES)) * LANES, 0)
    n_slabs = (-o_min + tk) // LANES
    qb, kb, first, last, slab = [], [], [], [], []
    for i in range(nqb):
        q0 = i * tq
        nvis = (q0 + tq + tk - 1) // tk
        for j in range(nvis):
            o = j * tk - q0
            qb.append(i)
            kb.append(j)
            first.append(1 if j == 0 else 0)
            last.append(1 if j == nvis - 1 else 0)
            slab.append(-1 if o < o_min else (o - o_min) // LANES)
    u = np.arange(n_slabs * LANES + tq - 1)
    g_bucket = _t5_bucket_np(np.clip(o_min + u - (tq - 1), -(s - 1), None))
    plan = tuple(np.asarray(a, np.int32) for a in (qb, kb, first, last, slab))
    return plan, g_bucket, far_bucket, n_slabs


def _dsa_attn(proj, kt, mask4, t5_table, q_col, v_col):
    s = proj.shape[0]
    tq, tk = ATT_TQ, SEL_TK
    (qb, kb, first, last, slab), g_bucket, far_bucket, n_slabs = _dsa_plan(s, tq, tk)
    nsteps = len(qb)
    t5l = t5_table * LOG2E
    g = jnp.transpose(t5l[g_bucket], (1, 0))
    band = _toeplitz(g, tq, n_slabs * LANES)
    band = jnp.transpose(band.reshape(H_A, tq, n_slabs, LANES), (0, 2, 1, 3))
    far = t5l[far_bucket]
    rq = tq // SEL_TQ
    grid_spec = pltpu.PrefetchScalarGridSpec(
        num_scalar_prefetch=5,
        grid=(nsteps,),
        in_specs=[
            pl.BlockSpec((tq, W_A), lambda t, qb, kb, f, l, sl: (qb[t], q_col)),
            pl.BlockSpec((W_A, tk), lambda t, qb, kb, f, l, sl: (0, kb[t])),
            pl.BlockSpec((tk, W_A), lambda t, qb, kb, f, l, sl: (kb[t], v_col)),
            pl.BlockSpec((rq, None, SEL_TQ, tk), lambda t, qb, kb, f, l, sl: (qb[t], kb[t], 0, 0)),
            pl.BlockSpec((H_A, n_slabs, tq, LANES), lambda t, qb, kb, f, l, sl: (0, 0, 0, 0)),
            pl.BlockSpec(memory_space=pltpu.SMEM),
        ],
        out_specs=pl.BlockSpec((tq, W_A), lambda t, qb, kb, f, l, sl: (qb[t], 0)),
        scratch_shapes=[pltpu.VMEM((H_A, tq, LANES), F32),
                        pltpu.VMEM((H_A, tq, LANES), F32),
                        pltpu.VMEM((tq, W_A), F32)],
    )
    return pl.pallas_call(
        _dsa_attn_kernel,
        grid_spec=grid_spec,
        out_shape=jax.ShapeDtypeStruct((s, W_A), BF16),
        compiler_params=_params(("arbitrary",)),
        name="dsa_attn",
    )(jnp.asarray(qb), jnp.asarray(kb), jnp.asarray(first), jnp.asarray(last), jnp.asarray(slab),
      proj, kt, proj, mask4, band, far)


def _band_kernel(q_ref, k0_ref, k1_ref, k2_ref, v0_ref, v1_ref, v2_ref, bias_ref, o_ref):
    i = pl.program_id(0)
    tq = q_ref.shape[0]
    wk = 3 * tq
    col = lax.broadcasted_iota(I32, (tq, wk), 1)
    start_mask = jnp.where(col + (i - 2) * tq >= 0, 0.0, NEG)
    for h in range(H_B):
        hs = slice(h * HD_B, (h + 1) * HD_B)
        kc = jnp.concatenate([k0_ref[hs, :], k1_ref[hs, :], k2_ref[hs, :]], axis=1)
        vc = jnp.concatenate([v0_ref[:, hs], v1_ref[:, hs], v2_ref[:, hs]], axis=0)
        s = jnp.dot(q_ref[:, hs], kc, preferred_element_type=F32) + bias_ref[h] + start_mask
        m = jnp.max(s, axis=1, keepdims=True)
        p = jnp.exp2(s - m)
        l = jnp.sum(p, axis=1, keepdims=True)
        pv = jnp.dot(p.astype(BF16), vc, preferred_element_type=F32)
        o_ref[:, hs] = (pv / l).astype(o_ref.dtype)


def _band_bias(rel_table, tq):
    assert 2 * tq >= N_LEFT_CHUNKS * CHUNK and tq % CHUNK == 0
    wk = 3 * tq
    x = np.arange(wk + tq - 1)
    idx = np.clip(2 * tq + (tq - 1) - x, -REL_CLIP, REL_CLIP) + REL_CLIP
    bias = _toeplitz(rel_table[:, idx] * LOG2E, tq, wk)
    r = np.arange(tq)[:, None]
    c = np.arange(wk)[None, :]
    dchunk = (c - 2 * tq) // CHUNK - r // CHUNK
    in_band = (dchunk <= 0) & (dchunk >= -N_LEFT_CHUNKS)
    return jnp.where(jnp.asarray(in_band)[None], bias, NEG).astype(F32)


def _band_attn(proj, kt, rel_table, q_col, v_col):
    s = proj.shape[0]
    tq = BAND_TQ
    bias = _band_bias(rel_table, tq)

    def kspec(back):
        return pl.BlockSpec((W_B, tq), lambda i: (0, jnp.maximum(i - back, 0)))

    def vspec(back):
        return pl.BlockSpec((tq, W_B), lambda i: (jnp.maximum(i - back, 0), v_col))

    return pl.pallas_call(
        _band_kernel,
        grid=(s // tq,),
        in_specs=[pl.BlockSpec((tq, W_B), lambda i: (i, q_col)),
                  kspec(2), kspec(1), kspec(0), vspec(2), vspec(1), vspec(0),
                  pl.BlockSpec((H_B, tq, 3 * tq), lambda i: (0, 0, 0))],
        out_specs=pl.BlockSpec((tq, W_B), lambda i: (i, 0)),
        out_shape=jax.ShapeDtypeStruct((s, W_B), BF16),
        compiler_params=_params(("parallel",)),
        name="band_attn",
    )(proj, kt, kt, kt, proj, proj, proj, bias)


def _merge_kernel(ya_ref, yb_ref, ga_ref, gb_ref, wa_ref, wb_ref, o_ref):
    ua = jnp.dot(ya_ref[...], wa_ref[...], preferred_element_type=F32)
    ub = jnp.dot(yb_ref[...], wb_ref[...], preferred_element_type=F32)
    o_ref[...] = (jax.nn.sigmoid(ga_ref[...]) * ua + jax.nn.sigmoid(gb_ref[...]) * ub).astype(o_ref.dtype)


def _merge(ya, yb, gates, wa, wb, tm=256):
    s, d = ya.shape[0], wa.shape[1]
    return pl.pallas_call(
        _merge_kernel,
        grid=(s // tm,),
        in_specs=[pl.BlockSpec((tm, W_A), lambda i: (i, 0)),
                  pl.BlockSpec((tm, W_B), lambda i: (i, 0)),
                  pl.BlockSpec((tm, d), lambda i: (i, 0)),
                  pl.BlockSpec((tm, d), lambda i: (i, 1)),
                  pl.BlockSpec((W_A, d), lambda i: (0, 0)),
                  pl.BlockSpec((W_B, d), lambda i: (0, 0))],
        out_specs=pl.BlockSpec((tm, d), lambda i: (i, 0)),
        out_shape=jax.ShapeDtypeStruct((s, d), BF16),
        compiler_params=_params(("parallel",)),
        name="merge",
    )(ya, yb, gates, gates, wa, wb)


def _post_kernel(x_ref, m_ref, wo_ref, g1_ref, gn_ref, sc_ref, sh_ref, x1_ref, h2_ref):
    x1 = x_ref[...] + g1_ref[...] * jnp.dot(m_ref[...], wo_ref[...], preferred_element_type=F32)
    x1_ref[...] = x1
    y = x1 * lax.rsqrt(jnp.mean(x1 * x1, axis=-1, keepdims=True) + EPS)
    _store_rows(h2_ref, y * gn_ref[...] * (1.0 + sc_ref[...]) + sh_ref[...])


def _post(x2, merged, wo, g1, gn, sc, sh, tm=256):
    t, d = x2.shape
    row = pl.BlockSpec((1, d), lambda i: (0, 0))
    tile = pl.BlockSpec((tm, d), lambda i: (i, 0))
    return pl.pallas_call(
        _post_kernel,
        grid=(t // tm,),
        in_specs=[tile, tile, pl.BlockSpec((d, d), lambda i: (0, 0)), row, row, row, row],
        out_specs=[tile, pl.BlockSpec((tm * (d // LANES), LANES), lambda i: (i, 0))],
        out_shape=[jax.ShapeDtypeStruct((t, d), F32), jax.ShapeDtypeStruct((t * (d // LANES), LANES), F32)],
        compiler_params=_params(("parallel",)),
        name="post",
    )(x2, merged, wo, g1, gn, sc, sh)


def _router_kernel(h_ref, wr_ref, br_ref, lg_ref):
    lg_ref[...] = jnp.dot(_load_rows(h_ref, wr_ref.shape[0] // LANES), wr_ref[...], preferred_element_type=F32,
                          precision=lax.Precision.HIGHEST) + br_ref[...]


def _router(h2, w_router, b_router, tm=512):
    d, nr = w_router.shape
    t = h2.shape[0] // (d // LANES)
    return pl.pallas_call(
        _router_kernel,
        grid=(t // tm,),
        in_specs=[pl.BlockSpec((tm * (d // LANES), LANES), lambda i: (i, 0)),
                  pl.BlockSpec((d, nr), lambda i: (0, 0)), pl.BlockSpec((1, nr), lambda i: (0, 0))],
        out_specs=pl.BlockSpec((tm, nr), lambda i: (i, 0)),
        out_shape=jax.ShapeDtypeStruct((t, nr), F32),
        compiler_params=_params(("parallel",)),
        name="router",
    )(h2, w_router, b_router)


def _moe_kernel(tok_ref, dst_ref, be_ref, nu_ref,
                h_hbm, wrow_ref, w1_ref, w3_ref, w2_ref, z_hbm,
                x0, x1, y0, y1, w1b, w3b, w2b, gsem, ssem):
    b = pl.program_id(0)
    c = w1b.shape[0] // LANES
    bm = x0.shape[0] // c
    nu = nu_ref[0]
    xs, ys = (x0, x1), (y0, y1)

    def gather_copy(sl, r, t):
        return pltpu.make_async_copy(h_hbm.at[pl.ds(t * c, c), :], xs[sl].at[pl.ds(r * c, c), :], gsem.at[sl])

    def scatter_copy(sl, r, d):
        return pltpu.make_async_copy(ys[sl].at[pl.ds(r * c, c), :], z_hbm.at[pl.ds(d * c, c), :], ssem.at[sl])

    def gather_start(blk, sl):
        for r in range(bm):
            gather_copy(sl, r, tok_ref[blk * bm + r]).start(priority=1)

    def scatter_start(blk, sl):
        for r in range(bm):
            scatter_copy(sl, r, dst_ref[(blk + 2) * bm + r]).start(priority=r % 2)

    def gather_wait(sl):
        for r in range(bm):
            gather_copy(sl, r, 0).wait()

    def scatter_wait(sl):
        for r in range(bm):
            scatter_copy(sl, r, 0).wait()

    @pl.when(b == 0)
    def _():
        y0[...] = jnp.zeros(y0.shape, F32)
        y1[...] = jnp.zeros(y1.shape, F32)
        scatter_start(-2, 0)
        gather_start(0, 0)

    @pl.when((b < nu) & ((b == 0) | (be_ref[b] != be_ref[jnp.maximum(b - 1, 0)])))
    def _():
        w1b[...] = w1_ref[0].astype(BF16)
        w3b[...] = w3_ref[0].astype(BF16)
        w2b[...] = w2_ref[0].astype(BF16)

    def main(sl):
        gather_wait(sl)
        scatter_wait(sl)
        gather_start(b + 1, 1 - sl)
        scatter_start(b - 1, 1 - sl)
        x = _load_rows(xs[sl], c).astype(BF16)
        a1 = jnp.dot(x, w1b[...], preferred_element_type=F32)
        a3 = jnp.dot(x, w3b[...], preferred_element_type=F32)
        a = (a1 * jax.nn.sigmoid(a1)) * a3
        y = jnp.dot(a.astype(BF16), w2b[...], preferred_element_type=F32)
        _store_rows(ys[sl], y * wrow_ref[...])

    def drain(sl):
        gather_wait(sl)
        scatter_wait(sl)
        scatter_start(b - 1, 1 - sl)
        scatter_wait(1 - sl)

    for sl in range(2):
        @pl.when((b < nu) & (b % 2 == sl))
        def _():
            main(sl)

        @pl.when((b == nu) & (b % 2 == sl))
        def _():
            drain(sl)


def _moe(h2, tok_buf, dst_buf, blk_e, n_used, w_rows, w1, w3, w2):
    d, dff = w1.shape[1:]
    c = d // LANES
    t = h2.shape[0] // c
    bm = MOE_BM
    nsteps = blk_e.shape[0]
    assert tok_buf.shape[0] == nsteps * bm and dst_buf.shape[0] == (nsteps + 2) * bm
    grid_spec = pltpu.PrefetchScalarGridSpec(
        num_scalar_prefetch=4,
        grid=(nsteps,),
        in_specs=[
            pl.BlockSpec(memory_space=pl.ANY),
            pl.BlockSpec((bm, 1), lambda b, tok, dst, be, nu: (b, 0)),
            pl.BlockSpec((1, d, dff), lambda b, tok, dst, be, nu: (be[b], 0, 0)),
            pl.BlockSpec((1, d, dff), lambda b, tok, dst, be, nu: (be[b], 0, 0)),
            pl.BlockSpec((1, dff, d), lambda b, tok, dst, be, nu: (be[b], 0, 0)),
        ],
        out_specs=pl.BlockSpec(memory_space=pl.ANY),
        scratch_shapes=[pltpu.VMEM((bm * c, LANES), F32), pltpu.VMEM((bm * c, LANES), F32),
                        pltpu.VMEM((bm * c, LANES), F32), pltpu.VMEM((bm * c, LANES), F32),
                        pltpu.VMEM((d, dff), BF16), pltpu.VMEM((d, dff), BF16), pltpu.VMEM((dff, d), BF16),
                        pltpu.SemaphoreType.DMA((2,)), pltpu.SemaphoreType.DMA((2,))],
    )
    return pl.pallas_call(
        _moe_kernel,
        grid_spec=grid_spec,
        out_shape=jax.ShapeDtypeStruct(((TOPK_IN_GROUP * t + 2 * bm) * c, LANES), F32),
        compiler_params=pltpu.CompilerParams(dimension_semantics=("arbitrary",),
                                             vmem_limit_bytes=VMEM_LIMIT, has_side_effects=True),
        name="moe",
    )(tok_buf, dst_buf, blk_e, n_used, h2, w_rows, w1, w3, w2)


def _final_kernel(x1_ref, z0_ref, z1_ref, g2_ref, gn_ref, o_ref):
    c = x1_ref.shape[1] // LANES
    x2 = x1_ref[...] + g2_ref[...] * (_load_rows(z0_ref, c) + _load_rows(z1_ref, c))
    y = x2 * lax.rsqrt(jnp.mean(x2 * x2, axis=-1, keepdims=True) + EPS)
    o_ref[...] = y * gn_ref[...]


def _final(x1, z, g2, gn, tm=256):
    t, d = x1.shape
    row = pl.BlockSpec((1, d), lambda i: (0, 0))
    return pl.pallas_call(
        _final_kernel,
        grid=(t // tm,),
        in_specs=[pl.BlockSpec((tm, d), lambda i: (i, 0)),
                  pl.BlockSpec((tm * (d // LANES), LANES), lambda i: (i, 0)),
                  pl.BlockSpec((tm * (d // LANES), LANES), lambda i: (t // tm + i, 0)), row, row],
        out_specs=pl.BlockSpec((tm, d), lambda i: (i, 0)),
        out_shape=jax.ShapeDtypeStruct((t, d), F32),
        compiler_params=_params(("parallel",)),
        name="final",
    )(x1, z, z, g2, gn)


def _route(logits, t):
    gl = logits[:, :N_GROUPS]
    el = logits[:, N_GROUPS:N_GROUPS + N_EXPERTS].reshape(t, N_GROUPS, EXP_PER_GROUP)
    g_prob = jax.nn.softmax(gl, axis=-1)
    grp = jnp.argmax(gl, axis=-1).astype(I32)
    p_grp = jnp.take_along_axis(g_prob, grp[:, None], axis=-1)[:, 0]
    e_in = jnp.take_along_axis(el, grp[:, None, None], axis=1)[:, 0]
    top_v, top_i = lax.top_k(e_in, TOPK_IN_GROUP)
    p_in = jax.nn.softmax(top_v, axis=-1)
    expert = grp[:, None] * EXP_PER_GROUP + top_i.astype(I32)
    weight = p_grp[:, None] * p_in

    bm = MOE_BM
    m = t * TOPK_IN_GROUP
    e_flat = expert.reshape(m)
    w_flat = weight.reshape(m)
    order = jnp.argsort(e_flat, stable=True).astype(I32)
    counts = jnp.bincount(e_flat, length=N_EXPERTS).astype(I32)
    start = jnp.cumsum(counts) - counts
    padded = ((counts + bm - 1) // bm) * bm
    pend = jnp.cumsum(padded)
    pstart = pend - padded
    nb = m // bm + N_EXPERTS + 1
    blk_e = jnp.minimum(jnp.searchsorted(pend, jnp.arange(nb, dtype=I32) * bm, side='right'),
                        N_EXPERTS - 1).astype(I32)
    n_used = (pend[-1] // bm).astype(I32).reshape(1)
    pos = jnp.arange(nb * bm, dtype=I32).reshape(nb, bm)
    src = (start[blk_e] - pstart[blk_e])[:, None] + pos
    valid = (src < (start + counts)[blk_e][:, None]) & (pos < pend[-1])
    a = order[jnp.clip(src, 0, m - 1)]
    tok = a // TOPK_IN_GROUP
    tok_buf = jnp.where(valid, tok, 0).reshape(-1)
    spare = m + pos % (2 * bm)
    dst_buf = jnp.where(valid, (a % TOPK_IN_GROUP) * t + tok, spare).reshape(-1)
    dst_buf = jnp.concatenate([m + jnp.arange(2 * bm, dtype=I32), dst_buf])
    w_rows = jnp.where(valid, w_flat[a], 0.0).reshape(-1, 1)
    return tok_buf, dst_buf, blk_e, n_used, w_rows


def kernel(x, c, w_ada, b_ada, norm_mix, w_in, t5_table, rel_table, w_up_a, w_up_b, w_o, norm_ffn,
           w_rg, b_rg, w_re, b_re, w1, w3, w2, norm_final):
    bn, s, d = x.shape
    assert bn == 1 and w_ada.shape[0] == 1
    assert s % 1024 == 0
    t = bn * s
    x2 = x.reshape(t, d)
    n_sel = min(TOPK_MAX, s // 4)

    mod = _ada(c.reshape(d, 1), w_ada[0], b_ada[0].reshape(1, 6 * d))
    sh1, sc1, g1, sh2, sc2, g2 = [mod[:, i * d:(i + 1) * d] for i in range(6)]

    h = _rms_mod(x2, norm_mix[0].reshape(1, d), sc1, sh1, BF16)

    cols = np.cumsum([0, W_A, W_A, W_A, W_IDX_Q, D_IDX, H_IDX, W_B, W_B, W_B, d, d])
    wsl = [w_in[0][:, cols[i]:cols[i + 1]] for i in range(11)]
    wqa, wka, wva, wqi, wki, wwi, wqb, wkb, wvb, wga, wgb = wsl
    w_main = jnp.concatenate([wqa * (LOG2E / math.sqrt(HD_A)), wka, wva, wqi,
                              wqb * (LOG2E / math.sqrt(HD_B)), wkb, wvb], axis=1).astype(BF16)
    w_gate = jnp.concatenate([wga, wgb], axis=1).astype(BF16)
    w_idx = jnp.concatenate([wki, wwi, jnp.zeros((d, LANES - D_IDX - H_IDX), F32)], axis=1).astype(BF16)

    proj = _matmul(h, w_main, BF16, 1024, 1024, "proj_main")
    gates = _matmul(h, w_gate, F32, 1024, 1024, "proj_gate")
    idx = _matmul(h, w_idx, F32, 1024, LANES, "proj_idx")
    nkt = s // SEL_TK
    kit3 = jnp.transpose(idx[:, :D_IDX].astype(BF16).reshape(nkt, SEL_TK, D_IDX), (0, 2, 1))
    wi = idx[:, D_IDX:D_IDX + H_IDX] * ((H_IDX ** -0.5) * (D_IDX ** -0.5))
    kt_a = jnp.transpose(proj[:, W_A:2 * W_A])
    kt_b = jnp.transpose(proj[:, 5 * W_A:6 * W_A])

    mask4 = _select(proj, wi, kit3, n_sel, qi_col_block=3)
    y_a = _dsa_attn(proj, kt_a, mask4, t5_table, q_col=0, v_col=2)
    y_b = _band_attn(proj, kt_b, rel_table[0], q_col=4, v_col=6)

    merged = _merge(y_a, y_b, gates, w_up_a[0].astype(BF16), w_up_b[0].astype(BF16))

    nr = LANES
    w_router = jnp.concatenate([w_rg[0], w_re[0], jnp.zeros((d, nr - N_GROUPS - N_EXPERTS), F32)], axis=1)
    b_router = jnp.concatenate([b_rg[0], b_re[0], jnp.zeros((nr - N_GROUPS - N_EXPERTS,), F32)]).reshape(1, nr)
    x1, h2 = _post(x2, merged, w_o[0].astype(BF16), g1, norm_ffn[0].reshape(1, d), sc2, sh2)
    logits = _router(h2, w_router, b_router)
    tok_buf, dst_buf, blk_e, n_used, w_rows = _route(logits, t)
    z = _moe(h2, tok_buf, dst_buf, blk_e, n_used, w_rows, w1[0], w3[0], w2[0])
    out = _final(x1, z, g2, norm_final.reshape(1, d))
    return out.reshape(bn, s, d)
```

```python
import functools
import math

import numpy as np
import jax
import jax.numpy as jnp
from jax import lax
from jax.experimental import pallas as pl
from jax.experimental.pallas import tpu as pltpu

F32 = jnp.float32
BF16 = jnp.bfloat16
I32 = jnp.int32

CHUNK = 64
EPS = 1e-6
H_A, HD_A = 8, 128
H_IDX, D_IDX = 16, 64
TOPK_MAX = 256
N_T5_BUCKETS = 32
T5_MAX_DIST = 1024
H_B, HD_B = 8, 128
N_LEFT_CHUNKS = 8
REL_CLIP = 128
N_GROUPS = 8
EXP_PER_GROUP = 8
N_EXPERTS = N_GROUPS * EXP_PER_GROUP
TOPK_IN_GROUP = 2

W_A = H_A * HD_A
W_B = H_B * HD_B
W_IDX_Q = H_IDX * D_IDX

NEG = -1e30
INT_MIN = -(2 ** 31)
KEY_FMAX = 0x7F7FFFFF
LOG2E = math.log2(math.e)

LANES = 128
VMEM_LIMIT = 56 * 1024 * 1024

SEL_TQ = 128
SEL_TK = 512
ATT_TQ = 256
BAND_TQ = 256
MOE_BM = 256


def _params(sem, vmem=VMEM_LIMIT):
    return pltpu.CompilerParams(dimension_semantics=sem, vmem_limit_bytes=vmem)


def _load_rows(ref, c):
    rows = ref.shape[0] // c
    return jnp.concatenate([ref[pl.ds(k, rows, stride=c), :] for k in range(c)], axis=1)


def _store_rows(ref, val):
    rows = val.shape[0]
    c = ref.shape[0] // rows
    for k in range(c):
        ref[pl.ds(k, rows, stride=c), :] = val[:, k * LANES:(k + 1) * LANES]


def _toeplitz(g, nrows, ncols):
    n = ncols + nrows - 1
    assert g.shape[-1] == n
    u = jnp.concatenate([g, jnp.zeros(g.shape[:-1] + (1,), g.dtype)], axis=-1)
    flat = jnp.tile(u, (1,) * (g.ndim - 1) + (nrows,))[..., :nrows * n]
    return flat.reshape(g.shape[:-1] + (nrows, n))[..., nrows - 1:]


def _ada_kernel(c_ref, w_ref, b_ref, o_ref, *, kc):
    d = w_ref.shape[0]
    tn = w_ref.shape[1]

    def body(k, acc):
        r0 = pl.multiple_of(k * kc, kc)
        cc = c_ref[pl.ds(r0, kc), :]
        ca = cc * jax.nn.sigmoid(cc)
        return acc + jnp.sum(w_ref[pl.ds(r0, kc), :] * ca, axis=0, keepdims=True)

    acc = lax.fori_loop(0, d // kc, body, jnp.zeros((1, tn), F32))
    o_ref[...] = acc + b_ref[...]


def _ada(c_col, w, b_row, tn=1024, kc=256):
    d, n = w.shape
    return pl.pallas_call(
        functools.partial(_ada_kernel, kc=kc),
        grid=(n // tn,),
        in_specs=[pl.BlockSpec((d, 1), lambda j: (0, 0)),
                  pl.BlockSpec((d, tn), lambda j: (0, j)),
                  pl.BlockSpec((1, tn), lambda j: (0, j))],
        out_specs=pl.BlockSpec((1, tn), lambda j: (0, j)),
        out_shape=jax.ShapeDtypeStruct((1, n), F32),
        compiler_params=_params(("arbitrary",)),
        name="ada",
    )(c_col, w, b_row)


def _rms_mod_kernel(x_ref, g_ref, sc_ref, sh_ref, o_ref):
    x = x_ref[...]
    y = x * lax.rsqrt(jnp.mean(x * x, axis=-1, keepdims=True) + EPS)
    o_ref[...] = (y * g_ref[...] * (1.0 + sc_ref[...]) + sh_ref[...]).astype(o_ref.dtype)


def _rms_mod(x2, g, sc, sh, out_dtype, tm=512):
    t, d = x2.shape
    row = pl.BlockSpec((1, d), lambda i: (0, 0))
    return pl.pallas_call(
        _rms_mod_kernel,
        grid=(t // tm,),
        in_specs=[pl.BlockSpec((tm, d), lambda i: (i, 0)), row, row, row],
        out_specs=pl.BlockSpec((tm, d), lambda i: (i, 0)),
        out_shape=jax.ShapeDtypeStruct((t, d), out_dtype),
        compiler_params=_params(("parallel",)),
        name="rms_mod",
    )(x2, g, sc, sh)


def _mm_kernel(a_ref, b_ref, o_ref):
    o_ref[...] = jnp.dot(a_ref[...], b_ref[...], preferred_element_type=F32).astype(o_ref.dtype)


def _matmul(a, b, out_dtype, tm, tn, name):
    m, k = a.shape
    n = b.shape[1]
    return pl.pallas_call(
        _mm_kernel,
        grid=(m // tm, n // tn),
        in_specs=[pl.BlockSpec((tm, k), lambda i, j: (i, 0)),
                  pl.BlockSpec((k, tn), lambda i, j: (0, j))],
        out_specs=pl.BlockSpec((tm, tn), lambda i, j: (i, j)),
        out_shape=jax.ShapeDtypeStruct((m, n), out_dtype),
        compiler_params=_params(("parallel", "arbitrary")),
        name=name,
    )(a, b)


def _sortable(x):
    bits = pltpu.bitcast(x, I32)
    return bits ^ ((bits >> 31) & 0x7FFFFFFF)


def _unsortable(k):
    return pltpu.bitcast(k ^ ((k >> 31) & 0x7FFFFFFF), F32)


def _select_kernel(qi_ref, wi_ref, kit_ref, mask_ref, key_sc, qh_sc, wb_sc, *, n_sel):
    nkt, tq, tk = key_sc.shape
    nsub = tk // LANES
    ngrp = -(-n_sel // LANES)
    i = pl.program_id(0)
    q0 = i * tq
    nvis = (q0 + tq + tk - 1) // tk

    for h in range(H_IDX):
        qh_sc[h * tq:(h + 1) * tq, :] = qi_ref[:, h * D_IDX:(h + 1) * D_IDX]
        wb_sc[h] = jnp.broadcast_to(wi_ref[:, h:h + 1], (tq, LANES))

    row = lax.broadcasted_iota(I32, (tq, 1), 0) + q0
    limit = (row // CHUNK + 1) * CHUNK
    lane = lax.broadcasted_iota(I32, (tq, LANES), 1)

    def score_tile(kt, gmax):
        k_t = kit_ref[kt]
        accs = [jnp.zeros((tq, LANES), F32) for _ in range(nsub)]
        s_all = jnp.dot(qh_sc[...], k_t, preferred_element_type=F32)
        for h in range(H_IDX):
            s = s_all[h * tq:(h + 1) * tq, :]
            wb = wb_sc[h]
            for a in range(nsub):
                accs[a] = accs[a] + wb * jnp.maximum(s[:, a * LANES:(a + 1) * LANES], 0.0)
        gmax = list(gmax)
        for a in range(nsub):
            col = lane + (kt * tk + a * LANES)
            key = jnp.where(col < limit, _sortable(accs[a]), INT_MIN)
            key_sc[kt, :, a * LANES:(a + 1) * LANES] = key
            gmax[a % ngrp] = jnp.maximum(gmax[a % ngrp], key)
        return tuple(gmax)

    gmax = lax.fori_loop(0, nvis, score_tile,
                         tuple(jnp.full((tq, LANES), INT_MIN, I32) for _ in range(ngrp)))

    def count(pred):
        def body(kt, c):
            for a in range(nsub):
                kk = key_sc[kt, :, a * LANES:(a + 1) * LANES]
                c = c + pred(kk, lane + (kt * tk + a * LANES)).astype(I32)
            return c
        c = lax.fori_loop(0, nvis, body, jnp.zeros((tq, LANES), I32))
        return jnp.sum(c, axis=1, keepdims=True)

    gmin, ghi = gmax[0], gmax[0]
    for g in gmax[1:]:
        gmin = jnp.minimum(gmin, g)
        ghi = jnp.maximum(ghi, g)
    lo0 = jnp.maximum(jnp.min(gmin, axis=1, keepdims=True), INT_MIN + 1)
    hi0 = jnp.max(ghi, axis=1, keepdims=True)
    few = limit < n_sel
    lo0 = jnp.where(few, INT_MIN + 1, lo0)
    hi0 = jnp.where(few, INT_MIN + 1, hi0)
    unknown = jnp.full((tq, 1), 2 ** 30, I32)

    def bis_cond(st):
        lo, hi, _, _ = st
        return jnp.max((lo < hi).astype(I32)) > 0

    def bis_body(st):
        lo, hi, c_lo, c_hi1 = st
        active = lo < hi
        mid = (lo | hi) - ((lo ^ hi) >> 1)
        fin = lambda k: jnp.clip(k, -KEY_FMAX - 1, KEY_FMAX)
        vmid = _sortable(0.5 * _unsortable(fin(lo)) + 0.5 * _unsortable(fin(hi)))
        mid = jnp.where((vmid > lo) & (vmid <= hi), vmid, mid)
        c = count(lambda kk, col: kk >= mid)
        up = active & (c >= n_sel)
        dn = active & (c < n_sel)
        hit = active & (c == n_sel)
        lo = jnp.where(up, mid, lo)
        c_lo = jnp.where(up, c, c_lo)
        hi = jnp.where(dn, mid - 1, jnp.where(hit, mid, hi))
        c_hi1 = jnp.where(dn, c, c_hi1)
        return lo, hi, c_lo, c_hi1

    thr, _, n_ge, n_gt = lax.while_loop(bis_cond, bis_body,
                                        (lo0, hi0, unknown, jnp.zeros((tq, 1), I32)))

    excess = (n_ge > n_sel) & jnp.logical_not(few)
    need = n_sel - n_gt
    ncol = nkt * tk

    def tie_cut():
        nbits = max(1, int(math.ceil(math.log2(ncol))))

        def cut_body(b, p):
            cand = p | jnp.left_shift(jnp.int32(1), nbits - 1 - b)
            cnt = count(lambda kk, col: (kk == thr) & (col < cand))
            return jnp.where(cnt < need, cand, p)

        p = lax.fori_loop(0, nbits, cut_body, jnp.zeros((tq, 1), I32))
        return p + 1

    cut = lax.cond(jnp.max(excess.astype(I32)) > 0, tie_cut, lambda: jnp.full((tq, 1), ncol, I32))
    cut = jnp.where(excess, cut, ncol)

    def write_tile(kt, carry):
        kk = key_sc[kt]
        col = lax.broadcasted_iota(I32, (tq, tk), 1) + kt * tk
        sel = (kk > thr) | ((kk == thr) & (col < cut))
        mask_ref[kt] = jnp.where(sel, 0.0, NEG).astype(mask_ref.dtype)
        return carry

    lax.fori_loop(0, nvis, write_tile, 0)

    def fill_tile(kt, carry):
        mask_ref[kt] = jnp.full((tq, tk), NEG, mask_ref.dtype)
        return carry

    lax.fori_loop(nvis, nkt, fill_tile, 0)


def _select(proj, wi, kit3, n_sel, qi_col_block):
    s = proj.shape[0]
    nkt, _, tk = kit3.shape
    tq = SEL_TQ
    nqb = s // tq
    assert -(-n_sel // LANES) <= tk // LANES
    return pl.pallas_call(
        functools.partial(_select_kernel, n_sel=n_sel),
        grid=(nqb,),
        in_specs=[pl.BlockSpec((tq, W_IDX_Q), lambda i: (i, qi_col_block)),
                  pl.BlockSpec((tq, H_IDX), lambda i: (i, 0)),
                  pl.BlockSpec((nkt, D_IDX, tk), lambda i: (0, 0, 0))],
        out_specs=pl.BlockSpec((None, nkt, tq, tk), lambda i: (i, 0, 0, 0)),
        out_shape=jax.ShapeDtypeStruct((nqb, nkt, tq, tk), BF16),
        scratch_shapes=[pltpu.VMEM((nkt, tq, tk), I32),
                        pltpu.VMEM((H_IDX * tq, D_IDX), BF16),
                        pltpu.VMEM((H_IDX, tq, LANES), F32)],
        compiler_params=_params(("arbitrary",)),
        name="dsa_select",
    )(proj, wi, kit3)


def _dsa_attn_kernel(qb_ref, kb_ref, first_ref, last_ref, slab_ref,
                     q_ref, kt_ref, v_ref, mask_ref, band_ref, far_ref, o_ref,
                     m_sc, l_sc, acc_sc):
    step = pl.program_id(0)
    tk = kt_ref.shape[1]
    nslab = tk // LANES
    slab0 = slab_ref[step]

    @pl.when(first_ref[step] == 1)
    def _():
        m_sc[...] = jnp.full(m_sc.shape, NEG, F32)
        l_sc[...] = jnp.zeros(l_sc.shape, F32)
        acc_sc[...] = jnp.zeros(acc_sc.shape, F32)

    def tile(near):
        mask = jnp.concatenate([mask_ref[r] for r in range(mask_ref.shape[0])], axis=0).astype(F32)
        ones = jnp.ones((tk, HD_A), BF16)
        for h in range(H_A):
            hs = slice(h * HD_A, (h + 1) * HD_A)
            s = jnp.dot(q_ref[:, hs], kt_ref[hs, :], preferred_element_type=F32) + mask
            if near:
                s = s + jnp.concatenate([band_ref[h, slab0 + a] for a in range(nslab)], axis=1)
                shift = 0.0
            else:
                shift = far_ref[h]
            smax = s[:, 0:LANES]
            for a in range(1, nslab):
                smax = jnp.maximum(smax, s[:, a * LANES:(a + 1) * LANES])
            m_old = m_sc[h]
            m_new = jnp.maximum(m_old, jnp.max(smax, axis=1, keepdims=True) + shift)
            alpha = jnp.exp2(m_old - m_new)
            mm = m_new - shift
            p = jnp.concatenate([jnp.exp2(s[:, a * LANES:(a + 1) * LANES] - mm) for a in range(nslab)],
                                axis=1).astype(BF16)
            pv = jnp.dot(p, jnp.concatenate([v_ref[:, hs], ones], axis=1), preferred_element_type=F32)
            acc_sc[:, hs] = alpha * acc_sc[:, hs] + pv[:, 0:HD_A]
            l_sc[h] = alpha * l_sc[h] + pv[:, HD_A:2 * HD_A]
            m_sc[h] = m_new

    @pl.when(slab0 >= 0)
    def _():
        tile(True)

    @pl.when(slab0 < 0)
    def _():
        tile(False)

    @pl.when(last_ref[step] == 1)
    def _():
        for h in range(H_A):
            hs = slice(h * HD_A, (h + 1) * HD_A)
            o_ref[:, hs] = (acc_sc[:, hs] / l_sc[h]).astype(o_ref.dtype)


def _t5_bucket_np(rel):
    nb = N_T5_BUCKETS // 2
    ret = (rel > 0).astype(np.int32) * nb
    n = np.abs(rel)
    max_exact = nb // 2
    nf = np.maximum(n, 1).astype(np.float32)
    large = max_exact + (np.log(nf / np.float32(max_exact)) / np.float32(math.log(T5_MAX_DIST / max_exact))
                         * np.float32(nb - max_exact)).astype(np.int32)
    large = np.minimum(large, nb - 1)
    return ret + np.where(n < max_exact, n, large)


def _dsa_plan(s, tq, tk):
    nqb = s // tq
    d_all = np.arange(-(s - 1), CHUNK, dtype=np.int64)
    b_all = _t5_bucket_np(d_all)
    far_bucket = int(b_all[0])
    varying = np.nonzero(b_all != far_bucket)[0]
    d_lo = int(d_all[varying[0]])
    o_min = min(int(math.ceil((d_lo - (tk - 1)) / LANES)) * LANES, 0)
    n_slabs = (-o_min + tk) // LANES
    qb, kb, first, last, slab = [], [], [], [], []
    for i in range(nqb):
        q0 = i * tq
        nvis = (q0 + tq + tk - 1) // tk
        for j in range(nvis):
            o = j * tk - q0
            qb.append(i)
            kb.append(j)
            first.append(1 if j == 0 else 0)
            last.append(1 if j == nvis - 1 else 0)
            slab.append(-1 if o < o_min else (o - o_min) // LANES)
    u = np.arange(n_slabs * LANES + tq - 1)
    g_bucket = _t5_bucket_np(np.clip(o_min + u - (tq - 1), -(s - 1), None))
    plan = tuple(np.asarray(a, np.int32) for a in (qb, kb, first, last, slab))
    return plan, g_bucket, far_bucket, n_slabs


def _dsa_attn(proj, kt, mask4, t5_table, q_col, v_col):
    s = proj.shape[0]
    tq, tk = ATT_TQ, SEL_TK
    (qb, kb, first, last, slab), g_bucket, far_bucket, n_slabs = _dsa_plan(s, tq, tk)
    nsteps = len(qb)
    t5l = t5_table * LOG2E
    g = jnp.transpose(t5l[g_bucket], (1, 0))
    band = _toeplitz(g, tq, n_slabs * LANES)
    band = jnp.transpose(band.reshape(H_A, tq, n_slabs, LANES), (0, 2, 1, 3))
    far = t5l[far_bucket]
    rq = tq // SEL_TQ
    grid_spec = pltpu.PrefetchScalarGridSpec(
        num_scalar_prefetch=5,
        grid=(nsteps,),
        in_specs=[
            pl.BlockSpec((tq, W_A), lambda t, qb, kb, f, l, sl: (qb[t], q_col)),
            pl.BlockSpec((W_A, tk), lambda t, qb, kb, f, l, sl: (0, kb[t])),
            pl.BlockSpec((tk, W_A), lambda t, qb, kb, f, l, sl: (kb[t], v_col)),
            pl.BlockSpec((rq, None, SEL_TQ, tk), lambda t, qb, kb, f, l, sl: (qb[t], kb[t], 0, 0)),
            pl.BlockSpec((H_A, n_slabs, tq, LANES), lambda t, qb, kb, f, l, sl: (0, 0, 0, 0)),
            pl.BlockSpec(memory_space=pltpu.SMEM),
        ],
        out_specs=pl.BlockSpec((tq, W_A), lambda t, qb, kb, f, l, sl: (qb[t], 0)),
        scratch_shapes=[pltpu.VMEM((H_A, tq, LANES), F32),
                        pltpu.VMEM((H_A, tq, LANES), F32),
                        pltpu.VMEM((tq, W_A), F32)],
    )
    return pl.pallas_call(
        _dsa_attn_kernel,
        grid_spec=grid_spec,
        out_shape=jax.ShapeDtypeStruct((s, W_A), BF16),
        compiler_params=_params(("arbitrary",)),
        name="dsa_attn",
    )(jnp.asarray(qb), jnp.asarray(kb), jnp.asarray(first), jnp.asarray(last), jnp.asarray(slab),
      proj, kt, proj, mask4, band, far)


def _band_kernel(q_ref, k0_ref, k1_ref, k2_ref, v0_ref, v1_ref, v2_ref, bias_ref, o_ref):
    i = pl.program_id(0)
    tq = q_ref.shape[0]
    wk = 3 * tq
    col = lax.broadcasted_iota(I32, (tq, wk), 1)
    start_mask = jnp.where(col + (i - 2) * tq >= 0, 0.0, NEG)
    for h in range(H_B):
        hs = slice(h * HD_B, (h + 1) * HD_B)
        kc = jnp.concatenate([k0_ref[hs, :], k1_ref[hs, :], k2_ref[hs, :]], axis=1)
        vc = jnp.concatenate([v0_ref[:, hs], v1_ref[:, hs], v2_ref[:, hs]], axis=0)
        s = jnp.dot(q_ref[:, hs], kc, preferred_element_type=F32) + bias_ref[h] + start_mask
        m = jnp.max(s, axis=1, keepdims=True)
        p = jnp.exp2(s - m)
        l = jnp.sum(p, axis=1, keepdims=True)
        pv = jnp.dot(p.astype(BF16), vc, preferred_element_type=F32)
        o_ref[:, hs] = (pv / l).astype(o_ref.dtype)


def _band_bias(rel_table, tq):
    assert 2 * tq >= N_LEFT_CHUNKS * CHUNK and tq % CHUNK == 0
    wk = 3 * tq
    x = np.arange(wk + tq - 1)
    idx = np.clip(2 * tq + (tq - 1) - x, -REL_CLIP, REL_CLIP) + REL_CLIP
    bias = _toeplitz(rel_table[:, idx] * LOG2E, tq, wk)
    r = np.arange(tq)[:, None]
    c = np.arange(wk)[None, :]
    dchunk = (c - 2 * tq) // CHUNK - r // CHUNK
    in_band = (dchunk <= 0) & (dchunk >= -N_LEFT_CHUNKS)
    return jnp.where(jnp.asarray(in_band)[None], bias, NEG).astype(F32)


def _band_attn(proj, kt, rel_table, q_col, v_col):
    s = proj.shape[0]
    tq = BAND_TQ
    bias = _band_bias(rel_table, tq)

    def kspec(back):
        return pl.BlockSpec((W_B, tq), lambda i: (0, jnp.maximum(i - back, 0)))

    def vspec(back):
        return pl.BlockSpec((tq, W_B), lambda i: (jnp.maximum(i - back, 0), v_col))

    return pl.pallas_call(
        _band_kernel,
        grid=(s // tq,),
        in_specs=[pl.BlockSpec((tq, W_B), lambda i: (i, q_col)),
                  kspec(2), kspec(1), kspec(0), vspec(2), vspec(1), vspec(0),
                  pl.BlockSpec((H_B, tq, 3 * tq), lambda i: (0, 0, 0))],
        out_specs=pl.BlockSpec((tq, W_B), lambda i: (i, 0)),
        out_shape=jax.ShapeDtypeStruct((s, W_B), BF16),
        compiler_params=_params(("parallel",)),
        name="band_attn",
    )(proj, kt, kt, kt, proj, proj, proj, bias)


def _merge_kernel(ya_ref, yb_ref, ga_ref, gb_ref, wa_ref, wb_ref, o_ref):
    ua = jnp.dot(ya_ref[...], wa_ref[...], preferred_element_type=F32)
    ub = jnp.dot(yb_ref[...], wb_ref[...], preferred_element_type=F32)
    o_ref[...] = (jax.nn.sigmoid(ga_ref[...]) * ua + jax.nn.sigmoid(gb_ref[...]) * ub).astype(o_ref.dtype)


def _merge(ya, yb, gates, wa, wb, tm=256):
    s, d = ya.shape[0], wa.shape[1]
    return pl.pallas_call(
        _merge_kernel,
        grid=(s // tm,),
        in_specs=[pl.BlockSpec((tm, W_A), lambda i: (i, 0)),
                  pl.BlockSpec((tm, W_B), lambda i: (i, 0)),
                  pl.BlockSpec((tm, d), lambda i: (i, 0)),
                  pl.BlockSpec((tm, d), lambda i: (i, 1)),
                  pl.BlockSpec((W_A, d), lambda i: (0, 0)),
                  pl.BlockSpec((W_B, d), lambda i: (0, 0))],
        out_specs=pl.BlockSpec((tm, d), lambda i: (i, 0)),
        out_shape=jax.ShapeDtypeStruct((s, d), BF16),
        compiler_params=_params(("parallel",)),
        name="merge",
    )(ya, yb, gates, gates, wa, wb)


def _post_kernel(x_ref, m_ref, wo_ref, g1_ref, gn_ref, sc_ref, sh_ref, x1_ref, h2_ref):
    x1 = x_ref[...] + g1_ref[...] * jnp.dot(m_ref[...], wo_ref[...], preferred_element_type=F32)
    x1_ref[...] = x1
    y = x1 * lax.rsqrt(jnp.mean(x1 * x1, axis=-1, keepdims=True) + EPS)
    _store_rows(h2_ref, y * gn_ref[...] * (1.0 + sc_ref[...]) + sh_ref[...])


def _post(x2, merged, wo, g1, gn, sc, sh, tm=256):
    t, d = x2.shape
    row = pl.BlockSpec((1, d), lambda i: (0, 0))
    tile = pl.BlockSpec((tm, d), lambda i: (i, 0))
    return pl.pallas_call(
        _post_kernel,
        grid=(t // tm,),
        in_specs=[tile, tile, pl.BlockSpec((d, d), lambda i: (0, 0)), row, row, row, row],
        out_specs=[tile, pl.BlockSpec((tm * (d // LANES), LANES), lambda i: (i, 0))],
        out_shape=[jax.ShapeDtypeStruct((t, d), F32), jax.ShapeDtypeStruct((t * (d // LANES), LANES), F32)],
        compiler_params=_params(("parallel",)),
        name="post",
    )(x2, merged, wo, g1, gn, sc, sh)


def _router_kernel(h_ref, wr_ref, br_ref, lg_ref):
    lg_ref[...] = jnp.dot(_load_rows(h_ref, wr_ref.shape[0] // LANES), wr_ref[...], preferred_element_type=F32,
                          precision=lax.Precision.HIGHEST) + br_ref[...]


def _router(h2, w_router, b_router, tm=512):
    d, nr = w_router.shape
    t = h2.shape[0] // (d // LANES)
    return pl.pallas_call(
        _router_kernel,
        grid=(t // tm,),
        in_specs=[pl.BlockSpec((tm * (d // LANES), LANES), lambda i: (i, 0)),
                  pl.BlockSpec((d, nr), lambda i: (0, 0)), pl.BlockSpec((1, nr), lambda i: (0, 0))],
        out_specs=pl.BlockSpec((tm, nr), lambda i: (i, 0)),
        out_shape=jax.ShapeDtypeStruct((t, nr), F32),
        compiler_params=_params(("parallel",)),
        name="router",
    )(h2, w_router, b_router)


def _moe_kernel(tok_ref, dst_ref, be_ref, rows_ref, nu_ref,
                h_hbm, w1_ref, w3_ref, w2_ref, z_hbm,
                x0, x1, y0, y1, w1b, w3b, w2b, gsem, ssem):
    b = pl.program_id(0)
    c = w1b.shape[0] // LANES
    bm = x0.shape[0] // c
    nu = nu_ref[0]
    xs, ys = (x0, x1), (y0, y1)

    def rows_of(blk):
        return rows_ref[blk + 2]

    def gather_copy(sl, r, t):
        return pltpu.make_async_copy(h_hbm.at[pl.ds(t * c, c), :], xs[sl].at[pl.ds(r * c, c), :], gsem.at[sl])

    def scatter_copy(sl, r, d):
        return pltpu.make_async_copy(ys[sl].at[pl.ds(r * c, c), :], z_hbm.at[pl.ds(d * c, c), :], ssem.at[sl])

    def for_rows(blk, fn):
        def body(r, carry):
            fn(r)
            return carry
        lax.fori_loop(0, rows_of(blk), body, 0)

    def gather_start(blk, sl):
        for_rows(blk, lambda r: gather_copy(sl, r, tok_ref[blk * bm + r]).start())

    def gather_wait(blk, sl):
        for_rows(blk, lambda r: gather_copy(sl, r, 0).wait())

    def scatter_start(blk, sl):
        for_rows(blk, lambda r: scatter_copy(sl, r, dst_ref[jnp.maximum(blk, 0) * bm + r]).start())

    def scatter_wait(blk, sl):
        for_rows(blk, lambda r: scatter_copy(sl, r, 0).wait())

    @pl.when(b == 0)
    def _():
        x0[...] = jnp.zeros(x0.shape, F32)
        x1[...] = jnp.zeros(x1.shape, F32)
        gather_start(0, 0)

    @pl.when((b < nu) & ((b == 0) | (be_ref[b] != be_ref[jnp.maximum(b - 1, 0)])))
    def _():
        w1b[...] = w1_ref[0].astype(BF16)
        w3b[...] = w3_ref[0].astype(BF16)
        w2b[...] = w2_ref[0].astype(BF16)

    def main(sl):
        gather_start(b + 1, 1 - sl)
        scatter_start(b - 1, 1 - sl)
        gather_wait(b, sl)
        scatter_wait(b - 2, sl)
        x = _load_rows(xs[sl], c).astype(BF16)
        a1 = jnp.dot(x, w1b[...], preferred_element_type=F32)
        a3 = jnp.dot(x, w3b[...], preferred_element_type=F32)
        a = (a1 * jax.nn.sigmoid(a1)) * a3
        y = jnp.dot(a.astype(BF16), w2b[...], preferred_element_type=F32)
        _store_rows(ys[sl], y)

    def drain(sl):
        scatter_wait(b - 2, sl)
        scatter_start(b - 1, 1 - sl)
        scatter_wait(b - 1, 1 - sl)

    for sl in range(2):
        @pl.when((b < nu) & (b % 2 == sl))
        def _():
            main(sl)

        @pl.when((b == nu) & (b % 2 == sl))
        def _():
            drain(sl)


def _moe(h2, tok_buf, dst_buf, blk_e, blk_rows, n_used, w1, w3, w2):
    d, dff = w1.shape[1:]
    c = d // LANES
    t = h2.shape[0] // c
    bm = MOE_BM
    nsteps = blk_e.shape[0]
    assert tok_buf.shape[0] == nsteps * bm and blk_rows.shape[0] == nsteps + 2
    grid_spec = pltpu.PrefetchScalarGridSpec(
        num_scalar_prefetch=5,
        grid=(nsteps,),
        in_specs=[
            pl.BlockSpec(memory_space=pl.ANY),
            pl.BlockSpec((1, d, dff), lambda b, tok, dst, be, rows, nu: (be[b], 0, 0)),
            pl.BlockSpec((1, d, dff), lambda b, tok, dst, be, rows, nu: (be[b], 0, 0)),
            pl.BlockSpec((1, dff, d), lambda b, tok, dst, be, rows, nu: (be[b], 0, 0)),
        ],
        out_specs=pl.BlockSpec(memory_space=pl.ANY),
        scratch_shapes=[pltpu.VMEM((bm * c, LANES), F32), pltpu.VMEM((bm * c, LANES), F32),
                        pltpu.VMEM((bm * c, LANES), F32), pltpu.VMEM((bm * c, LANES), F32),
                        pltpu.VMEM((d, dff), BF16), pltpu.VMEM((d, dff), BF16), pltpu.VMEM((dff, d), BF16),
                        pltpu.SemaphoreType.DMA((2,)), pltpu.SemaphoreType.DMA((2,))],
    )
    return pl.pallas_call(
        _moe_kernel,
        grid_spec=grid_spec,
        out_shape=jax.ShapeDtypeStruct((TOPK_IN_GROUP * t * c, LANES), F32),
        compiler_params=pltpu.CompilerParams(dimension_semantics=("arbitrary",),
                                             vmem_limit_bytes=VMEM_LIMIT, has_side_effects=True),
        name="moe",
    )(tok_buf, dst_buf, blk_e, blk_rows, n_used, h2, w1, w3, w2)


def _final_kernel(x1_ref, z0_ref, z1_ref, w_ref, g2_ref, gn_ref, o_ref):
    c = x1_ref.shape[1] // LANES
    moe = w_ref[:, 0:1] * _load_rows(z0_ref, c) + w_ref[:, 1:2] * _load_rows(z1_ref, c)
    x2 = x1_ref[...] + g2_ref[...] * moe
    y = x2 * lax.rsqrt(jnp.mean(x2 * x2, axis=-1, keepdims=True) + EPS)
    o_ref[...] = y * gn_ref[...]


def _final(x1, z, weight, g2, gn, tm=256):
    t, d = x1.shape
    row = pl.BlockSpec((1, d), lambda i: (0, 0))
    return pl.pallas_call(
        _final_kernel,
        grid=(t // tm,),
        in_specs=[pl.BlockSpec((tm, d), lambda i: (i, 0)),
                  pl.BlockSpec((tm * (d // LANES), LANES), lambda i: (i, 0)),
                  pl.BlockSpec((tm * (d // LANES), LANES), lambda i: (t // tm + i, 0)),
                  pl.BlockSpec((tm, TOPK_IN_GROUP), lambda i: (i, 0)), row, row],
        out_specs=pl.BlockSpec((tm, d), lambda i: (i, 0)),
        out_shape=jax.ShapeDtypeStruct((t, d), F32),
        compiler_params=_params(("parallel",)),
        name="final",
    )(x1, z, z, weight, g2, gn)


def _route(logits, t):
    gl = logits[:, :N_GROUPS]
    el = logits[:, N_GROUPS:N_GROUPS + N_EXPERTS].reshape(t, N_GROUPS, EXP_PER_GROUP)
    g_prob = jax.nn.softmax(gl, axis=-1)
    grp = jnp.argmax(gl, axis=-1).astype(I32)
    p_grp = jnp.take_along_axis(g_prob, grp[:, None], axis=-1)[:, 0]
    e_in = jnp.take_along_axis(el, grp[:, None, None], axis=1)[:, 0]
    top_v, top_i = lax.top_k(e_in, TOPK_IN_GROUP)
    p_in = jax.nn.softmax(top_v, axis=-1)
    expert = grp[:, None] * EXP_PER_GROUP + top_i.astype(I32)
    weight = p_grp[:, None] * p_in

    bm = MOE_BM
    m = t * TOPK_IN_GROUP
    e_flat = expert.reshape(m)
    order = jnp.argsort(e_flat, stable=True).astype(I32)
    counts = jnp.bincount(e_flat, length=N_EXPERTS).astype(I32)
    start = jnp.cumsum(counts) - counts
    padded = ((counts + bm - 1) // bm) * bm
    pend = jnp.cumsum(padded)
    pstart = pend - padded
    nb = m // bm + N_EXPERTS + 1
    blk_e = jnp.minimum(jnp.searchsorted(pend, jnp.arange(nb, dtype=I32) * bm, side='right'),
                        N_EXPERTS - 1).astype(I32)
    n_used = (pend[-1] // bm).astype(I32).reshape(1)
    pos = jnp.arange(nb * bm, dtype=I32).reshape(nb, bm)
    src = (start[blk_e] - pstart[blk_e])[:, None] + pos
    valid = (src < (start + counts)[blk_e][:, None]) & (pos < pend[-1])
    a = order[jnp.clip(src, 0, m - 1)]
    tok = a // TOPK_IN_GROUP
    tok_buf = jnp.where(valid, tok, 0).reshape(-1)
    dst_buf = jnp.where(valid, (a % TOPK_IN_GROUP) * t + tok, 0).reshape(-1)
    blk_rows = jnp.concatenate([jnp.zeros((2,), I32), jnp.sum(valid, axis=1).astype(I32)])
    return tok_buf, dst_buf, blk_e, blk_rows, n_used, weight


def kernel(x, c, w_ada, b_ada, norm_mix, w_in, t5_table, rel_table, w_up_a, w_up_b, w_o, norm_ffn,
           w_rg, b_rg, w_re, b_re, w1, w3, w2, norm_final):
    bn, s, d = x.shape
    assert bn == 1 and w_ada.shape[0] == 1
    assert s % 1024 == 0
    t = bn * s
    x2 = x.reshape(t, d)
    n_sel = min(TOPK_MAX, s // 4)

    mod = _ada(c.reshape(d, 1), w_ada[0], b_ada[0].reshape(1, 6 * d))
    sh1, sc1, g1, sh2, sc2, g2 = [mod[:, i * d:(i + 1) * d] for i in range(6)]

    h = _rms_mod(x2, norm_mix[0].reshape(1, d), sc1, sh1, BF16)

    cols = np.cumsum([0, W_A, W_A, W_A, W_IDX_Q, D_IDX, H_IDX, W_B, W_B, W_B, d, d])
    wsl = [w_in[0][:, cols[i]:cols[i + 1]] for i in range(11)]
    wqa, wka, wva, wqi, wki, wwi, wqb, wkb, wvb, wga, wgb = wsl
    w_main = jnp.concatenate([wqa * (LOG2E / math.sqrt(HD_A)), wka, wva, wqi,
                              wqb * (LOG2E / math.sqrt(HD_B)), wkb, wvb], axis=1).astype(BF16)
    w_gate = jnp.concatenate([wga, wgb], axis=1).astype(BF16)
    w_idx = jnp.concatenate([wki, wwi, jnp.zeros((d, LANES - D_IDX - H_IDX), F32)], axis=1).astype(BF16)

    proj = _matmul(h, w_main, BF16, 1024, 1024, "proj_main")
    gates = _matmul(h, w_gate, F32, 1024, 1024, "proj_gate")
    idx = _matmul(h, w_idx, F32, 1024, LANES, "proj_idx")
    nkt = s // SEL_TK
    kit3 = jnp.transpose(idx[:, :D_IDX].astype(BF16).reshape(nkt, SEL_TK, D_IDX), (0, 2, 1))
    wi = idx[:, D_IDX:D_IDX + H_IDX] * ((H_IDX ** -0.5) * (D_IDX ** -0.5))
    kt_a = jnp.transpose(proj[:, W_A:2 * W_A])
    kt_b = jnp.transpose(proj[:, 5 * W_A:6 * W_A])

    mask4 = _select(proj, wi, kit3, n_sel, qi_col_block=3)
    y_a = _dsa_attn(proj, kt_a, mask4, t5_table, q_col=0, v_col=2)
    y_b = _band_attn(proj, kt_b, rel_table[0], q_col=4, v_col=6)

    merged = _merge(y_a, y_b, gates, w_up_a[0].astype(BF16), w_up_b[0].astype(BF16))

    nr = LANES
    w_router = jnp.concatenate([w_rg[0], w_re[0], jnp.zeros((d, nr - N_GROUPS - N_EXPERTS), F32)], axis=1)
    b_router = jnp.concatenate([b_rg[0], b_re[0], jnp.zeros((nr - N_GROUPS - N_EXPERTS,), F32)]).reshape(1, nr)
    x1, h2 = _post(x2, merged, w_o[0].astype(BF16), g1, norm_ffn[0].reshape(1, d), sc2, sh2)
    logits = _router(h2, w_router, b_router)
    tok_buf, dst_buf, blk_e, blk_rows, n_used, weight = _route(logits, t)
    z = _moe(h2, tok_buf, dst_buf, blk_e, blk_rows, n_used, w1[0], w3[0], w2[0])
    out = _final(x1, z, weight, g2, norm_final.reshape(1, d))
    return out.reshape(bn, s, d)
```

```python
import functools
import math

import numpy as np
import jax
import jax.numpy as jnp
from jax import lax
from jax.experimental import pallas as pl
from jax.experimental.pallas import tpu as pltpu

F32 = jnp.float32
BF16 = jnp.bfloat16
I32 = jnp.int32

CHUNK = 64
EPS = 1e-6
H_A, HD_A = 8, 128
H_IDX, D_IDX = 16, 64
TOPK_MAX = 256
N_T5_BUCKETS = 32
T5_MAX_DIST = 1024
H_B, HD_B = 8, 128
N_LEFT_CHUNKS = 8
REL_CLIP = 128
N_GROUPS = 8
EXP_PER_GROUP = 8
N_EXPERTS = N_GROUPS * EXP_PER_GROUP
TOPK_IN_GROUP = 2

W_A = H_A * HD_A
W_B = H_B * HD_B
W_IDX_Q = H_IDX * D_IDX

NEG = -1e30
INT_MIN = -(2 ** 31)
KEY_FMAX = 0x7F7FFFFF
LOG2E = math.log2(math.e)

LANES = 128
VMEM_LIMIT = 56 * 1024 * 1024

SEL_TQ = 128
SEL_TK = 512
ATT_TQ = 256
ATT_TK = 1024
BAND_TQ = 256
MOE_BM = 256


def _params(sem, vmem=VMEM_LIMIT):
    return pltpu.CompilerParams(dimension_semantics=sem, vmem_limit_bytes=vmem)


def _load_rows(ref, c):
    rows = ref.shape[0] // c
    return jnp.concatenate([ref[pl.ds(k, rows, stride=c), :] for k in range(c)], axis=1)


def _store_rows(ref, val):
    rows = val.shape[0]
    c = ref.shape[0] // rows
    for k in range(c):
        ref[pl.ds(k, rows, stride=c), :] = val[:, k * LANES:(k + 1) * LANES]


def _toeplitz(g, nrows, ncols):
    n = ncols + nrows - 1
    assert g.shape[-1] == n
    u = jnp.concatenate([g, jnp.zeros(g.shape[:-1] + (1,), g.dtype)], axis=-1)
    flat = jnp.tile(u, (1,) * (g.ndim - 1) + (nrows,))[..., :nrows * n]
    return flat.reshape(g.shape[:-1] + (nrows, n))[..., nrows - 1:]


def _ada_kernel(c_ref, w_ref, b_ref, o_ref, *, kc):
    d = w_ref.shape[0]
    tn = w_ref.shape[1]

    def body(k, acc):
        r0 = pl.multiple_of(k * kc, kc)
        cc = c_ref[pl.ds(r0, kc), :]
        ca = cc * jax.nn.sigmoid(cc)
        return acc + jnp.sum(w_ref[pl.ds(r0, kc), :] * ca, axis=0, keepdims=True)

    acc = lax.fori_loop(0, d // kc, body, jnp.zeros((1, tn), F32))
    o_ref[...] = acc + b_ref[...]


def _ada(c_col, w, b_row, tn=1024, kc=256):
    d, n = w.shape
    return pl.pallas_call(
        functools.partial(_ada_kernel, kc=kc),
        grid=(n // tn,),
        in_specs=[pl.BlockSpec((d, 1), lambda j: (0, 0)),
                  pl.BlockSpec((d, tn), lambda j: (0, j)),
                  pl.BlockSpec((1, tn), lambda j: (0, j))],
        out_specs=pl.BlockSpec((1, tn), lambda j: (0, j)),
        out_shape=jax.ShapeDtypeStruct((1, n), F32),
        compiler_params=_params(("arbitrary",)),
        name="ada",
    )(c_col, w, b_row)


def _rms_mod_kernel(x_ref, g_ref, sc_ref, sh_ref, o_ref):
    x = x_ref[...]
    y = x * lax.rsqrt(jnp.mean(x * x, axis=-1, keepdims=True) + EPS)
    o_ref[...] = (y * g_ref[...] * (1.0 + sc_ref[...]) + sh_ref[...]).astype(o_ref.dtype)


def _rms_mod(x2, g, sc, sh, out_dtype, tm=512):
    t, d = x2.shape
    row = pl.BlockSpec((1, d), lambda i: (0, 0))
    return pl.pallas_call(
        _rms_mod_kernel,
        grid=(t // tm,),
        in_specs=[pl.BlockSpec((tm, d), lambda i: (i, 0)), row, row, row],
        out_specs=pl.BlockSpec((tm, d), lambda i: (i, 0)),
        out_shape=jax.ShapeDtypeStruct((t, d), out_dtype),
        compiler_params=_params(("parallel",)),
        name="rms_mod",
    )(x2, g, sc, sh)


def _mm_kernel(a_ref, b_ref, o_ref):
    o_ref[...] = jnp.dot(a_ref[...], b_ref[...], preferred_element_type=F32).astype(o_ref.dtype)


def _matmul(a, b, out_dtype, tm, tn, name):
    m, k = a.shape
    n = b.shape[1]
    return pl.pallas_call(
        _mm_kernel,
        grid=(m // tm, n // tn),
        in_specs=[pl.BlockSpec((tm, k), lambda i, j: (i, 0)),
                  pl.BlockSpec((k, tn), lambda i, j: (0, j))],
        out_specs=pl.BlockSpec((tm, tn), lambda i, j: (i, j)),
        out_shape=jax.ShapeDtypeStruct((m, n), out_dtype),
        compiler_params=_params(("parallel", "arbitrary")),
        name=name,
    )(a, b)


def _sortable(x):
    bits = pltpu.bitcast(x, I32)
    return bits ^ ((bits >> 31) & 0x7FFFFFFF)


def _unsortable(k):
    return pltpu.bitcast(k ^ ((k >> 31) & 0x7FFFFFFF), F32)


def _select_kernel(qi_ref, wi_ref, kit_ref, mask_ref, key_sc, qh_sc, wb_sc, *, n_sel):
    nkt, tq, tk = key_sc.shape
    nsub = tk // LANES
    ngrp = -(-n_sel // LANES)
    i = pl.program_id(0)
    q0 = i * tq
    nvis = (q0 + tq + tk - 1) // tk

    for h in range(H_IDX):
        qh_sc[h * tq:(h + 1) * tq, :] = qi_ref[:, h * D_IDX:(h + 1) * D_IDX]
        wb_sc[h] = jnp.broadcast_to(wi_ref[:, h:h + 1], (tq, LANES))

    row = lax.broadcasted_iota(I32, (tq, 1), 0) + q0
    limit = (row // CHUNK + 1) * CHUNK
    lane = lax.broadcasted_iota(I32, (tq, LANES), 1)

    def score_tile(kt, gmax):
        k_t = kit_ref[kt]
        accs = [jnp.zeros((tq, LANES), F32) for _ in range(nsub)]
        s_all = jnp.dot(qh_sc[...], k_t, preferred_element_type=F32)
        for h in range(H_IDX):
            s = s_all[h * tq:(h + 1) * tq, :]
            wb = wb_sc[h]
            for a in range(nsub):
                accs[a] = accs[a] + wb * jnp.maximum(s[:, a * LANES:(a + 1) * LANES], 0.0)
        gmax = list(gmax)
        for a in range(nsub):
            col = lane + (kt * tk + a * LANES)
            key = jnp.where(col < limit, _sortable(accs[a]), INT_MIN)
            key_sc[kt, :, a * LANES:(a + 1) * LANES] = key
            gmax[a % ngrp] = jnp.maximum(gmax[a % ngrp], key)
        return tuple(gmax)

    gmax = lax.fori_loop(0, nvis, score_tile,
                         tuple(jnp.full((tq, LANES), INT_MIN, I32) for _ in range(ngrp)))

    def count(pred):
        def body(kt, c):
            for a in range(nsub):
                kk = key_sc[kt, :, a * LANES:(a + 1) * LANES]
                c = c + pred(kk, lane + (kt * tk + a * LANES)).astype(I32)
            return c
        c = lax.fori_loop(0, nvis, body, jnp.zeros((tq, LANES), I32))
        return jnp.sum(c, axis=1, keepdims=True)

    gmin, ghi = gmax[0], gmax[0]
    for g in gmax[1:]:
        gmin = jnp.minimum(gmin, g)
        ghi = jnp.maximum(ghi, g)
    lo0 = jnp.maximum(jnp.min(gmin, axis=1, keepdims=True), INT_MIN + 1)
    hi0 = jnp.max(ghi, axis=1, keepdims=True)
    few = limit < n_sel
    lo0 = jnp.where(few, INT_MIN + 1, lo0)
    hi0 = jnp.where(few, INT_MIN + 1, hi0)
    unknown = jnp.full((tq, 1), 2 ** 30, I32)

    def bis_cond(st):
        lo, hi, _, _ = st
        return jnp.max((lo < hi).astype(I32)) > 0

    def bis_body(st):
        lo, hi, c_lo, c_hi1 = st
        active = lo < hi
        mid = (lo | hi) - ((lo ^ hi) >> 1)
        fin = lambda k: jnp.clip(k, -KEY_FMAX - 1, KEY_FMAX)
        vmid = _sortable(0.5 * _unsortable(fin(lo)) + 0.5 * _unsortable(fin(hi)))
        mid = jnp.where((vmid > lo) & (vmid <= hi), vmid, mid)
        c = count(lambda kk, col: kk >= mid)
        up = active & (c >= n_sel)
        dn = active & (c < n_sel)
        hit = active & (c == n_sel)
        lo = jnp.where(up, mid, lo)
        c_lo = jnp.where(up, c, c_lo)
        hi = jnp.where(dn, mid - 1, jnp.where(hit, mid, hi))
        c_hi1 = jnp.where(dn, c, c_hi1)
        return lo, hi, c_lo, c_hi1

    thr, _, n_ge, n_gt = lax.while_loop(bis_cond, bis_body,
                                        (lo0, hi0, unknown, jnp.zeros((tq, 1), I32)))

    excess = (n_ge > n_sel) & jnp.logical_not(few)
    need = n_sel - n_gt
    ncol = nkt * tk

    def tie_cut():
        nbits = max(1, int(math.ceil(math.log2(ncol))))

        def cut_body(b, p):
            cand = p | jnp.left_shift(jnp.int32(1), nbits - 1 - b)
            cnt = count(lambda kk, col: (kk == thr) & (col < cand))
            return jnp.where(cnt < need, cand, p)

        p = lax.fori_loop(0, nbits, cut_body, jnp.zeros((tq, 1), I32))
        return p + 1

    cut = lax.cond(jnp.max(excess.astype(I32)) > 0, tie_cut, lambda: jnp.full((tq, 1), ncol, I32))
    cut = jnp.where(excess, cut, ncol)

    def write_tile(kt, carry):
        kk = key_sc[kt]
        col = lax.broadcasted_iota(I32, (tq, tk), 1) + kt * tk
        sel = (kk > thr) | ((kk == thr) & (col < cut))
        mask_ref[kt] = jnp.where(sel, 0.0, NEG).astype(mask_ref.dtype)
        return carry

    lax.fori_loop(0, nvis, write_tile, 0)

    def fill_tile(kt, carry):
        mask_ref[kt] = jnp.full((tq, tk), NEG, mask_ref.dtype)
        return carry

    lax.fori_loop(nvis, nkt, fill_tile, 0)


def _select(proj, wi, kit3, n_sel, qi_col_block):
    s = proj.shape[0]
    nkt, _, tk = kit3.shape
    tq = SEL_TQ
    nqb = s // tq
    assert -(-n_sel // LANES) <= tk // LANES
    return pl.pallas_call(
        functools.partial(_select_kernel, n_sel=n_sel),
        grid=(nqb,),
        in_specs=[pl.BlockSpec((tq, W_IDX_Q), lambda i: (i, qi_col_block)),
                  pl.BlockSpec((tq, H_IDX), lambda i: (i, 0)),
                  pl.BlockSpec((nkt, D_IDX, tk), lambda i: (0, 0, 0))],
        out_specs=pl.BlockSpec((None, nkt, tq, tk), lambda i: (i, 0, 0, 0)),
        out_shape=jax.ShapeDtypeStruct((nqb, nkt, tq, tk), BF16),
        scratch_shapes=[pltpu.VMEM((nkt, tq, tk), I32),
                        pltpu.VMEM((H_IDX * tq, D_IDX), BF16),
                        pltpu.VMEM((H_IDX, tq, LANES), F32)],
        compiler_params=_params(("arbitrary",)),
        name="dsa_select",
    )(proj, wi, kit3)


def _dsa_attn_kernel(qb_ref, kb_ref, first_ref, last_ref, slab_ref,
                     q_ref, kt_ref, v_ref, mask_ref, band_ref, far_ref, o_ref,
                     m_sc, l_sc, acc_sc):
    step = pl.program_id(0)
    tk = kt_ref.shape[1]
    nslab = tk // LANES
    slab0 = slab_ref[step]

    @pl.when(first_ref[step] == 1)
    def _():
        m_sc[...] = jnp.full(m_sc.shape, NEG, F32)
        l_sc[...] = jnp.zeros(l_sc.shape, F32)
        acc_sc[...] = jnp.zeros(acc_sc.shape, F32)

    def tile(near):
        mask = jnp.concatenate(
            [jnp.concatenate([mask_ref[r, j] for j in range(mask_ref.shape[1])], axis=1)
             for r in range(mask_ref.shape[0])], axis=0).astype(F32)
        ones = jnp.ones((tk, HD_A), BF16)
        for h in range(H_A):
            hs = slice(h * HD_A, (h + 1) * HD_A)
            s = jnp.dot(q_ref[:, hs], kt_ref[hs, :], preferred_element_type=F32) + mask
            if near:
                s = s + jnp.concatenate([band_ref[h, slab0 + a] for a in range(nslab)], axis=1)
                shift = 0.0
            else:
                shift = far_ref[h]
            smax = s[:, 0:LANES]
            for a in range(1, nslab):
                smax = jnp.maximum(smax, s[:, a * LANES:(a + 1) * LANES])
            m_old = m_sc[h]
            m_new = jnp.maximum(m_old, jnp.max(smax, axis=1, keepdims=True) + shift)
            alpha = jnp.exp2(m_old - m_new)
            mm = m_new - shift
            p = jnp.concatenate([jnp.exp2(s[:, a * LANES:(a + 1) * LANES] - mm) for a in range(nslab)],
                                axis=1).astype(BF16)
            pv = jnp.dot(p, jnp.concatenate([v_ref[:, hs], ones], axis=1), preferred_element_type=F32)
            acc_sc[:, hs] = alpha * acc_sc[:, hs] + pv[:, 0:HD_A]
            l_sc[h] = alpha * l_sc[h] + pv[:, HD_A:2 * HD_A]
            m_sc[h] = m_new

    @pl.when(slab0 >= 0)
    def _():
        tile(True)

    @pl.when(slab0 < 0)
    def _():
        tile(False)

    @pl.when(last_ref[step] == 1)
    def _():
        for h in range(H_A):
            hs = slice(h * HD_A, (h + 1) * HD_A)
            o_ref[:, hs] = (acc_sc[:, hs] / l_sc[h]).astype(o_ref.dtype)


def _t5_bucket_np(rel):
    nb = N_T5_BUCKETS // 2
    ret = (rel > 0).astype(np.int32) * nb
    n = np.abs(rel)
    max_exact = nb // 2
    nf = np.maximum(n, 1).astype(np.float32)
    large = max_exact + (np.log(nf / np.float32(max_exact)) / np.float32(math.log(T5_MAX_DIST / max_exact))
                         * np.float32(nb - max_exact)).astype(np.int32)
    large = np.minimum(large, nb - 1)
    return ret + np.where(n < max_exact, n, large)


def _dsa_plan(s, tq, tk):
    nqb = s // tq
    d_all = np.arange(-(s - 1), CHUNK, dtype=np.int64)
    b_all = _t5_bucket_np(d_all)
    far_bucket = int(b_all[0])
    varying = np.nonzero(b_all != far_bucket)[0]
    d_lo = int(d_all[varying[0]])
    o_min = min(int(math.ceil((d_lo - (tk - 1)) / LANES)) * LANES, 0)
    n_slabs = (-o_min + tk) // LANES
    qb, kb, first, last, slab = [], [], [], [], []
    for i in range(nqb):
        q0 = i * tq
        nvis = (q0 + tq + tk - 1) // tk
        for j in range(nvis):
            o = j * tk - q0
            qb.append(i)
            kb.append(j)
            first.append(1 if j == 0 else 0)
            last.append(1 if j == nvis - 1 else 0)
            slab.append(-1 if o < o_min else (o - o_min) // LANES)
    u = np.arange(n_slabs * LANES + tq - 1)
    g_bucket = _t5_bucket_np(np.clip(o_min + u - (tq - 1), -(s - 1), None))
    plan = tuple(np.asarray(a, np.int32) for a in (qb, kb, first, last, slab))
    return plan, g_bucket, far_bucket, n_slabs


def _dsa_attn(proj, kt, mask4, t5_table, q_col, v_col):
    s = proj.shape[0]
    tq, tk = ATT_TQ, ATT_TK
    (qb, kb, first, last, slab), g_bucket, far_bucket, n_slabs = _dsa_plan(s, tq, tk)
    nsteps = len(qb)
    t5l = t5_table * LOG2E
    g = jnp.transpose(t5l[g_bucket], (1, 0))
    band = _toeplitz(g, tq, n_slabs * LANES)
    band = jnp.transpose(band.reshape(H_A, tq, n_slabs, LANES), (0, 2, 1, 3))
    far = t5l[far_bucket]
    rq = tq // SEL_TQ
    grid_spec = pltpu.PrefetchScalarGridSpec(
        num_scalar_prefetch=5,
        grid=(nsteps,),
        in_specs=[
            pl.BlockSpec((tq, W_A), lambda t, qb, kb, f, l, sl: (qb[t], q_col)),
            pl.BlockSpec((W_A, tk), lambda t, qb, kb, f, l, sl: (0, kb[t])),
            pl.BlockSpec((tk, W_A), lambda t, qb, kb, f, l, sl: (kb[t], v_col)),
            pl.BlockSpec((rq, tk // SEL_TK, SEL_TQ, SEL_TK), lambda t, qb, kb, f, l, sl: (qb[t], kb[t], 0, 0)),
            pl.BlockSpec((H_A, n_slabs, tq, LANES), lambda t, qb, kb, f, l, sl: (0, 0, 0, 0),
                         pipeline_mode=pl.Buffered(1)),
            pl.BlockSpec(memory_space=pltpu.SMEM),
        ],
        out_specs=pl.BlockSpec((tq, W_A), lambda t, qb, kb, f, l, sl: (qb[t], 0)),
        scratch_shapes=[pltpu.VMEM((H_A, tq, LANES), F32),
                        pltpu.VMEM((H_A, tq, LANES), F32),
                        pltpu.VMEM((tq, W_A), F32)],
    )
    return pl.pallas_call(
        _dsa_attn_kernel,
        grid_spec=grid_spec,
        out_shape=jax.ShapeDtypeStruct((s, W_A), BF16),
        compiler_params=_params(("arbitrary",)),
        name="dsa_attn",
    )(jnp.asarray(qb), jnp.asarray(kb), jnp.asarray(first), jnp.asarray(last), jnp.asarray(slab),
      proj, kt, proj, mask4, band, far)


def _band_kernel(q_ref, k0_ref, k1_ref, k2_ref, v0_ref, v1_ref, v2_ref, bias_ref, o_ref):
    i = pl.program_id(0)
    tq = q_ref.shape[0]
    wk = 3 * tq
    col = lax.broadcasted_iota(I32, (tq, wk), 1)
    start_mask = jnp.where(col + (i - 2) * tq >= 0, 0.0, NEG)
    for h in range(H_B):
        hs = slice(h * HD_B, (h + 1) * HD_B)
        kc = jnp.concatenate([k0_ref[hs, :], k1_ref[hs, :], k2_ref[hs, :]], axis=1)
        vc = jnp.concatenate([v0_ref[:, hs], v1_ref[:, hs], v2_ref[:, hs]], axis=0)
        s = jnp.dot(q_ref[:, hs], kc, preferred_element_type=F32) + bias_ref[h] + start_mask
        m = jnp.max(s, axis=1, keepdims=True)
        p = jnp.exp2(s - m)
        l = jnp.sum(p, axis=1, keepdims=True)
        pv = jnp.dot(p.astype(BF16), vc, preferred_element_type=F32)
        o_ref[:, hs] = (pv / l).astype(o_ref.dtype)


def _band_bias(rel_table, tq):
    assert 2 * tq >= N_LEFT_CHUNKS * CHUNK and tq % CHUNK == 0
    wk = 3 * tq
    x = np.arange(wk + tq - 1)
    idx = np.clip(2 * tq + (tq - 1) - x, -REL_CLIP, REL_CLIP) + REL_CLIP
    bias = _toeplitz(rel_table[:, idx] * LOG2E, tq, wk)
    r = np.arange(tq)[:, None]
    c = np.arange(wk)[None, :]
    dchunk = (c - 2 * tq) // CHUNK - r // CHUNK
    in_band = (dchunk <= 0) & (dchunk >= -N_LEFT_CHUNKS)
    return jnp.where(jnp.asarray(in_band)[None], bias, NEG).astype(F32)


def _band_attn(proj, kt, rel_table, q_col, v_col):
    s = proj.shape[0]
    tq = BAND_TQ
    bias = _band_bias(rel_table, tq)

    def kspec(back):
        return pl.BlockSpec((W_B, tq), lambda i: (0, jnp.maximum(i - back, 0)))

    def vspec(back):
        return pl.BlockSpec((tq, W_B), lambda i: (jnp.maximum(i - back, 0), v_col))

    return pl.pallas_call(
        _band_kernel,
        grid=(s // tq,),
        in_specs=[pl.BlockSpec((tq, W_B), lambda i: (i, q_col)),
                  kspec(2), kspec(1), kspec(0), vspec(2), vspec(1), vspec(0),
                  pl.BlockSpec((H_B, tq, 3 * tq), lambda i: (0, 0, 0))],
        out_specs=pl.BlockSpec((tq, W_B), lambda i: (i, 0)),
        out_shape=jax.ShapeDtypeStruct((s, W_B), BF16),
        compiler_params=_params(("parallel",)),
        name="band_attn",
    )(proj, kt, kt, kt, proj, proj, proj, bias)


def _merge_kernel(ya_ref, yb_ref, ga_ref, gb_ref, wa_ref, wb_ref, o_ref):
    ua = jnp.dot(ya_ref[...], wa_ref[...], preferred_element_type=F32)
    ub = jnp.dot(yb_ref[...], wb_ref[...], preferred_element_type=F32)
    o_ref[...] = (jax.nn.sigmoid(ga_ref[...]) * ua + jax.nn.sigmoid(gb_ref[...]) * ub).astype(o_ref.dtype)


def _merge(ya, yb, gates, wa, wb, tm=256):
    s, d = ya.shape[0], wa.shape[1]
    return pl.pallas_call(
        _merge_kernel,
        grid=(s // tm,),
        in_specs=[pl.BlockSpec((tm, W_A), lambda i: (i, 0)),
                  pl.BlockSpec((tm, W_B), lambda i: (i, 0)),
                  pl.BlockSpec((tm, d), lambda i: (i, 0)),
                  pl.BlockSpec((tm, d), lambda i: (i, 1)),
                  pl.BlockSpec((W_A, d), lambda i: (0, 0)),
                  pl.BlockSpec((W_B, d), lambda i: (0, 0))],
        out_specs=pl.BlockSpec((tm, d), lambda i: (i, 0)),
        out_shape=jax.ShapeDtypeStruct((s, d), BF16),
        compiler_params=_params(("parallel",)),
        name="merge",
    )(ya, yb, gates, gates, wa, wb)


def _post_kernel(x_ref, m_ref, wo_ref, g1_ref, gn_ref, sc_ref, sh_ref, x1_ref, h2_ref):
    x1 = x_ref[...] + g1_ref[...] * jnp.dot(m_ref[...], wo_ref[...], preferred_element_type=F32)
    x1_ref[...] = x1
    y = x1 * lax.rsqrt(jnp.mean(x1 * x1, axis=-1, keepdims=True) + EPS)
    _store_rows(h2_ref, y * gn_ref[...] * (1.0 + sc_ref[...]) + sh_ref[...])


def _post(x2, merged, wo, g1, gn, sc, sh, tm=256):
    t, d = x2.shape
    row = pl.BlockSpec((1, d), lambda i: (0, 0))
    tile = pl.BlockSpec((tm, d), lambda i: (i, 0))
    return pl.pallas_call(
        _post_kernel,
        grid=(t // tm,),
        in_specs=[tile, tile, pl.BlockSpec((d, d), lambda i: (0, 0)), row, row, row, row],
        out_specs=[tile, pl.BlockSpec((tm * (d // LANES), LANES), lambda i: (i, 0))],
        out_shape=[jax.ShapeDtypeStruct((t, d), F32), jax.ShapeDtypeStruct((t * (d // LANES), LANES), F32)],
        compiler_params=_params(("parallel",)),
        name="post",
    )(x2, merged, wo, g1, gn, sc, sh)


def _router_kernel(h_ref, wr_ref, br_ref, lg_ref):
    lg_ref[...] = jnp.dot(_load_rows(h_ref, wr_ref.shape[0] // LANES), wr_ref[...], preferred_element_type=F32,
                          precision=lax.Precision.HIGHEST) + br_ref[...]


def _router(h2, w_router, b_router, tm=512):
    d, nr = w_router.shape
    t = h2.shape[0] // (d // LANES)
    return pl.pallas_call(
        _router_kernel,
        grid=(t // tm,),
        in_specs=[pl.BlockSpec((tm * (d // LANES), LANES), lambda i: (i, 0)),
                  pl.BlockSpec((d, nr), lambda i: (0, 0)), pl.BlockSpec((1, nr), lambda i: (0, 0))],
        out_specs=pl.BlockSpec((tm, nr), lambda i: (i, 0)),
        out_shape=jax.ShapeDtypeStruct((t, nr), F32),
        compiler_params=_params(("parallel",)),
        name="router",
    )(h2, w_router, b_router)


def _moe_kernel(tok_ref, dst_ref, be_ref, rows_ref, nu_ref,
                h_hbm, w1_ref, w3_ref, w2_ref, z_hbm,
                x0, x1, y0, y1, w1b, w3b, w2b, gsem, ssem):
    b = pl.program_id(0)
    c = w1b.shape[0] // LANES
    bm = x0.shape[0] // c
    nu = nu_ref[0]
    xs, ys = (x0, x1), (y0, y1)

    def rows_of(blk):
        return rows_ref[blk + 2]

    def gather_copy(sl, r, t):
        return pltpu.make_async_copy(h_hbm.at[pl.ds(t * c, c), :], xs[sl].at[pl.ds(r * c, c), :], gsem.at[sl])

    def scatter_copy(sl, r, d):
        return pltpu.make_async_copy(ys[sl].at[pl.ds(r * c, c), :], z_hbm.at[pl.ds(d * c, c), :], ssem.at[sl])

    def for_rows(blk, fn):
        def body(r, carry):
            fn(r)
            return carry
        lax.fori_loop(0, rows_of(blk), body, 0)

    def gather_start(blk, sl):
        for_rows(blk, lambda r: gather_copy(sl, r, tok_ref[blk * bm + r]).start())

    def gather_wait(blk, sl):
        for_rows(blk, lambda r: gather_copy(sl, r, 0).wait())

    def scatter_start(blk, sl):
        for_rows(blk, lambda r: scatter_copy(sl, r, dst_ref[jnp.maximum(blk, 0) * bm + r]).start())

    def scatter_wait(blk, sl):
        for_rows(blk, lambda r: scatter_copy(sl, r, 0).wait())

    @pl.when(b == 0)
    def _():
        x0[...] = jnp.zeros(x0.shape, F32)
        x1[...] = jnp.zeros(x1.shape, F32)
        gather_start(0, 0)

    @pl.when((b < nu) & ((b == 0) | (be_ref[b] != be_ref[jnp.maximum(b - 1, 0)])))
    def _():
        w1b[...] = w1_ref[0].astype(BF16)
        w3b[...] = w3_ref[0].astype(BF16)
        w2b[...] = w2_ref[0].astype(BF16)

    def main(sl):
        gather_start(b + 1, 1 - sl)
        scatter_start(b - 1, 1 - sl)
        gather_wait(b, sl)
        scatter_wait(b - 2, sl)
        x = _load_rows(xs[sl], c).astype(BF16)
        a1 = jnp.dot(x, w1b[...], preferred_element_type=F32)
        a3 = jnp.dot(x, w3b[...], preferred_element_type=F32)
        a = (a1 * jax.nn.sigmoid(a1)) * a3
        y = jnp.dot(a.astype(BF16), w2b[...], preferred_element_type=F32)
        _store_rows(ys[sl], y)

    def drain(sl):
        scatter_wait(b - 2, sl)
        scatter_start(b - 1, 1 - sl)
        scatter_wait(b - 1, 1 - sl)

    for sl in range(2):
        @pl.when((b < nu) & (b % 2 == sl))
        def _():
            main(sl)

        @pl.when((b == nu) & (b % 2 == sl))
        def _():
            drain(sl)


def _moe(h2, tok_buf, dst_buf, blk_e, blk_rows, n_used, w1, w3, w2):
    d, dff = w1.shape[1:]
    c = d // LANES
    t = h2.shape[0] // c
    bm = MOE_BM
    nsteps = blk_e.shape[0]
    assert tok_buf.shape[0] == nsteps * bm and blk_rows.shape[0] == nsteps + 2
    grid_spec = pltpu.PrefetchScalarGridSpec(
        num_scalar_prefetch=5,
        grid=(nsteps,),
        in_specs=[
            pl.BlockSpec(memory_space=pl.ANY),
            pl.BlockSpec((1, d, dff), lambda b, tok, dst, be, rows, nu: (be[b], 0, 0)),
            pl.BlockSpec((1, d, dff), lambda b, tok, dst, be, rows, nu: (be[b], 0, 0)),
            pl.BlockSpec((1, dff, d), lambda b, tok, dst, be, rows, nu: (be[b], 0, 0)),
        ],
        out_specs=pl.BlockSpec(memory_space=pl.ANY),
        scratch_shapes=[pltpu.VMEM((bm * c, LANES), F32), pltpu.VMEM((bm * c, LANES), F32),
                        pltpu.VMEM((bm * c, LANES), F32), pltpu.VMEM((bm * c, LANES), F32),
                        pltpu.VMEM((d, dff), BF16), pltpu.VMEM((d, dff), BF16), pltpu.VMEM((dff, d), BF16),
                        pltpu.SemaphoreType.DMA((2,)), pltpu.SemaphoreType.DMA((2,))],
    )
    return pl.pallas_call(
        _moe_kernel,
        grid_spec=grid_spec,
        out_shape=jax.ShapeDtypeStruct((TOPK_IN_GROUP * t * c, LANES), F32),
        compiler_params=pltpu.CompilerParams(dimension_semantics=("arbitrary",),
                                             vmem_limit_bytes=VMEM_LIMIT, has_side_effects=True),
        name="moe",
    )(tok_buf, dst_buf, blk_e, blk_rows, n_used, h2, w1, w3, w2)


def _final_kernel(x1_ref, z0_ref, z1_ref, w_ref, g2_ref, gn_ref, o_ref):
    c = x1_ref.shape[1] // LANES
    moe = w_ref[:, 0:1] * _load_rows(z0_ref, c) + w_ref[:, 1:2] * _load_rows(z1_ref, c)
    x2 = x1_ref[...] + g2_ref[...] * moe
    y = x2 * lax.rsqrt(jnp.mean(x2 * x2, axis=-1, keepdims=True) + EPS)
    o_ref[...] = y * gn_ref[...]


def _final(x1, z, weight, g2, gn, tm=256):
    t, d = x1.shape
    row = pl.BlockSpec((1, d), lambda i: (0, 0))
    return pl.pallas_call(
        _final_kernel,
        grid=(t // tm,),
        in_specs=[pl.BlockSpec((tm, d), lambda i: (i, 0)),
                  pl.BlockSpec((tm * (d // LANES), LANES), lambda i: (i, 0)),
                  pl.BlockSpec((tm * (d // LANES), LANES), lambda i: (t // tm + i, 0)),
                  pl.BlockSpec((tm, TOPK_IN_GROUP), lambda i: (i, 0)), row, row],
        out_specs=pl.BlockSpec((tm, d), lambda i: (i, 0)),
        out_shape=jax.ShapeDtypeStruct((t, d), F32),
        compiler_params=_params(("parallel",)),
        name="final",
    )(x1, z, z, weight, g2, gn)


def _route(logits, t):
    gl = logits[:, :N_GROUPS]
    el = logits[:, N_GROUPS:N_GROUPS + N_EXPERTS].reshape(t, N_GROUPS, EXP_PER_GROUP)
    g_prob = jax.nn.softmax(gl, axis=-1)
    grp = jnp.argmax(gl, axis=-1).astype(I32)
    p_grp = jnp.take_along_axis(g_prob, grp[:, None], axis=-1)[:, 0]
    e_in = jnp.take_along_axis(el, grp[:, None, None], axis=1)[:, 0]
    top_v, top_i = lax.top_k(e_in, TOPK_IN_GROUP)
    p_in = jax.nn.softmax(top_v, axis=-1)
    expert = grp[:, None] * EXP_PER_GROUP + top_i.astype(I32)
    weight = p_grp[:, None] * p_in

    bm = MOE_BM
    m = t * TOPK_IN_GROUP
    e_flat = expert.reshape(m)
    order = jnp.argsort(e_flat, stable=True).astype(I32)
    counts = jnp.bincount(e_flat, length=N_EXPERTS).astype(I32)
    start = jnp.cumsum(counts) - counts
    padded = ((counts + bm - 1) // bm) * bm
    pend = jnp.cumsum(padded)
    pstart = pend - padded
    nb = m // bm + N_EXPERTS + 1
    blk_e = jnp.minimum(jnp.searchsorted(pend, jnp.arange(nb, dtype=I32) * bm, side='right'),
                        N_EXPERTS - 1).astype(I32)
    n_used = (pend[-1] // bm).astype(I32).reshape(1)
    pos = jnp.arange(nb * bm, dtype=I32).reshape(nb, bm)
    src = (start[blk_e] - pstart[blk_e])[:, None] + pos
    valid = (src < (start + counts)[blk_e][:, None]) & (pos < pend[-1])
    a = order[jnp.clip(src, 0, m - 1)]
    tok = a // TOPK_IN_GROUP
    tok_buf = jnp.where(valid, tok, 0).reshape(-1)
    dst_buf = jnp.where(valid, (a % TOPK_IN_GROUP) * t + tok, 0).reshape(-1)
    blk_rows = jnp.concatenate([jnp.zeros((2,), I32), jnp.sum(valid, axis=1).astype(I32)])
    return tok_buf, dst_buf, blk_e, blk_rows, n_used, weight


def kernel(x, c, w_ada, b_ada, norm_mix, w_in, t5_table, rel_table, w_up_a, w_up_b, w_o, norm_ffn,
           w_rg, b_rg, w_re, b_re, w1, w3, w2, norm_final):
    bn, s, d = x.shape
    assert bn == 1 and w_ada.shape[0] == 1
    assert s % 1024 == 0
    t = bn * s
    x2 = x.reshape(t, d)
    n_sel = min(TOPK_MAX, s // 4)

    mod = _ada(c.reshape(d, 1), w_ada[0], b_ada[0].reshape(1, 6 * d))
    sh1, sc1, g1, sh2, sc2, g2 = [mod[:, i * d:(i + 1) * d] for i in range(6)]

    h = _rms_mod(x2, norm_mix[0].reshape(1, d), sc1, sh1, BF16)

    cols = np.cumsum([0, W_A, W_A, W_A, W_IDX_Q, D_IDX, H_IDX, W_B, W_B, W_B, d, d])
    wsl = [w_in[0][:, cols[i]:cols[i + 1]] for i in range(11)]
    wqa, wka, wva, wqi, wki, wwi, wqb, wkb, wvb, wga, wgb = wsl
    w_main = jnp.concatenate([wqa * (LOG2E / math.sqrt(HD_A)), wka, wva, wqi,
                              wqb * (LOG2E / math.sqrt(HD_B)), wkb, wvb], axis=1).astype(BF16)
    w_gate = jnp.concatenate([wga, wgb], axis=1).astype(BF16)
    w_idx = jnp.concatenate([wki, wwi, jnp.zeros((d, LANES - D_IDX - H_IDX), F32)], axis=1).astype(BF16)

    proj = _matmul(h, w_main, BF16, 1024, 1024, "proj_main")
    gates = _matmul(h, w_gate, F32, 1024, 1024, "proj_gate")
    idx = _matmul(h, w_idx, F32, 1024, LANES, "proj_idx")
    nkt = s // SEL_TK
    kit3 = jnp.transpose(idx[:, :D_IDX].astype(BF16).reshape(nkt, SEL_TK, D_IDX), (0, 2, 1))
    wi = idx[:, D_IDX:D_IDX + H_IDX] * ((H_IDX ** -0.5) * (D_IDX ** -0.5))
    kt_a = jnp.transpose(proj[:, W_A:2 * W_A])
    kt_b = jnp.transpose(proj[:, 5 * W_A:6 * W_A])

    mask4 = _select(proj, wi, kit3, n_sel, qi_col_block=3)
    y_a = _dsa_attn(proj, kt_a, mask4, t5_table, q_col=0, v_col=2)
    y_b = _band_attn(proj, kt_b, rel_table[0], q_col=4, v_col=6)

    merged = _merge(y_a, y_b, gates, w_up_a[0].astype(BF16), w_up_b[0].astype(BF16))

    nr = LANES
    w_router = jnp.concatenate([w_rg[0], w_re[0], jnp.zeros((d, nr - N_GROUPS - N_EXPERTS), F32)], axis=1)
    b_router = jnp.concatenate([b_rg[0], b_re[0], jnp.zeros((nr - N_GROUPS - N_EXPERTS,), F32)]).reshape(1, nr)
    x1, h2 = _post(x2, merged, w_o[0].astype(BF16), g1, norm_ffn[0].reshape(1, d), sc2, sh2)
    logits = _router(h2, w_router, b_router)
    tok_buf, dst_buf, blk_e, blk_rows, n_used, weight = _route(logits, t)
    z = _moe(h2, tok_buf, dst_buf, blk_e, blk_rows, n_used, w1[0], w3[0], w2[0])
    out = _final(x1, z, weight, g2, norm_final.reshape(1, d))
    return out.reshape(bn, s, d)
```

```python
import functools
import math

import numpy as np
import jax
import jax.numpy as jnp
from jax import lax
from jax.experimental import pallas as pl
from jax.experimental.pallas import tpu as pltpu

F32 = jnp.float32
BF16 = jnp.bfloat16
I32 = jnp.int32

CHUNK = 64
EPS = 1e-6
H_A, HD_A = 8, 128
H_IDX, D_IDX = 16, 64
TOPK_MAX = 256
N_T5_BUCKETS = 32
T5_MAX_DIST = 1024
H_B, HD_B = 8, 128
N_LEFT_CHUNKS = 8
REL_CLIP = 128
N_GROUPS = 8
EXP_PER_GROUP = 8
N_EXPERTS = N_GROUPS * EXP_PER_GROUP
TOPK_IN_GROUP = 2

W_A = H_A * HD_A
W_B = H_B * HD_B
W_IDX_Q = H_IDX * D_IDX

NEG = -1e30
INT_MIN = -(2 ** 31)
KEY_FMAX = 0x7F7FFFFF
LOG2E = math.log2(math.e)

LANES = 128
VMEM_LIMIT = 56 * 1024 * 1024

SEL_TQ = 128
SEL_TK = 512
ATT_TQ = 256
ATT_TK = 1024
BAND_TQ = 256
MOE_BM = 256


def _params(sem, vmem=VMEM_LIMIT):
    return pltpu.CompilerParams(dimension_semantics=sem, vmem_limit_bytes=vmem)


def _load_rows(ref, c):
    rows = ref.shape[0] // c
    return jnp.concatenate([ref[pl.ds(k, rows, stride=c), :] for k in range(c)], axis=1)


def _store_rows(ref, val):
    rows = val.shape[0]
    c = ref.shape[0] // rows
    for k in range(c):
        ref[pl.ds(k, rows, stride=c), :] = val[:, k * LANES:(k + 1) * LANES]


def _toeplitz(g, nrows, ncols):
    n = ncols + nrows - 1
    assert g.shape[-1] == n
    u = jnp.concatenate([g, jnp.zeros(g.shape[:-1] + (1,), g.dtype)], axis=-1)
    flat = jnp.tile(u, (1,) * (g.ndim - 1) + (nrows,))[..., :nrows * n]
    return flat.reshape(g.shape[:-1] + (nrows, n))[..., nrows - 1:]


def _ada_kernel(c_ref, w_ref, b_ref, o_ref, *, kc):
    d = w_ref.shape[0]
    tn = w_ref.shape[1]

    def body(k, acc):
        r0 = pl.multiple_of(k * kc, kc)
        cc = c_ref[pl.ds(r0, kc), :]
        ca = cc * jax.nn.sigmoid(cc)
        return acc + jnp.sum(w_ref[pl.ds(r0, kc), :] * ca, axis=0, keepdims=True)

    acc = lax.fori_loop(0, d // kc, body, jnp.zeros((1, tn), F32))
    o_ref[...] = acc + b_ref[...]


def _ada(c_col, w, b_row, tn=1024, kc=256):
    d, n = w.shape
    return pl.pallas_call(
        functools.partial(_ada_kernel, kc=kc),
        grid=(n // tn,),
        in_specs=[pl.BlockSpec((d, 1), lambda j: (0, 0)),
                  pl.BlockSpec((d, tn), lambda j: (0, j)),
                  pl.BlockSpec((1, tn), lambda j: (0, j))],
        out_specs=pl.BlockSpec((1, tn), lambda j: (0, j)),
        out_shape=jax.ShapeDtypeStruct((1, n), F32),
        compiler_params=_params(("arbitrary",)),
        name="ada",
    )(c_col, w, b_row)


def _rms_mod_kernel(x_ref, g_ref, sc_ref, sh_ref, o_ref):
    x = x_ref[...]
    y = x * lax.rsqrt(jnp.mean(x * x, axis=-1, keepdims=True) + EPS)
    o_ref[...] = (y * g_ref[...] * (1.0 + sc_ref[...]) + sh_ref[...]).astype(o_ref.dtype)


def _rms_mod(x2, g, sc, sh, out_dtype, tm=512):
    t, d = x2.shape
    row = pl.BlockSpec((1, d), lambda i: (0, 0))
    return pl.pallas_call(
        _rms_mod_kernel,
        grid=(t // tm,),
        in_specs=[pl.BlockSpec((tm, d), lambda i: (i, 0)), row, row, row],
        out_specs=pl.BlockSpec((tm, d), lambda i: (i, 0)),
        out_shape=jax.ShapeDtypeStruct((t, d), out_dtype),
        compiler_params=_params(("parallel",)),
        name="rms_mod",
    )(x2, g, sc, sh)


def _mm_kernel(a_ref, b_ref, o_ref):
    o_ref[...] = jnp.dot(a_ref[...], b_ref[...], preferred_element_type=F32).astype(o_ref.dtype)


def _matmul(a, b, out_dtype, tm, tn, name):
    m, k = a.shape
    n = b.shape[1]
    return pl.pallas_call(
        _mm_kernel,
        grid=(m // tm, n // tn),
        in_specs=[pl.BlockSpec((tm, k), lambda i, j: (i, 0)),
                  pl.BlockSpec((k, tn), lambda i, j: (0, j))],
        out_specs=pl.BlockSpec((tm, tn), lambda i, j: (i, j)),
        out_shape=jax.ShapeDtypeStruct((m, n), out_dtype),
        compiler_params=_params(("parallel", "arbitrary")),
        name=name,
    )(a, b)


def _sortable(x):
    bits = pltpu.bitcast(x, I32)
    return bits ^ ((bits >> 31) & 0x7FFFFFFF)


def _unsortable(k):
    return pltpu.bitcast(k ^ ((k >> 31) & 0x7FFFFFFF), F32)


def _select_kernel(qi_ref, wi_ref, kit_ref, mask_ref, key_sc, qh_sc, wb_sc, *, n_sel):
    nkt, tq, tk = key_sc.shape
    nsub = tk // LANES
    ngrp = -(-n_sel // LANES)
    i = pl.program_id(0)
    q0 = i * tq
    nvis = (q0 + tq + tk - 1) // tk

    for h in range(H_IDX):
        qh_sc[h * tq:(h + 1) * tq, :] = qi_ref[:, h * D_IDX:(h + 1) * D_IDX]
        wb_sc[h] = jnp.broadcast_to(wi_ref[:, h:h + 1], (tq, LANES))

    row = lax.broadcasted_iota(I32, (tq, 1), 0) + q0
    limit = (row // CHUNK + 1) * CHUNK
    lane = lax.broadcasted_iota(I32, (tq, LANES), 1)

    def score_tile(kt, gmax):
        k_t = kit_ref[kt]
        accs = [jnp.zeros((tq, LANES), F32) for _ in range(nsub)]
        s_all = jnp.dot(qh_sc[...], k_t, preferred_element_type=F32)
        for h in range(H_IDX):
            s = s_all[h * tq:(h + 1) * tq, :]
            wb = wb_sc[h]
            for a in range(nsub):
                accs[a] = accs[a] + wb * jnp.maximum(s[:, a * LANES:(a + 1) * LANES], 0.0)
        gmax = list(gmax)
        for a in range(nsub):
            col = lane + (kt * tk + a * LANES)
            key = jnp.where(col < limit, _sortable(accs[a]), INT_MIN)
            key_sc[kt, :, a * LANES:(a + 1) * LANES] = key
            gmax[a % ngrp] = jnp.maximum(gmax[a % ngrp], key)
        return tuple(gmax)

    gmax = lax.fori_loop(0, nvis, score_tile,
                         tuple(jnp.full((tq, LANES), INT_MIN, I32) for _ in range(ngrp)))

    def count(pred):
        def body(kt, c):
            for a in range(nsub):
                kk = key_sc[kt, :, a * LANES:(a + 1) * LANES]
                c = c + pred(kk, lane + (kt * tk + a * LANES)).astype(I32)
            return c
        c = lax.fori_loop(0, nvis, body, jnp.zeros((tq, LANES), I32))
        return jnp.sum(c, axis=1, keepdims=True)

    gmin, ghi = gmax[0], gmax[0]
    for g in gmax[1:]:
        gmin = jnp.minimum(gmin, g)
        ghi = jnp.maximum(ghi, g)
    lo0 = jnp.maximum(jnp.min(gmin, axis=1, keepdims=True), INT_MIN + 1)
    hi0 = jnp.max(ghi, axis=1, keepdims=True)
    few = limit < n_sel
    lo0 = jnp.where(few, INT_MIN + 1, lo0)
    hi0 = jnp.where(few, INT_MIN + 1, hi0)
    unknown = jnp.full((tq, 1), 2 ** 30, I32)

    def bis_cond(st):
        lo, hi, _, _ = st
        return jnp.max((lo < hi).astype(I32)) > 0

    def bis_body(st):
        lo, hi, c_lo, c_hi1 = st
        active = lo < hi
        mid = (lo | hi) - ((lo ^ hi) >> 1)
        fin = lambda k: jnp.clip(k, -KEY_FMAX - 1, KEY_FMAX)
        vmid = _sortable(0.5 * _unsortable(fin(lo)) + 0.5 * _unsortable(fin(hi)))
        mid = jnp.where((vmid > lo) & (vmid <= hi), vmid, mid)
        c = count(lambda kk, col: kk >= mid)
        up = active & (c >= n_sel)
        dn = active & (c < n_sel)
        hit = active & (c == n_sel)
        lo = jnp.where(up, mid, lo)
        c_lo = jnp.where(up, c, c_lo)
        hi = jnp.where(dn, mid - 1, jnp.where(hit, mid, hi))
        c_hi1 = jnp.where(dn, c, c_hi1)
        return lo, hi, c_lo, c_hi1

    thr, _, n_ge, n_gt = lax.while_loop(bis_cond, bis_body,
                                        (lo0, hi0, unknown, jnp.zeros((tq, 1), I32)))

    excess = (n_ge > n_sel) & jnp.logical_not(few)
    need = n_sel - n_gt
    ncol = nkt * tk

    def tie_cut():
        nbits = max(1, int(math.ceil(math.log2(ncol))))

        def cut_body(b, p):
            cand = p | jnp.left_shift(jnp.int32(1), nbits - 1 - b)
            cnt = count(lambda kk, col: (kk == thr) & (col < cand))
            return jnp.where(cnt < need, cand, p)

        p = lax.fori_loop(0, nbits, cut_body, jnp.zeros((tq, 1), I32))
        return p + 1

    cut = lax.cond(jnp.max(excess.astype(I32)) > 0, tie_cut, lambda: jnp.full((tq, 1), ncol, I32))
    cut = jnp.where(excess, cut, ncol)

    def write_tile(kt, carry):
        kk = key_sc[kt]
        col = lax.broadcasted_iota(I32, (tq, tk), 1) + kt * tk
        sel = (kk > thr) | ((kk == thr) & (col < cut))
        mask_ref[kt] = jnp.where(sel, 0.0, NEG).astype(mask_ref.dtype)
        return carry

    lax.fori_loop(0, nvis, write_tile, 0)

    def fill_tile(kt, carry):
        mask_ref[kt] = jnp.full((tq, tk), NEG, mask_ref.dtype)
        return carry

    lax.fori_loop(nvis, nkt, fill_tile, 0)


def _select(proj, wi, kit3, n_sel, qi_col_block):
    s = proj.shape[0]
    nkt, _, tk = kit3.shape
    tq = SEL_TQ
    nqb = s // tq
    assert -(-n_sel // LANES) <= tk // LANES
    return pl.pallas_call(
        functools.partial(_select_kernel, n_sel=n_sel),
        grid=(nqb,),
        in_specs=[pl.BlockSpec((tq, W_IDX_Q), lambda i: (i, qi_col_block)),
                  pl.BlockSpec((tq, H_IDX), lambda i: (i, 0)),
                  pl.BlockSpec((nkt, D_IDX, tk), lambda i: (0, 0, 0))],
        out_specs=pl.BlockSpec((None, nkt, tq, tk), lambda i: (i, 0, 0, 0)),
        out_shape=jax.ShapeDtypeStruct((nqb, nkt, tq, tk), BF16),
        scratch_shapes=[pltpu.VMEM((nkt, tq, tk), I32),
                        pltpu.VMEM((H_IDX * tq, D_IDX), BF16),
                        pltpu.VMEM((H_IDX, tq, LANES), F32)],
        compiler_params=_params(("arbitrary",)),
        name="dsa_select",
    )(proj, wi, kit3)


def _dsa_attn_kernel(qb_ref, kb_ref, first_ref, last_ref, slab_ref,
                     q_ref, kt_ref, v_ref, mask_ref, gen_ref, far_ref, o_ref,
                     m_sc, l_sc, acc_sc, band_ref):
    step = pl.program_id(0)
    tq = q_ref.shape[0]
    tk = kt_ref.shape[1]
    nslab = tk // LANES
    slab0 = slab_ref[step]

    @pl.when(step == 0)
    def _():
        win = LANES + tq
        for h in range(H_A):
            for u in range(band_ref.shape[1]):
                rows = jnp.broadcast_to(gen_ref[h:h + 1, u * LANES:u * LANES + win], (tq, win))
                rows = pltpu.roll(rows, win - (tq - 1), 1, stride=1, stride_axis=0)
                band_ref[h, u] = rows[:, 0:LANES]

    @pl.when(first_ref[step] == 1)
    def _():
        m_sc[...] = jnp.full(m_sc.shape, NEG, F32)
        l_sc[...] = jnp.zeros(l_sc.shape, F32)
        acc_sc[...] = jnp.zeros(acc_sc.shape, F32)

    def tile(near):
        mask = jnp.concatenate(
            [jnp.concatenate([mask_ref[r, j] for j in range(mask_ref.shape[1])], axis=1)
             for r in range(mask_ref.shape[0])], axis=0).astype(F32)
        ones = jnp.ones((tk, HD_A), BF16)
        for h in range(H_A):
            hs = slice(h * HD_A, (h + 1) * HD_A)
            s = jnp.dot(q_ref[:, hs], kt_ref[hs, :], preferred_element_type=F32) + mask
            if near:
                s = s + jnp.concatenate([band_ref[h, slab0 + a] for a in range(nslab)], axis=1)
                shift = 0.0
            else:
                shift = far_ref[h]
            smax = s[:, 0:LANES]
            for a in range(1, nslab):
                smax = jnp.maximum(smax, s[:, a * LANES:(a + 1) * LANES])
            m_old = m_sc[h]
            m_new = jnp.maximum(m_old, jnp.max(smax, axis=1, keepdims=True) + shift)
            alpha = jnp.exp2(m_old - m_new)
            mm = m_new - shift
            p = jnp.concatenate([jnp.exp2(s[:, a * LANES:(a + 1) * LANES] - mm) for a in range(nslab)],
                                axis=1).astype(BF16)
            pv = jnp.dot(p, jnp.concatenate([v_ref[:, hs], ones], axis=1), preferred_element_type=F32)
            acc_sc[:, hs] = alpha * acc_sc[:, hs] + pv[:, 0:HD_A]
            l_sc[h] = alpha * l_sc[h] + pv[:, HD_A:2 * HD_A]
            m_sc[h] = m_new

    @pl.when(slab0 >= 0)
    def _():
        tile(True)

    @pl.when(slab0 < 0)
    def _():
        tile(False)

    @pl.when(last_ref[step] == 1)
    def _():
        for h in range(H_A):
            hs = slice(h * HD_A, (h + 1) * HD_A)
            o_ref[:, hs] = (acc_sc[:, hs] / l_sc[h]).astype(o_ref.dtype)


def _t5_bucket_np(rel):
    nb = N_T5_BUCKETS // 2
    ret = (rel > 0).astype(np.int32) * nb
    n = np.abs(rel)
    max_exact = nb // 2
    nf = np.maximum(n, 1).astype(np.float32)
    large = max_exact + (np.log(nf / np.float32(max_exact)) / np.float32(math.log(T5_MAX_DIST / max_exact))
                         * np.float32(nb - max_exact)).astype(np.int32)
    large = np.minimum(large, nb - 1)
    return ret + np.where(n < max_exact, n, large)


def _dsa_plan(s, tq, tk):
    nqb = s // tq
    d_all = np.arange(-(s - 1), CHUNK, dtype=np.int64)
    b_all = _t5_bucket_np(d_all)
    far_bucket = int(b_all[0])
    varying = np.nonzero(b_all != far_bucket)[0]
    d_lo = int(d_all[varying[0]])
    o_min = min(int(math.ceil((d_lo - (tk - 1)) / LANES)) * LANES, 0)
    n_slabs = (-o_min + tk) // LANES
    qb, kb, first, last, slab = [], [], [], [], []
    for i in range(nqb):
        q0 = i * tq
        nvis = (q0 + tq + tk - 1) // tk
        for j in range(nvis):
            o = j * tk - q0
            qb.append(i)
            kb.append(j)
            first.append(1 if j == 0 else 0)
            last.append(1 if j == nvis - 1 else 0)
            slab.append(-1 if o < o_min else (o - o_min) // LANES)
    u = np.arange(n_slabs * LANES + tq - 1)
    g_bucket = _t5_bucket_np(np.clip(o_min + u - (tq - 1), -(s - 1), None))
    plan = tuple(np.asarray(a, np.int32) for a in (qb, kb, first, last, slab))
    return plan, g_bucket, far_bucket, n_slabs


def _dsa_attn(proj, kt, mask4, t5_table, q_col, v_col):
    s = proj.shape[0]
    tq, tk = ATT_TQ, ATT_TK
    (qb, kb, first, last, slab), g_bucket, far_bucket, n_slabs = _dsa_plan(s, tq, tk)
    nsteps = len(qb)
    t5l = t5_table * LOG2E
    gen = jnp.transpose(t5l[g_bucket], (1, 0))
    gen = jnp.pad(gen, ((0, 0), (0, n_slabs * LANES + tq - gen.shape[1])))
    far = t5l[far_bucket]
    rq = tq // SEL_TQ
    grid_spec = pltpu.PrefetchScalarGridSpec(
        num_scalar_prefetch=5,
        grid=(nsteps,),
        in_specs=[
            pl.BlockSpec((tq, W_A), lambda t, qb, kb, f, l, sl: (qb[t], q_col)),
            pl.BlockSpec((W_A, tk), lambda t, qb, kb, f, l, sl: (0, kb[t])),
            pl.BlockSpec((tk, W_A), lambda t, qb, kb, f, l, sl: (kb[t], v_col)),
            pl.BlockSpec((rq, tk // SEL_TK, SEL_TQ, SEL_TK), lambda t, qb, kb, f, l, sl: (qb[t], kb[t], 0, 0)),
            pl.BlockSpec((H_A, n_slabs * LANES + tq), lambda t, qb, kb, f, l, sl: (0, 0)),
            pl.BlockSpec(memory_space=pltpu.SMEM),
        ],
        out_specs=pl.BlockSpec((tq, W_A), lambda t, qb, kb, f, l, sl: (qb[t], 0)),
        scratch_shapes=[pltpu.VMEM((H_A, tq, LANES), F32),
                        pltpu.VMEM((H_A, tq, LANES), F32),
                        pltpu.VMEM((tq, W_A), F32),
                        pltpu.VMEM((H_A, n_slabs, tq, LANES), F32)],
    )
    return pl.pallas_call(
        _dsa_attn_kernel,
        grid_spec=grid_spec,
        out_shape=jax.ShapeDtypeStruct((s, W_A), BF16),
        compiler_params=_params(("arbitrary",)),
        name="dsa_attn",
    )(jnp.asarray(qb), jnp.asarray(kb), jnp.asarray(first), jnp.asarray(last), jnp.asarray(slab),
      proj, kt, proj, mask4, gen, far)


def _band_kernel(q_ref, k0_ref, k1_ref, k2_ref, v0_ref, v1_ref, v2_ref, bias_ref, o_ref):
    i = pl.program_id(0)
    tq = q_ref.shape[0]
    wk = 3 * tq
    col = lax.broadcasted_iota(I32, (tq, wk), 1)
    start_mask = jnp.where(col + (i - 2) * tq >= 0, 0.0, NEG)
    for h in range(H_B):
        hs = slice(h * HD_B, (h + 1) * HD_B)
        kc = jnp.concatenate([k0_ref[hs, :], k1_ref[hs, :], k2_ref[hs, :]], axis=1)
        vc = jnp.concatenate([v0_ref[:, hs], v1_ref[:, hs], v2_ref[:, hs]], axis=0)
        s = jnp.dot(q_ref[:, hs], kc, preferred_element_type=F32) + bias_ref[h] + start_mask
        m = jnp.max(s, axis=1, keepdims=True)
        p = jnp.exp2(s - m)
        l = jnp.sum(p, axis=1, keepdims=True)
        pv = jnp.dot(p.astype(BF16), vc, preferred_element_type=F32)
        o_ref[:, hs] = (pv / l).astype(o_ref.dtype)


def _band_bias(rel_table, tq):
    assert 2 * tq >= N_LEFT_CHUNKS * CHUNK and tq % CHUNK == 0
    wk = 3 * tq
    x = np.arange(wk + tq - 1)
    idx = np.clip(2 * tq + (tq - 1) - x, -REL_CLIP, REL_CLIP) + REL_CLIP
    bias = _toeplitz(rel_table[:, idx] * LOG2E, tq, wk)
    r = np.arange(tq)[:, None]
    c = np.arange(wk)[None, :]
    dchunk = (c - 2 * tq) // CHUNK - r // CHUNK
    in_band = (dchunk <= 0) & (dchunk >= -N_LEFT_CHUNKS)
    return jnp.where(jnp.asarray(in_band)[None], bias, NEG).astype(F32)


def _band_attn(proj, kt, rel_table, q_col, v_col):
    s = proj.shape[0]
    tq = BAND_TQ
    bias = _band_bias(rel_table, tq)

    def kspec(back):
        return pl.BlockSpec((W_B, tq), lambda i: (0, jnp.maximum(i - back, 0)))

    def vspec(back):
        return pl.BlockSpec((tq, W_B), lambda i: (jnp.maximum(i - back, 0), v_col))

    return pl.pallas_call(
        _band_kernel,
        grid=(s // tq,),
        in_specs=[pl.BlockSpec((tq, W_B), lambda i: (i, q_col)),
                  kspec(2), kspec(1), kspec(0), vspec(2), vspec(1), vspec(0),
                  pl.BlockSpec((H_B, tq, 3 * tq), lambda i: (0, 0, 0))],
        out_specs=pl.BlockSpec((tq, W_B), lambda i: (i, 0)),
        out_shape=jax.ShapeDtypeStruct((s, W_B), BF16),
        compiler_params=_params(("parallel",)),
        name="band_attn",
    )(proj, kt, kt, kt, proj, proj, proj, bias)


def _merge_kernel(ya_ref, yb_ref, ga_ref, gb_ref, wa_ref, wb_ref, o_ref):
    ua = jnp.dot(ya_ref[...], wa_ref[...], preferred_element_type=F32)
    ub = jnp.dot(yb_ref[...], wb_ref[...], preferred_element_type=F32)
    o_ref[...] = (jax.nn.sigmoid(ga_ref[...]) * ua + jax.nn.sigmoid(gb_ref[...]) * ub).astype(o_ref.dtype)


def _merge(ya, yb, gates, wa, wb, tm=256):
    s, d = ya.shape[0], wa.shape[1]
    return pl.pallas_call(
        _merge_kernel,
        grid=(s // tm,),
        in_specs=[pl.BlockSpec((tm, W_A), lambda i: (i, 0)),
                  pl.BlockSpec((tm, W_B), lambda i: (i, 0)),
                  pl.BlockSpec((tm, d), lambda i: (i, 0)),
                  pl.BlockSpec((tm, d), lambda i: (i, 1)),
                  pl.BlockSpec((W_A, d), lambda i: (0, 0)),
                  pl.BlockSpec((W_B, d), lambda i: (0, 0))],
        out_specs=pl.BlockSpec((tm, d), lambda i: (i, 0)),
        out_shape=jax.ShapeDtypeStruct((s, d), BF16),
        compiler_params=_params(("parallel",)),
        name="merge",
    )(ya, yb, gates, gates, wa, wb)


def _post_kernel(x_ref, m_ref, wo_ref, g1_ref, gn_ref, sc_ref, sh_ref, x1_ref, h2_ref):
    x1 = x_ref[...] + g1_ref[...] * jnp.dot(m_ref[...], wo_ref[...], preferred_element_type=F32)
    x1_ref[...] = x1
    y = x1 * lax.rsqrt(jnp.mean(x1 * x1, axis=-1, keepdims=True) + EPS)
    _store_rows(h2_ref, y * gn_ref[...] * (1.0 + sc_ref[...]) + sh_ref[...])


def _post(x2, merged, wo, g1, gn, sc, sh, tm=256):
    t, d = x2.shape
    row = pl.BlockSpec((1, d), lambda i: (0, 0))
    tile = pl.BlockSpec((tm, d), lambda i: (i, 0))
    return pl.pallas_call(
        _post_kernel,
        grid=(t // tm,),
        in_specs=[tile, tile, pl.BlockSpec((d, d), lambda i: (0, 0)), row, row, row, row],
        out_specs=[tile, pl.BlockSpec((tm * (d // LANES), LANES), lambda i: (i, 0))],
        out_shape=[jax.ShapeDtypeStruct((t, d), F32), jax.ShapeDtypeStruct((t * (d // LANES), LANES), F32)],
        compiler_params=_params(("parallel",)),
        name="post",
    )(x2, merged, wo, g1, gn, sc, sh)


def _router_kernel(h_ref, wr_ref, br_ref, lg_ref):
    lg_ref[...] = jnp.dot(_load_rows(h_ref, wr_ref.shape[0] // LANES), wr_ref[...], preferred_element_type=F32,
                          precision=lax.Precision.HIGHEST) + br_ref[...]


def _router(h2, w_router, b_router, tm=512):
    d, nr = w_router.shape
    t = h2.shape[0] // (d // LANES)
    return pl.pallas_call(
        _router_kernel,
        grid=(t // tm,),
        in_specs=[pl.BlockSpec((tm * (d // LANES), LANES), lambda i: (i, 0)),
                  pl.BlockSpec((d, nr), lambda i: (0, 0)), pl.BlockSpec((1, nr), lambda i: (0, 0))],
        out_specs=pl.BlockSpec((tm, nr), lambda i: (i, 0)),
        out_shape=jax.ShapeDtypeStruct((t, nr), F32),
        compiler_params=_params(("parallel",)),
        name="router",
    )(h2, w_router, b_router)


def _moe_kernel(tok_ref, dst_ref, be_ref, rows_ref, nu_ref,
                h_hbm, w1_ref, w3_ref, w2_ref, z_hbm,
                x0, x1, y0, y1, w1b, w3b, w2b, gsem, ssem):
    b = pl.program_id(0)
    c = w1b.shape[0] // LANES
    bm = x0.shape[0] // c
    nu = nu_ref[0]
    xs, ys = (x0, x1), (y0, y1)

    def rows_of(blk):
        return rows_ref[blk + 2]

    def gather_copy(sl, r, t):
        return pltpu.make_async_copy(h_hbm.at[pl.ds(t * c, c), :], xs[sl].at[pl.ds(r * c, c), :], gsem.at[sl])

    def scatter_copy(sl, r, d):
        return pltpu.make_async_copy(ys[sl].at[pl.ds(r * c, c), :], z_hbm.at[pl.ds(d * c, c), :], ssem.at[sl])

    def for_rows(blk, fn):
        def body(r, carry):
            fn(r)
            return carry
        lax.fori_loop(0, rows_of(blk), body, 0)

    def gather_start(blk, sl):
        for_rows(blk, lambda r: gather_copy(sl, r, tok_ref[blk * bm + r]).start())

    def gather_wait(blk, sl):
        for_rows(blk, lambda r: gather_copy(sl, r, 0).wait())

    def scatter_start(blk, sl):
        for_rows(blk, lambda r: scatter_copy(sl, r, dst_ref[jnp.maximum(blk, 0) * bm + r]).start())

    def scatter_wait(blk, sl):
        for_rows(blk, lambda r: scatter_copy(sl, r, 0).wait())

    @pl.when(b == 0)
    def _():
        x0[...] = jnp.zeros(x0.shape, F32)
        x1[...] = jnp.zeros(x1.shape, F32)
        gather_start(0, 0)

    @pl.when((b < nu) & ((b == 0) | (be_ref[b] != be_ref[jnp.maximum(b - 1, 0)])))
    def _():
        w1b[...] = w1_ref[0].astype(BF16)
        w3b[...] = w3_ref[0].astype(BF16)
        w2b[...] = w2_ref[0].astype(BF16)

    def main(sl):
        gather_start(b + 1, 1 - sl)
        scatter_start(b - 1, 1 - sl)
        gather_wait(b, sl)
        scatter_wait(b - 2, sl)
        x = _load_rows(xs[sl], c).astype(BF16)
        a1 = jnp.dot(x, w1b[...], preferred_element_type=F32)
        a3 = jnp.dot(x, w3b[...], preferred_element_type=F32)
        a = (a1 * jax.nn.sigmoid(a1)) * a3
        y = jnp.dot(a.astype(BF16), w2b[...], preferred_element_type=F32)
        _store_rows(ys[sl], y)

    def drain(sl):
        scatter_wait(b - 2, sl)
        scatter_start(b - 1, 1 - sl)
        scatter_wait(b - 1, 1 - sl)

    for sl in range(2):
        @pl.when((b < nu) & (b % 2 == sl))
        def _():
            main(sl)

        @pl.when((b == nu) & (b % 2 == sl))
        def _():
            drain(sl)


def _moe(h2, tok_buf, dst_buf, blk_e, blk_rows, n_used, w1, w3, w2):
    d, dff = w1.shape[1:]
    c = d // LANES
    t = h2.shape[0] // c
    bm = MOE_BM
    nsteps = blk_e.shape[0]
    assert tok_buf.shape[0] == nsteps * bm and blk_rows.shape[0] == nsteps + 2
    grid_spec = pltpu.PrefetchScalarGridSpec(
        num_scalar_prefetch=5,
        grid=(nsteps,),
        in_specs=[
            pl.BlockSpec(memory_space=pl.ANY),
            pl.BlockSpec((1, d, dff), lambda b, tok, dst, be, rows, nu: (be[b], 0, 0)),
            pl.BlockSpec((1, d, dff), lambda b, tok, dst, be, rows, nu: (be[b], 0, 0)),
            pl.BlockSpec((1, dff, d), lambda b, tok, dst, be, rows, nu: (be[b], 0, 0)),
        ],
        out_specs=pl.BlockSpec(memory_space=pl.ANY),
        scratch_shapes=[pltpu.VMEM((bm * c, LANES), F32), pltpu.VMEM((bm * c, LANES), F32),
                        pltpu.VMEM((bm * c, LANES), F32), pltpu.VMEM((bm * c, LANES), F32),
                        pltpu.VMEM((d, dff), BF16), pltpu.VMEM((d, dff), BF16), pltpu.VMEM((dff, d), BF16),
                        pltpu.SemaphoreType.DMA((2,)), pltpu.SemaphoreType.DMA((2,))],
    )
    return pl.pallas_call(
        _moe_kernel,
        grid_spec=grid_spec,
        out_shape=jax.ShapeDtypeStruct((TOPK_IN_GROUP * t * c, LANES), F32),
        compiler_params=pltpu.CompilerParams(dimension_semantics=("arbitrary",),
                                             vmem_limit_bytes=VMEM_LIMIT, has_side_effects=True),
        name="moe",
    )(tok_buf, dst_buf, blk_e, blk_rows, n_used, h2, w1, w3, w2)


def _final_kernel(x1_ref, z0_ref, z1_ref, w_ref, g2_ref, gn_ref, o_ref):
    c = x1_ref.shape[1] // LANES
    moe = w_ref[:, 0:1] * _load_rows(z0_ref, c) + w_ref[:, 1:2] * _load_rows(z1_ref, c)
    x2 = x1_ref[...] + g2_ref[...] * moe
    y = x2 * lax.rsqrt(jnp.mean(x2 * x2, axis=-1, keepdims=True) + EPS)
    o_ref[...] = y * gn_ref[...]


def _final(x1, z, weight, g2, gn, tm=256):
    t, d = x1.shape
    row = pl.BlockSpec((1, d), lambda i: (0, 0))
    return pl.pallas_call(
        _final_kernel,
        grid=(t // tm,),
        in_specs=[pl.BlockSpec((tm, d), lambda i: (i, 0)),
                  pl.BlockSpec((tm * (d // LANES), LANES), lambda i: (i, 0)),
                  pl.BlockSpec((tm * (d // LANES), LANES), lambda i: (t // tm + i, 0)),
                  pl.BlockSpec((tm, TOPK_IN_GROUP), lambda i: (i, 0)), row, row],
        out_specs=pl.BlockSpec((tm, d), lambda i: (i, 0)),
        out_shape=jax.ShapeDtypeStruct((t, d), F32),
        compiler_params=_params(("parallel",)),
        name="final",
    )(x1, z, z, weight, g2, gn)


def _route(logits, t):
    gl = logits[:, :N_GROUPS]
    el = logits[:, N_GROUPS:N_GROUPS + N_EXPERTS].reshape(t, N_GROUPS, EXP_PER_GROUP)
    g_prob = jax.nn.softmax(gl, axis=-1)
    grp = jnp.argmax(gl, axis=-1).astype(I32)
    p_grp = jnp.take_along_axis(g_prob, grp[:, None], axis=-1)[:, 0]
    e_in = jnp.take_along_axis(el, grp[:, None, None], axis=1)[:, 0]
    top_v, top_i = lax.top_k(e_in, TOPK_IN_GROUP)
    p_in = jax.nn.softmax(top_v, axis=-1)
    expert = grp[:, None] * EXP_PER_GROUP + top_i.astype(I32)
    weight = p_grp[:, None] * p_in

    bm = MOE_BM
    m = t * TOPK_IN_GROUP
    e_flat = expert.reshape(m)
    order = jnp.argsort(e_flat, stable=True).astype(I32)
    counts = jnp.bincount(e_flat, length=N_EXPERTS).astype(I32)
    start = jnp.cumsum(counts) - counts
    padded = ((counts + bm - 1) // bm) * bm
    pend = jnp.cumsum(padded)
    pstart = pend - padded
    nb = m // bm + N_EXPERTS + 1
    blk_e = jnp.minimum(jnp.searchsorted(pend, jnp.arange(nb, dtype=I32) * bm, side='right'),
                        N_EXPERTS - 1).astype(I32)
    n_used = (pend[-1] // bm).astype(I32).reshape(1)
    pos = jnp.arange(nb * bm, dtype=I32).reshape(nb, bm)
    src = (start[blk_e] - pstart[blk_e])[:, None] + pos
    valid = (src < (start + counts)[blk_e][:, None]) & (pos < pend[-1])
    a = order[jnp.clip(src, 0, m - 1)]
    tok = a // TOPK_IN_GROUP
    tok_buf = jnp.where(valid, tok, 0).reshape(-1)
    dst_buf = jnp.where(valid, (a % TOPK_IN_GROUP) * t + tok, 0).reshape(-1)
    blk_rows = jnp.concatenate([jnp.zeros((2,), I32), jnp.sum(valid, axis=1).astype(I32)])
    return tok_buf, dst_buf, blk_e, blk_rows, n_used, weight


def kernel(x, c, w_ada, b_ada, norm_mix, w_in, t5_table, rel_table, w_up_a, w_up_b, w_o, norm_ffn,
           w_rg, b_rg, w_re, b_re, w1, w3, w2, norm_final):
    bn, s, d = x.shape
    assert bn == 1 and w_ada.shape[0] == 1
    assert s % 1024 == 0
    t = bn * s
    x2 = x.reshape(t, d)
    n_sel = min(TOPK_MAX, s // 4)

    mod = _ada(c.reshape(d, 1), w_ada[0], b_ada[0].reshape(1, 6 * d))
    sh1, sc1, g1, sh2, sc2, g2 = [mod[:, i * d:(i + 1) * d] for i in range(6)]

    h = _rms_mod(x2, norm_mix[0].reshape(1, d), sc1, sh1, BF16)

    cols = np.cumsum([0, W_A, W_A, W_A, W_IDX_Q, D_IDX, H_IDX, W_B, W_B, W_B, d, d])
    wsl = [w_in[0][:, cols[i]:cols[i + 1]] for i in range(11)]
    wqa, wka, wva, wqi, wki, wwi, wqb, wkb, wvb, wga, wgb = wsl
    w_main = jnp.concatenate([wqa * (LOG2E / math.sqrt(HD_A)), wka, wva, wqi,
                              wqb * (LOG2E / math.sqrt(HD_B)), wkb, wvb], axis=1).astype(BF16)
    w_gate = jnp.concatenate([wga, wgb], axis=1).astype(BF16)
    w_idx = jnp.concatenate([wki, wwi, jnp.zeros((d, LANES - D_IDX - H_IDX), F32)], axis=1).astype(BF16)

    proj = _matmul(h, w_main, BF16, 1024, 1024, "proj_main")
    gates = _matmul(h, w_gate, F32, 1024, 1024, "proj_gate")
    idx = _matmul(h, w_idx, F32, 1024, LANES, "proj_idx")
    nkt = s // SEL_TK
    kit3 = jnp.transpose(idx[:, :D_IDX].astype(BF16).reshape(nkt, SEL_TK, D_IDX), (0, 2, 1))
    wi = idx[:, D_IDX:D_IDX + H_IDX] * ((H_IDX ** -0.5) * (D_IDX ** -0.5))
    kt_a = jnp.transpose(proj[:, W_A:2 * W_A])
    kt_b = jnp.transpose(proj[:, 5 * W_A:6 * W_A])

    mask4 = _select(proj, wi, kit3, n_sel, qi_col_block=3)
    y_a = _dsa_attn(proj, kt_a, mask4, t5_table, q_col=0, v_col=2)
    y_b = _band_attn(proj, kt_b, rel_table[0], q_col=4, v_col=6)

    merged = _merge(y_a, y_b, gates, w_up_a[0].astype(BF16), w_up_b[0].astype(BF16))

    nr = LANES
    w_router = jnp.concatenate([w_rg[0], w_re[0], jnp.zeros((d, nr - N_GROUPS - N_EXPERTS), F32)], axis=1)
    b_router = jnp.concatenate([b_rg[0], b_re[0], jnp.zeros((nr - N_GROUPS - N_EXPERTS,), F32)]).reshape(1, nr)
    x1, h2 = _post(x2, merged, w_o[0].astype(BF16), g1, norm_ffn[0].reshape(1, d), sc2, sh2)
    logits = _router(h2, w_router, b_router)
    tok_buf, dst_buf, blk_e, blk_rows, n_used, weight = _route(logits, t)
    z = _moe(h2, tok_buf, dst_buf, blk_e, blk_rows, n_used, w1[0], w3[0], w2[0])
    out = _final(x1, z, weight, g2, norm_final.reshape(1, d))
    return out.reshape(bn, s, d)
```

```python
import functools
import math

import numpy as np
import jax
import jax.numpy as jnp
from jax import lax
from jax.experimental import pallas as pl
from jax.experimental.pallas import tpu as pltpu

F32 = jnp.float32
BF16 = jnp.bfloat16
I32 = jnp.int32

CHUNK = 64
EPS = 1e-6
H_A, HD_A = 8, 128
H_IDX, D_IDX = 16, 64
TOPK_MAX = 256
N_T5_BUCKETS = 32
T5_MAX_DIST = 1024
H_B, HD_B = 8, 128
N_LEFT_CHUNKS = 8
REL_CLIP = 128
N_GROUPS = 8
EXP_PER_GROUP = 8
N_EXPERTS = N_GROUPS * EXP_PER_GROUP
TOPK_IN_GROUP = 2

W_A = H_A * HD_A
W_B = H_B * HD_B
W_IDX_Q = H_IDX * D_IDX

NEG = -1e30
INT_MIN = -(2 ** 31)
KEY_FMAX = 0x7F7FFFFF
LOG2E = math.log2(math.e)

LANES = 128
VMEM_LIMIT = 56 * 1024 * 1024

SEL_TQ = 128
SEL_TK = 512
ATT_TQ = 256
ATT_TK = 1024
BAND_TQ = 256
MOE_BM = 256


def _params(sem, vmem=VMEM_LIMIT):
    return pltpu.CompilerParams(dimension_semantics=sem, vmem_limit_bytes=vmem)


def _load_rows(ref, c):
    rows = ref.shape[0] // c
    return jnp.concatenate([ref[pl.ds(k, rows, stride=c), :] for k in range(c)], axis=1)


def _store_rows(ref, val):
    rows = val.shape[0]
    c = ref.shape[0] // rows
    for k in range(c):
        ref[pl.ds(k, rows, stride=c), :] = val[:, k * LANES:(k + 1) * LANES]


def _ada_kernel(c_ref, w_ref, b_ref, o_ref, *, kc):
    d = w_ref.shape[0]
    tn = w_ref.shape[1]

    def body(k, acc):
        r0 = pl.multiple_of(k * kc, kc)
        cc = c_ref[pl.ds(r0, kc), :]
        ca = cc * jax.nn.sigmoid(cc)
        return acc + jnp.sum(w_ref[pl.ds(r0, kc), :] * ca, axis=0, keepdims=True)

    acc = lax.fori_loop(0, d // kc, body, jnp.zeros((1, tn), F32))
    o_ref[...] = acc + b_ref[...]


def _ada(c_col, w, b_row, tn=1024, kc=256):
    d, n = w.shape
    return pl.pallas_call(
        functools.partial(_ada_kernel, kc=kc),
        grid=(n // tn,),
        in_specs=[pl.BlockSpec((d, 1), lambda j: (0, 0)),
                  pl.BlockSpec((d, tn), lambda j: (0, j)),
                  pl.BlockSpec((1, tn), lambda j: (0, j))],
        out_specs=pl.BlockSpec((1, tn), lambda j: (0, j)),
        out_shape=jax.ShapeDtypeStruct((1, n), F32),
        compiler_params=_params(("arbitrary",)),
        name="ada",
    )(c_col, w, b_row)


def _rms_mod_kernel(x_ref, g_ref, sc_ref, sh_ref, o_ref):
    x = x_ref[...]
    y = x * lax.rsqrt(jnp.mean(x * x, axis=-1, keepdims=True) + EPS)
    o_ref[...] = (y * g_ref[...] * (1.0 + sc_ref[...]) + sh_ref[...]).astype(o_ref.dtype)


def _rms_mod(x2, g, sc, sh, out_dtype, tm=512):
    t, d = x2.shape
    row = pl.BlockSpec((1, d), lambda i: (0, 0))
    return pl.pallas_call(
        _rms_mod_kernel,
        grid=(t // tm,),
        in_specs=[pl.BlockSpec((tm, d), lambda i: (i, 0)), row, row, row],
        out_specs=pl.BlockSpec((tm, d), lambda i: (i, 0)),
        out_shape=jax.ShapeDtypeStruct((t, d), out_dtype),
        compiler_params=_params(("parallel",)),
        name="rms_mod",
    )(x2, g, sc, sh)


def _mm_kernel(a_ref, b_ref, o_ref):
    o_ref[...] = jnp.dot(a_ref[...], b_ref[...], preferred_element_type=F32).astype(o_ref.dtype)


def _matmul(a, b, out_dtype, tm, tn, name):
    m, k = a.shape
    n = b.shape[1]
    return pl.pallas_call(
        _mm_kernel,
        grid=(m // tm, n // tn),
        in_specs=[pl.BlockSpec((tm, k), lambda i, j: (i, 0)),
                  pl.BlockSpec((k, tn), lambda i, j: (0, j))],
        out_specs=pl.BlockSpec((tm, tn), lambda i, j: (i, j)),
        out_shape=jax.ShapeDtypeStruct((m, n), out_dtype),
        compiler_params=_params(("parallel", "arbitrary")),
        name=name,
    )(a, b)


def _sortable(x):
    bits = pltpu.bitcast(x, I32)
    return bits ^ ((bits >> 31) & 0x7FFFFFFF)


def _unsortable(k):
    return pltpu.bitcast(k ^ ((k >> 31) & 0x7FFFFFFF), F32)


def _select_kernel(qi_ref, wi_ref, kit_ref, mask_ref, key_sc, qh_sc, wb_sc, *, n_sel):
    nkt, tq, tk = key_sc.shape
    nsub = tk // LANES
    ngrp = -(-n_sel // LANES)
    i = pl.program_id(0)
    q0 = i * tq
    nvis = (q0 + tq + tk - 1) // tk

    for h in range(H_IDX):
        qh_sc[h * tq:(h + 1) * tq, :] = qi_ref[:, h * D_IDX:(h + 1) * D_IDX]
        wb_sc[h] = jnp.broadcast_to(wi_ref[:, h:h + 1], (tq, LANES))

    row = lax.broadcasted_iota(I32, (tq, 1), 0) + q0
    limit = (row // CHUNK + 1) * CHUNK
    lane = lax.broadcasted_iota(I32, (tq, LANES), 1)

    def score_tile(kt, gmax):
        k_t = kit_ref[kt]
        accs = [jnp.zeros((tq, LANES), F32) for _ in range(nsub)]
        s_all = jnp.dot(qh_sc[...], k_t, preferred_element_type=F32)
        for h in range(H_IDX):
            s = s_all[h * tq:(h + 1) * tq, :]
            wb = wb_sc[h]
            for a in range(nsub):
                accs[a] = accs[a] + wb * jnp.maximum(s[:, a * LANES:(a + 1) * LANES], 0.0)
        gmax = list(gmax)
        for a in range(nsub):
            col = lane + (kt * tk + a * LANES)
            key = jnp.where(col < limit, _sortable(accs[a]), INT_MIN)
            key_sc[kt, :, a * LANES:(a + 1) * LANES] = key
            gmax[a % ngrp] = jnp.maximum(gmax[a % ngrp], key)
        return tuple(gmax)

    gmax = lax.fori_loop(0, nvis, score_tile,
                         tuple(jnp.full((tq, LANES), INT_MIN, I32) for _ in range(ngrp)))

    def count(pred):
        def body(kt, c):
            for a in range(nsub):
                kk = key_sc[kt, :, a * LANES:(a + 1) * LANES]
                c = c + pred(kk, lane + (kt * tk + a * LANES)).astype(I32)
            return c
        c = lax.fori_loop(0, nvis, body, jnp.zeros((tq, LANES), I32))
        return jnp.sum(c, axis=1, keepdims=True)

    gmin, ghi = gmax[0], gmax[0]
    for g in gmax[1:]:
        gmin = jnp.minimum(gmin, g)
        ghi = jnp.maximum(ghi, g)
    lo0 = jnp.maximum(jnp.min(gmin, axis=1, keepdims=True), INT_MIN + 1)
    hi0 = jnp.max(ghi, axis=1, keepdims=True)
    few = limit < n_sel
    lo0 = jnp.where(few, INT_MIN + 1, lo0)
    hi0 = jnp.where(few, INT_MIN + 1, hi0)
    unknown = jnp.full((tq, 1), 2 ** 30, I32)

    def bis_cond(st):
        lo, hi, _, _ = st
        return jnp.max((lo < hi).astype(I32)) > 0

    def bis_body(st):
        lo, hi, c_lo, c_hi1 = st
        active = lo < hi
        mid = (lo | hi) - ((lo ^ hi) >> 1)
        fin = lambda k: jnp.clip(k, -KEY_FMAX - 1, KEY_FMAX)
        vmid = _sortable(0.5 * _unsortable(fin(lo)) + 0.5 * _unsortable(fin(hi)))
        mid = jnp.where((vmid > lo) & (vmid <= hi), vmid, mid)
        c = count(lambda kk, col: kk >= mid)
        up = active & (c >= n_sel)
        dn = active & (c < n_sel)
        hit = active & (c == n_sel)
        lo = jnp.where(up, mid, lo)
        c_lo = jnp.where(up, c, c_lo)
        hi = jnp.where(dn, mid - 1, jnp.where(hit, mid, hi))
        c_hi1 = jnp.where(dn, c, c_hi1)
        return lo, hi, c_lo, c_hi1

    thr, _, n_ge, n_gt = lax.while_loop(bis_cond, bis_body,
                                        (lo0, hi0, unknown, jnp.zeros((tq, 1), I32)))

    excess = (n_ge > n_sel) & jnp.logical_not(few)
    need = n_sel - n_gt
    ncol = nkt * tk

    def tie_cut():
        nbits = max(1, int(math.ceil(math.log2(ncol))))

        def cut_body(b, p):
            cand = p | jnp.left_shift(jnp.int32(1), nbits - 1 - b)
            cnt = count(lambda kk, col: (kk == thr) & (col < cand))
            return jnp.where(cnt < need, cand, p)

        p = lax.fori_loop(0, nbits, cut_body, jnp.zeros((tq, 1), I32))
        return p + 1

    cut = lax.cond(jnp.max(excess.astype(I32)) > 0, tie_cut, lambda: jnp.full((tq, 1), ncol, I32))
    cut = jnp.where(excess, cut, ncol)

    def write_tile(kt, carry):
        kk = key_sc[kt]
        col = lax.broadcasted_iota(I32, (tq, tk), 1) + kt * tk
        sel = (kk > thr) | ((kk == thr) & (col < cut))
        mask_ref[kt] = jnp.where(sel, 0.0, NEG).astype(mask_ref.dtype)
        return carry

    lax.fori_loop(0, nvis, write_tile, 0)

    def fill_tile(kt, carry):
        mask_ref[kt] = jnp.full((tq, tk), NEG, mask_ref.dtype)
        return carry

    lax.fori_loop(nvis, nkt, fill_tile, 0)


def _select(proj, wi, kit3, n_sel, qi_col_block):
    s = proj.shape[0]
    nkt, _, tk = kit3.shape
    tq = SEL_TQ
    nqb = s // tq
    assert -(-n_sel // LANES) <= tk // LANES
    return pl.pallas_call(
        functools.partial(_select_kernel, n_sel=n_sel),
        grid=(nqb,),
        in_specs=[pl.BlockSpec((tq, W_IDX_Q), lambda i: (i, qi_col_block)),
                  pl.BlockSpec((tq, H_IDX), lambda i: (i, 0)),
                  pl.BlockSpec((nkt, D_IDX, tk), lambda i: (0, 0, 0))],
        out_specs=pl.BlockSpec((None, nkt, tq, tk), lambda i: (i, 0, 0, 0)),
        out_shape=jax.ShapeDtypeStruct((nqb, nkt, tq, tk), BF16),
        scratch_shapes=[pltpu.VMEM((nkt, tq, tk), I32),
                        pltpu.VMEM((H_IDX * tq, D_IDX), BF16),
                        pltpu.VMEM((H_IDX, tq, LANES), F32)],
        compiler_params=_params(("arbitrary",)),
        name="dsa_select",
    )(proj, wi, kit3)


def _dsa_attn_kernel(qb_ref, kb_ref, first_ref, last_ref, slab_ref,
                     q_ref, kt_ref, v_ref, mask_ref, gen_ref, far_ref, o_ref,
                     m_sc, l_sc, acc_sc, band_ref):
    step = pl.program_id(0)
    tq = q_ref.shape[0]
    tk = kt_ref.shape[1]
    nslab = tk // LANES
    slab0 = slab_ref[step]

    @pl.when(step == 0)
    def _():
        win = LANES + tq
        for h in range(H_A):
            for u in range(band_ref.shape[1]):
                rows = jnp.broadcast_to(gen_ref[h:h + 1, u * LANES:u * LANES + win], (tq, win))
                rows = pltpu.roll(rows, win - (tq - 1), 1, stride=1, stride_axis=0)
                band_ref[h, u] = rows[:, 0:LANES]

    @pl.when(first_ref[step] == 1)
    def _():
        m_sc[...] = jnp.full(m_sc.shape, NEG, F32)
        l_sc[...] = jnp.zeros(l_sc.shape, F32)
        acc_sc[...] = jnp.zeros(acc_sc.shape, F32)

    def tile(near):
        mask = jnp.concatenate(
            [jnp.concatenate([mask_ref[r, j] for j in range(mask_ref.shape[1])], axis=1)
             for r in range(mask_ref.shape[0])], axis=0).astype(F32)
        ones = jnp.ones((tk, HD_A), BF16)
        for h in range(H_A):
            hs = slice(h * HD_A, (h + 1) * HD_A)
            s = jnp.dot(q_ref[:, hs], kt_ref[hs, :], preferred_element_type=F32) + mask
            if near:
                s = s + jnp.concatenate([band_ref[h, slab0 + a] for a in range(nslab)], axis=1)
                shift = 0.0
            else:
                shift = far_ref[h]
            smax = s[:, 0:LANES]
            for a in range(1, nslab):
                smax = jnp.maximum(smax, s[:, a * LANES:(a + 1) * LANES])
            m_old = m_sc[h]
            m_new = jnp.maximum(m_old, jnp.max(smax, axis=1, keepdims=True) + shift)
            alpha = jnp.exp2(m_old - m_new)
            mm = m_new - shift
            p = jnp.concatenate([jnp.exp2(s[:, a * LANES:(a + 1) * LANES] - mm) for a in range(nslab)],
                                axis=1).astype(BF16)
            pv = jnp.dot(p, jnp.concatenate([v_ref[:, hs], ones], axis=1), preferred_element_type=F32)
            acc_sc[:, hs] = alpha * acc_sc[:, hs] + pv[:, 0:HD_A]
            l_sc[h] = alpha * l_sc[h] + pv[:, HD_A:2 * HD_A]
            m_sc[h] = m_new

    @pl.when(slab0 >= 0)
    def _():
        tile(True)

    @pl.when(slab0 < 0)
    def _():
        tile(False)

    @pl.when(last_ref[step] == 1)
    def _():
        for h in range(H_A):
            hs = slice(h * HD_A, (h + 1) * HD_A)
            o_ref[:, hs] = (acc_sc[:, hs] / l_sc[h]).astype(o_ref.dtype)


def _t5_bucket_np(rel):
    nb = N_T5_BUCKETS // 2
    ret = (rel > 0).astype(np.int32) * nb
    n = np.abs(rel)
    max_exact = nb // 2
    nf = np.maximum(n, 1).astype(np.float32)
    large = max_exact + (np.log(nf / np.float32(max_exact)) / np.float32(math.log(T5_MAX_DIST / max_exact))
                         * np.float32(nb - max_exact)).astype(np.int32)
    large = np.minimum(large, nb - 1)
    return ret + np.where(n < max_exact, n, large)


def _dsa_plan(s, tq, tk):
    nqb = s // tq
    d_all = np.arange(-(s - 1), CHUNK, dtype=np.int64)
    b_all = _t5_bucket_np(d_all)
    far_bucket = int(b_all[0])
    varying = np.nonzero(b_all != far_bucket)[0]
    d_lo = int(d_all[varying[0]])
    o_min = min(int(math.ceil((d_lo - (tk - 1)) / LANES)) * LANES, 0)
    n_slabs = (-o_min + tk) // LANES
    qb, kb, first, last, slab = [], [], [], [], []
    for i in range(nqb):
        q0 = i * tq
        nvis = (q0 + tq + tk - 1) // tk
        for j in range(nvis):
            o = j * tk - q0
            qb.append(i)
            kb.append(j)
            first.append(1 if j == 0 else 0)
            last.append(1 if j == nvis - 1 else 0)
            slab.append(-1 if o < o_min else (o - o_min) // LANES)
    u = np.arange(n_slabs * LANES + tq - 1)
    g_bucket = _t5_bucket_np(np.clip(o_min + u - (tq - 1), -(s - 1), None))
    plan = tuple(np.asarray(a, np.int32) for a in (qb, kb, first, last, slab))
    return plan, g_bucket, far_bucket, n_slabs


def _dsa_attn(proj, kt, mask4, t5_table, q_col, v_col):
    s = proj.shape[0]
    tq, tk = ATT_TQ, ATT_TK
    (qb, kb, first, last, slab), g_bucket, far_bucket, n_slabs = _dsa_plan(s, tq, tk)
    nsteps = len(qb)
    t5l = t5_table * LOG2E
    gen = jnp.transpose(t5l[g_bucket], (1, 0))
    gen = jnp.pad(gen, ((0, 0), (0, n_slabs * LANES + tq - gen.shape[1])))
    far = t5l[far_bucket]
    rq = tq // SEL_TQ
    grid_spec = pltpu.PrefetchScalarGridSpec(
        num_scalar_prefetch=5,
        grid=(nsteps,),
        in_specs=[
            pl.BlockSpec((tq, W_A), lambda t, qb, kb, f, l, sl: (qb[t], q_col)),
            pl.BlockSpec((W_A, tk), lambda t, qb, kb, f, l, sl: (0, kb[t])),
            pl.BlockSpec((tk, W_A), lambda t, qb, kb, f, l, sl: (kb[t], v_col)),
            pl.BlockSpec((rq, tk // SEL_TK, SEL_TQ, SEL_TK), lambda t, qb, kb, f, l, sl: (qb[t], kb[t], 0, 0)),
            pl.BlockSpec((H_A, n_slabs * LANES + tq), lambda t, qb, kb, f, l, sl: (0, 0)),
            pl.BlockSpec(memory_space=pltpu.SMEM),
        ],
        out_specs=pl.BlockSpec((tq, W_A), lambda t, qb, kb, f, l, sl: (qb[t], 0)),
        scratch_shapes=[pltpu.VMEM((H_A, tq, LANES), F32),
                        pltpu.VMEM((H_A, tq, LANES), F32),
                        pltpu.VMEM((tq, W_A), F32),
                        pltpu.VMEM((H_A, n_slabs, tq, LANES), F32)],
    )
    return pl.pallas_call(
        _dsa_attn_kernel,
        grid_spec=grid_spec,
        out_shape=jax.ShapeDtypeStruct((s, W_A), BF16),
        compiler_params=_params(("arbitrary",)),
        name="dsa_attn",
    )(jnp.asarray(qb), jnp.asarray(kb), jnp.asarray(first), jnp.asarray(last), jnp.asarray(slab),
      proj, kt, proj, mask4, gen, far)


def _band_kernel(q_ref, k0_ref, k1_ref, k2_ref, v0_ref, v1_ref, v2_ref, gen_ref, o_ref, bias_ref):
    i = pl.program_id(0)
    tq = q_ref.shape[0]
    wk = 3 * tq

    @pl.when(i == 0)
    def _():
        r = lax.broadcasted_iota(I32, (tq, wk), 0)
        c = lax.broadcasted_iota(I32, (tq, wk), 1)
        dchunk = (c // CHUNK - 2 * tq // CHUNK) - r // CHUNK
        in_band = (dchunk <= 0) & (dchunk >= -N_LEFT_CHUNKS)
        win = wk + tq
        for h in range(H_B):
            rows = jnp.broadcast_to(gen_ref[h:h + 1, :], (tq, win))
            rows = pltpu.roll(rows, win - (tq - 1), 1, stride=1, stride_axis=0)
            bias_ref[h] = jnp.where(in_band, rows[:, 0:wk], NEG)
    col = lax.broadcasted_iota(I32, (tq, wk), 1)
    start_mask = jnp.where(col + (i - 2) * tq >= 0, 0.0, NEG)
    for h in range(H_B):
        hs = slice(h * HD_B, (h + 1) * HD_B)
        kc = jnp.concatenate([k0_ref[hs, :], k1_ref[hs, :], k2_ref[hs, :]], axis=1)
        vc = jnp.concatenate([v0_ref[:, hs], v1_ref[:, hs], v2_ref[:, hs]], axis=0)
        s = jnp.dot(q_ref[:, hs], kc, preferred_element_type=F32) + bias_ref[h] + start_mask
        m = jnp.max(s, axis=1, keepdims=True)
        p = jnp.exp2(s - m)
        l = jnp.sum(p, axis=1, keepdims=True)
        pv = jnp.dot(p.astype(BF16), vc, preferred_element_type=F32)
        o_ref[:, hs] = (pv / l).astype(o_ref.dtype)


def _band_gen(rel_table, tq):
    assert 2 * tq >= N_LEFT_CHUNKS * CHUNK and tq % CHUNK == 0
    x = np.arange(4 * tq)
    idx = np.clip(2 * tq + (tq - 1) - x, -REL_CLIP, REL_CLIP) + REL_CLIP
    return rel_table[:, idx] * LOG2E


def _band_attn(proj, kt, rel_table, q_col, v_col):
    s = proj.shape[0]
    tq = BAND_TQ
    gen = _band_gen(rel_table, tq)

    def kspec(back):
        return pl.BlockSpec((W_B, tq), lambda i: (0, jnp.maximum(i - back, 0)))

    def vspec(back):
        return pl.BlockSpec((tq, W_B), lambda i: (jnp.maximum(i - back, 0), v_col))

    return pl.pallas_call(
        _band_kernel,
        grid=(s // tq,),
        in_specs=[pl.BlockSpec((tq, W_B), lambda i: (i, q_col)),
                  kspec(2), kspec(1), kspec(0), vspec(2), vspec(1), vspec(0),
                  pl.BlockSpec((H_B, 4 * tq), lambda i: (0, 0))],
        out_specs=pl.BlockSpec((tq, W_B), lambda i: (i, 0)),
        out_shape=jax.ShapeDtypeStruct((s, W_B), BF16),
        scratch_shapes=[pltpu.VMEM((H_B, tq, 3 * tq), F32)],
        compiler_params=_params(("arbitrary",)),
        name="band_attn",
    )(proj, kt, kt, kt, proj, proj, proj, gen)


def _merge_kernel(ya_ref, yb_ref, ga_ref, gb_ref, wa_ref, wb_ref, o_ref):
    ua = jnp.dot(ya_ref[...], wa_ref[...], preferred_element_type=F32)
    ub = jnp.dot(yb_ref[...], wb_ref[...], preferred_element_type=F32)
    o_ref[...] = (jax.nn.sigmoid(ga_ref[...]) * ua + jax.nn.sigmoid(gb_ref[...]) * ub).astype(o_ref.dtype)


def _merge(ya, yb, gates, wa, wb, tm=256):
    s, d = ya.shape[0], wa.shape[1]
    return pl.pallas_call(
        _merge_kernel,
        grid=(s // tm,),
        in_specs=[pl.BlockSpec((tm, W_A), lambda i: (i, 0)),
                  pl.BlockSpec((tm, W_B), lambda i: (i, 0)),
                  pl.BlockSpec((tm, d), lambda i: (i, 0)),
                  pl.BlockSpec((tm, d), lambda i: (i, 1)),
                  pl.BlockSpec((W_A, d), lambda i: (0, 0)),
                  pl.BlockSpec((W_B, d), lambda i: (0, 0))],
        out_specs=pl.BlockSpec((tm, d), lambda i: (i, 0)),
        out_shape=jax.ShapeDtypeStruct((s, d), BF16),
        compiler_params=_params(("parallel",)),
        name="merge",
    )(ya, yb, gates, gates, wa, wb)


def _post_kernel(x_ref, m_ref, wo_ref, g1_ref, gn_ref, sc_ref, sh_ref, x1_ref, h2_ref):
    x1 = x_ref[...] + g1_ref[...] * jnp.dot(m_ref[...], wo_ref[...], preferred_element_type=F32)
    x1_ref[...] = x1
    y = x1 * lax.rsqrt(jnp.mean(x1 * x1, axis=-1, keepdims=True) + EPS)
    _store_rows(h2_ref, y * gn_ref[...] * (1.0 + sc_ref[...]) + sh_ref[...])


def _post(x2, merged, wo, g1, gn, sc, sh, tm=256):
    t, d = x2.shape
    row = pl.BlockSpec((1, d), lambda i: (0, 0))
    tile = pl.BlockSpec((tm, d), lambda i: (i, 0))
    return pl.pallas_call(
        _post_kernel,
        grid=(t // tm,),
        in_specs=[tile, tile, pl.BlockSpec((d, d), lambda i: (0, 0)), row, row, row, row],
        out_specs=[tile, pl.BlockSpec((tm * (d // LANES), LANES), lambda i: (i, 0))],
        out_shape=[jax.ShapeDtypeStruct((t, d), F32), jax.ShapeDtypeStruct((t * (d // LANES), LANES), F32)],
        compiler_params=_params(("parallel",)),
        name="post",
    )(x2, merged, wo, g1, gn, sc, sh)


def _router_kernel(h_ref, wr_ref, br_ref, lg_ref):
    lg_ref[...] = jnp.dot(_load_rows(h_ref, wr_ref.shape[0] // LANES), wr_ref[...], preferred_element_type=F32,
                          precision=lax.Precision.HIGHEST) + br_ref[...]


def _router(h2, w_router, b_router, tm=512):
    d, nr = w_router.shape
    t = h2.shape[0] // (d // LANES)
    return pl.pallas_call(
        _router_kernel,
        grid=(t // tm,),
        in_specs=[pl.BlockSpec((tm * (d // LANES), LANES), lambda i: (i, 0)),
                  pl.BlockSpec((d, nr), lambda i: (0, 0)), pl.BlockSpec((1, nr), lambda i: (0, 0))],
        out_specs=pl.BlockSpec((tm, nr), lambda i: (i, 0)),
        out_shape=jax.ShapeDtypeStruct((t, nr), F32),
        compiler_params=_params(("parallel",)),
        name="router",
    )(h2, w_router, b_router)


def _moe_kernel(tok_ref, dst_ref, be_ref, nu_ref,
                h_hbm, w1_ref, w3_ref, w2_ref, z_hbm,
                x0, x1, y0, y1, w1b, w3b, w2b, gsem, ssem):
    b = pl.program_id(0)
    c = w1b.shape[0] // LANES
    bm = x0.shape[0] // c
    nu = nu_ref[0]
    xs, ys = (x0, x1), (y0, y1)

    def gather_copy(sl, r, t):
        return pltpu.make_async_copy(h_hbm.at[pl.ds(t * c, c), :], xs[sl].at[pl.ds(r * c, c), :], gsem.at[sl])

    def scatter_copy(sl, r, d):
        return pltpu.make_async_copy(ys[sl].at[pl.ds(r * c, c), :], z_hbm.at[pl.ds(d * c, c), :], ssem.at[sl])

    def gather_start(blk, sl):
        for r in range(bm):
            gather_copy(sl, r, tok_ref[blk * bm + r]).start()

    def scatter_start(blk, sl):
        for r in range(bm):
            scatter_copy(sl, r, dst_ref[(blk + 2) * bm + r]).start()

    def gather_wait(sl):
        for r in range(bm):
            gather_copy(sl, r, 0).wait()

    def scatter_wait(sl):
        for r in range(bm):
            scatter_copy(sl, r, 0).wait()

    @pl.when(b == 0)
    def _():
        y0[...] = jnp.zeros(y0.shape, F32)
        y1[...] = jnp.zeros(y1.shape, F32)
        scatter_start(-2, 0)
        gather_start(0, 0)

    @pl.when((b < nu) & ((b == 0) | (be_ref[b] != be_ref[jnp.maximum(b - 1, 0)])))
    def _():
        w1b[...] = w1_ref[0].astype(BF16)
        w3b[...] = w3_ref[0].astype(BF16)
        w2b[...] = w2_ref[0].astype(BF16)

    def main(sl):
        gather_wait(sl)
        scatter_wait(sl)
        gather_start(b + 1, 1 - sl)
        scatter_start(b - 1, 1 - sl)
        x = _load_rows(xs[sl], c).astype(BF16)
        a1 = jnp.dot(x, w1b[...], preferred_element_type=F32)
        a3 = jnp.dot(x, w3b[...], preferred_element_type=F32)
        a = (a1 * jax.nn.sigmoid(a1)) * a3
        y = jnp.dot(a.astype(BF16), w2b[...], preferred_element_type=F32)
        _store_rows(ys[sl], y)

    def drain(sl):
        gather_wait(sl)
        scatter_wait(sl)
        scatter_start(b - 1, 1 - sl)
        scatter_wait(1 - sl)

    for sl in range(2):
        @pl.when((b < nu) & (b % 2 == sl))
        def _():
            main(sl)

        @pl.when((b == nu) & (b % 2 == sl))
        def _():
            drain(sl)


def _moe(h2, tok_buf, dst_buf, blk_e, n_used, w1, w3, w2):
    d, dff = w1.shape[1:]
    c = d // LANES
    t = h2.shape[0] // c
    bm = MOE_BM
    nsteps = blk_e.shape[0]
    assert tok_buf.shape[0] == nsteps * bm and dst_buf.shape[0] == (nsteps + 2) * bm
    grid_spec = pltpu.PrefetchScalarGridSpec(
        num_scalar_prefetch=4,
        grid=(nsteps,),
        in_specs=[
            pl.BlockSpec(memory_space=pl.ANY),
            pl.BlockSpec((1, d, dff), lambda b, tok, dst, be, nu: (be[b], 0, 0)),
            pl.BlockSpec((1, d, dff), lambda b, tok, dst, be, nu: (be[b], 0, 0)),
            pl.BlockSpec((1, dff, d), lambda b, tok, dst, be, nu: (be[b], 0, 0)),
        ],
        out_specs=pl.BlockSpec(memory_space=pl.ANY),
        scratch_shapes=[pltpu.VMEM((bm * c, LANES), F32), pltpu.VMEM((bm * c, LANES), F32),
                        pltpu.VMEM((bm * c, LANES), F32), pltpu.VMEM((bm * c, LANES), F32),
                        pltpu.VMEM((d, dff), BF16), pltpu.VMEM((d, dff), BF16), pltpu.VMEM((dff, d), BF16),
                        pltpu.SemaphoreType.DMA((2,)), pltpu.SemaphoreType.DMA((2,))],
    )
    return pl.pallas_call(
        _moe_kernel,
        grid_spec=grid_spec,
        out_shape=jax.ShapeDtypeStruct(((TOPK_IN_GROUP * t + 2 * bm) * c, LANES), F32),
        compiler_params=pltpu.CompilerParams(dimension_semantics=("arbitrary",),
                                             vmem_limit_bytes=VMEM_LIMIT, has_side_effects=True),
        name="moe",
    )(tok_buf, dst_buf, blk_e, n_used, h2, w1, w3, w2)


def _final_kernel(x1_ref, z0_ref, z1_ref, w_ref, g2_ref, gn_ref, o_ref):
    c = x1_ref.shape[1] // LANES
    moe = w_ref[:, 0:1] * _load_rows(z0_ref, c) + w_ref[:, 1:2] * _load_rows(z1_ref, c)
    x2 = x1_ref[...] + g2_ref[...] * moe
    y = x2 * lax.rsqrt(jnp.mean(x2 * x2, axis=-1, keepdims=True) + EPS)
    o_ref[...] = y * gn_ref[...]


def _final(x1, z, weight, g2, gn, tm=256):
    t, d = x1.shape
    row = pl.BlockSpec((1, d), lambda i: (0, 0))
    return pl.pallas_call(
        _final_kernel,
        grid=(t // tm,),
        in_specs=[pl.BlockSpec((tm, d), lambda i: (i, 0)),
                  pl.BlockSpec((tm * (d // LANES), LANES), lambda i: (i, 0)),
                  pl.BlockSpec((tm * (d // LANES), LANES), lambda i: (t // tm + i, 0)),
                  pl.BlockSpec((tm, TOPK_IN_GROUP), lambda i: (i, 0)), row, row],
        out_specs=pl.BlockSpec((tm, d), lambda i: (i, 0)),
        out_shape=jax.ShapeDtypeStruct((t, d), F32),
        compiler_params=_params(("parallel",)),
        name="final",
    )(x1, z, z, weight, g2, gn)


def _route(logits, t):
    gl = logits[:, :N_GROUPS]
    el = logits[:, N_GROUPS:N_GROUPS + N_EXPERTS].reshape(t, N_GROUPS, EXP_PER_GROUP)
    g_prob = jax.nn.softmax(gl, axis=-1)
    grp = jnp.argmax(gl, axis=-1).astype(I32)
    p_grp = jnp.take_along_axis(g_prob, grp[:, None], axis=-1)[:, 0]
    e_in = jnp.take_along_axis(el, grp[:, None, None], axis=1)[:, 0]
    top_v, top_i = lax.top_k(e_in, TOPK_IN_GROUP)
    p_in = jax.nn.softmax(top_v, axis=-1)
    expert = grp[:, None] * EXP_PER_GROUP + top_i.astype(I32)
    weight = p_grp[:, None] * p_in

    bm = MOE_BM
    m = t * TOPK_IN_GROUP
    e_flat = expert.reshape(m)
    order = jnp.argsort(e_flat, stable=True).astype(I32)
    counts = jnp.bincount(e_flat, length=N_EXPERTS).astype(I32)
    start = jnp.cumsum(counts) - counts
    padded = ((counts + bm - 1) // bm) * bm
    pend = jnp.cumsum(padded)
    pstart = pend - padded
    nb = m // bm + N_EXPERTS + 1
    blk_e = jnp.minimum(jnp.searchsorted(pend, jnp.arange(nb, dtype=I32) * bm, side='right'),
                        N_EXPERTS - 1).astype(I32)
    n_used = (pend[-1] // bm).astype(I32).reshape(1)
    pos = jnp.arange(nb * bm, dtype=I32).reshape(nb, bm)
    src = (start[blk_e] - pstart[blk_e])[:, None] + pos
    valid = (src < (start + counts)[blk_e][:, None]) & (pos < pend[-1])
    a = order[jnp.clip(src, 0, m - 1)]
    tok = a // TOPK_IN_GROUP
    tok_buf = jnp.where(valid, tok, 0).reshape(-1)
    spare = m + pos % (2 * bm)
    dst_buf = jnp.where(valid, (a % TOPK_IN_GROUP) * t + tok, spare).reshape(-1)
    dst_buf = jnp.concatenate([m + jnp.arange(2 * bm, dtype=I32), dst_buf])
    return tok_buf, dst_buf, blk_e, n_used, weight


def kernel(x, c, w_ada, b_ada, norm_mix, w_in, t5_table, rel_table, w_up_a, w_up_b, w_o, norm_ffn,
           w_rg, b_rg, w_re, b_re, w1, w3, w2, norm_final):
    bn, s, d = x.shape
    assert bn == 1 and w_ada.shape[0] == 1
    assert s % 1024 == 0
    t = bn * s
    x2 = x.reshape(t, d)
    n_sel = min(TOPK_MAX, s // 4)

    mod = _ada(c.reshape(d, 1), w_ada[0], b_ada[0].reshape(1, 6 * d))
    sh1, sc1, g1, sh2, sc2, g2 = [mod[:, i * d:(i + 1) * d] for i in range(6)]

    h = _rms_mod(x2, norm_mix[0].reshape(1, d), sc1, sh1, BF16)

    cols = np.cumsum([0, W_A, W_A, W_A, W_IDX_Q, D_IDX, H_IDX, W_B, W_B, W_B, d, d])
    wsl = [w_in[0][:, cols[i]:cols[i + 1]] for i in range(11)]
    wqa, wka, wva, wqi, wki, wwi, wqb, wkb, wvb, wga, wgb = wsl
    w_main = jnp.concatenate([wqa * (LOG2E / math.sqrt(HD_A)), wka, wva, wqi,
                              wqb * (LOG2E / math.sqrt(HD_B)), wkb, wvb], axis=1).astype(BF16)
    w_gate = jnp.concatenate([wga, wgb], axis=1).astype(BF16)
    w_idx = jnp.concatenate([wki, wwi, jnp.zeros((d, LANES - D_IDX - H_IDX), F32)], axis=1).astype(BF16)

    proj = _matmul(h, w_main, BF16, 1024, 1024, "proj_main")
    gates = _matmul(h, w_gate, F32, 1024, 1024, "proj_gate")
    idx = _matmul(h, w_idx, F32, 1024, LANES, "proj_idx")
    nkt = s // SEL_TK
    kit3 = jnp.transpose(idx[:, :D_IDX].astype(BF16).reshape(nkt, SEL_TK, D_IDX), (0, 2, 1))
    wi = idx[:, D_IDX:D_IDX + H_IDX] * ((H_IDX ** -0.5) * (D_IDX ** -0.5))
    kt_a = jnp.transpose(proj[:, W_A:2 * W_A])
    kt_b = jnp.transpose(proj[:, 5 * W_A:6 * W_A])

    mask4 = _select(proj, wi, kit3, n_sel, qi_col_block=3)
    y_a = _dsa_attn(proj, kt_a, mask4, t5_table, q_col=0, v_col=2)
    y_b = _band_attn(proj, kt_b, rel_table[0], q_col=4, v_col=6)

    merged = _merge(y_a, y_b, gates, w_up_a[0].astype(BF16), w_up_b[0].astype(BF16))

    nr = LANES
    w_router = jnp.concatenate([w_rg[0], w_re[0], jnp.zeros((d, nr - N_GROUPS - N_EXPERTS), F32)], axis=1)
    b_router = jnp.concatenate([b_rg[0], b_re[0], jnp.zeros((nr - N_GROUPS - N_EXPERTS,), F32)]).reshape(1, nr)
    x1, h2 = _post(x2, merged, w_o[0].astype(BF16), g1, norm_ffn[0].reshape(1, d), sc2, sh2)
    logits = _router(h2, w_router, b_router)
    tok_buf, dst_buf, blk_e, n_used, weight = _route(logits, t)
    z = _moe(h2, tok_buf, dst_buf, blk_e, n_used, w1[0], w3[0], w2[0])
    out = _final(x1, z, weight, g2, norm_final.reshape(1, d))
    return out.reshape(bn, s, d)
```

```python
import functools
import math

import numpy as np
import jax
import jax.numpy as jnp
from jax import lax
from jax.experimental import pallas as pl
from jax.experimental.pallas import tpu as pltpu

F32 = jnp.float32
BF16 = jnp.bfloat16
I32 = jnp.int32

CHUNK = 64
EPS = 1e-6
H_A, HD_A = 8, 128
H_IDX, D_IDX = 16, 64
TOPK_MAX = 256
N_T5_BUCKETS = 32
T5_MAX_DIST = 1024
H_B, HD_B = 8, 128
N_LEFT_CHUNKS = 8
REL_CLIP = 128
N_GROUPS = 8
EXP_PER_GROUP = 8
N_EXPERTS = N_GROUPS * EXP_PER_GROUP
TOPK_IN_GROUP = 2

W_A = H_A * HD_A
W_B = H_B * HD_B
W_IDX_Q = H_IDX * D_IDX

NEG = -1e30
INT_MIN = -(2 ** 31)
KEY_FMAX = 0x7F7FFFFF
LOG2E = math.log2(math.e)

LANES = 128
VMEM_LIMIT = 56 * 1024 * 1024

SEL_TQ = 256
SEL_ROWS = 128
SEL_TK = 512
ATT_TQ = 256
ATT_TK = 1024
BAND_TQ = 256
MOE_BM = 256


def _params(sem, vmem=VMEM_LIMIT):
    return pltpu.CompilerParams(dimension_semantics=sem, vmem_limit_bytes=vmem)


def _load_rows(ref, c):
    rows = ref.shape[0] // c
    return jnp.concatenate([ref[pl.ds(k, rows, stride=c), :] for k in range(c)], axis=1)


def _store_rows(ref, val):
    rows = val.shape[0]
    c = ref.shape[0] // rows
    for k in range(c):
        ref[pl.ds(k, rows, stride=c), :] = val[:, k * LANES:(k + 1) * LANES]


def _ada_kernel(c_ref, w_ref, b_ref, o_ref, *, kc):
    d = w_ref.shape[0]
    tn = w_ref.shape[1]

    def body(k, acc):
        r0 = pl.multiple_of(k * kc, kc)
        cc = c_ref[pl.ds(r0, kc), :]
        ca = cc * jax.nn.sigmoid(cc)
        return acc + jnp.sum(w_ref[pl.ds(r0, kc), :] * ca, axis=0, keepdims=True)

    acc = lax.fori_loop(0, d // kc, body, jnp.zeros((1, tn), F32))
    o_ref[...] = acc + b_ref[...]


def _ada(c_col, w, b_row, tn=1024, kc=256):
    d, n = w.shape
    return pl.pallas_call(
        functools.partial(_ada_kernel, kc=kc),
        grid=(n // tn,),
        in_specs=[pl.BlockSpec((d, 1), lambda j: (0, 0)),
                  pl.BlockSpec((d, tn), lambda j: (0, j)),
                  pl.BlockSpec((1, tn), lambda j: (0, j))],
        out_specs=pl.BlockSpec((1, tn), lambda j: (0, j)),
        out_shape=jax.ShapeDtypeStruct((1, n), F32),
        compiler_params=_params(("arbitrary",)),
        name="ada",
    )(c_col, w, b_row)


def _rms_mod_kernel(x_ref, g_ref, sc_ref, sh_ref, o_ref):
    x = x_ref[...]
    y = x * lax.rsqrt(jnp.mean(x * x, axis=-1, keepdims=True) + EPS)
    o_ref[...] = (y * g_ref[...] * (1.0 + sc_ref[...]) + sh_ref[...]).astype(o_ref.dtype)


def _rms_mod(x2, g, sc, sh, out_dtype, tm=512):
    t, d = x2.shape
    row = pl.BlockSpec((1, d), lambda i: (0, 0))
    return pl.pallas_call(
        _rms_mod_kernel,
        grid=(t // tm,),
        in_specs=[pl.BlockSpec((tm, d), lambda i: (i, 0)), row, row, row],
        out_specs=pl.BlockSpec((tm, d), lambda i: (i, 0)),
        out_shape=jax.ShapeDtypeStruct((t, d), out_dtype),
        compiler_params=_params(("parallel",)),
        name="rms_mod",
    )(x2, g, sc, sh)


def _mm_kernel(a_ref, b_ref, o_ref):
    o_ref[...] = jnp.dot(a_ref[...], b_ref[...], preferred_element_type=F32).astype(o_ref.dtype)


def _matmul(a, b, out_dtype, tm, tn, name):
    m, k = a.shape
    n = b.shape[1]
    return pl.pallas_call(
        _mm_kernel,
        grid=(m // tm, n // tn),
        in_specs=[pl.BlockSpec((tm, k), lambda i, j: (i, 0)),
                  pl.BlockSpec((k, tn), lambda i, j: (0, j))],
        out_specs=pl.BlockSpec((tm, tn), lambda i, j: (i, j)),
        out_shape=jax.ShapeDtypeStruct((m, n), out_dtype),
        compiler_params=_params(("parallel", "arbitrary")),
        name=name,
    )(a, b)


def _sortable(x):
    bits = pltpu.bitcast(x, I32)
    return bits ^ ((bits >> 31) & 0x7FFFFFFF)


def _unsortable(k):
    return pltpu.bitcast(k ^ ((k >> 31) & 0x7FFFFFFF), F32)


def _select_kernel(qi_ref, wi_ref, kit_ref, mask_ref, key_sc, qh_sc, wb_sc, gm_sc, *, n_sel):
    nkt, tq, tk = key_sc.shape
    nsub = tk // LANES
    ngrp = gm_sc.shape[0]
    npc = tq // SEL_ROWS
    pieces = [slice(p * SEL_ROWS, (p + 1) * SEL_ROWS) for p in range(npc)]
    i = pl.program_id(0)
    q0 = i * tq
    nvis = (q0 + tq + tk - 1) // tk

    for h in range(H_IDX):
        qh_sc[h * tq:(h + 1) * tq, :] = qi_ref[:, h * D_IDX:(h + 1) * D_IDX]
        wb_sc[h] = jnp.broadcast_to(wi_ref[:, h:h + 1], (tq, LANES))

    row = lax.broadcasted_iota(I32, (tq, 1), 0) + q0
    limit = (row // CHUNK + 1) * CHUNK
    lane = lax.broadcasted_iota(I32, (tq, LANES), 1)
    lane_p = lax.broadcasted_iota(I32, (SEL_ROWS, LANES), 1)
    limits = [((lax.broadcasted_iota(I32, (SEL_ROWS, 1), 0) + (q0 + p * SEL_ROWS)) // CHUNK + 1) * CHUNK
              for p in range(npc)]

    def score_tile(kt, gmax):
        k_t = kit_ref[kt]
        accs = [jnp.zeros((tq, LANES), F32) for _ in range(nsub)]
        s_all = jnp.dot(qh_sc[...], k_t, preferred_element_type=F32)
        for h in range(H_IDX):
            s = s_all[h * tq:(h + 1) * tq, :]
            wb = wb_sc[h]
            for a in range(nsub):
                accs[a] = accs[a] + wb * jnp.maximum(s[:, a * LANES:(a + 1) * LANES], 0.0)
        gmax = list(gmax)
        for a in range(nsub):
            col = lane + (kt * tk + a * LANES)
            key = jnp.where(col < limit, _sortable(accs[a]), INT_MIN)
            key_sc[kt, :, a * LANES:(a + 1) * LANES] = key
            gmax[a % ngrp] = jnp.maximum(gmax[a % ngrp], key)
        return tuple(gmax)

    gmax = lax.fori_loop(0, nvis, score_tile,
                         tuple(jnp.full((tq, LANES), INT_MIN, I32) for _ in range(ngrp)))
    for g in range(ngrp):
        gm_sc[g] = gmax[g]

    def count(preds):
        def body(kt, cs):
            cs = list(cs)
            for p in range(npc):
                for a in range(nsub):
                    kk = key_sc[kt, pieces[p], a * LANES:(a + 1) * LANES]
                    cs[p] = cs[p] + preds[p](kk, lane_p + (kt * tk + a * LANES)).astype(I32)
            return tuple(cs)
        cs = lax.fori_loop(0, nvis, body, tuple(jnp.zeros((SEL_ROWS, LANES), I32) for _ in range(npc)))
        return [jnp.sum(c, axis=1, keepdims=True) for c in cs]

    few, st0 = [], []
    for p in range(npc):
        gmin, ghi = gm_sc[0, pieces[p], :], gm_sc[0, pieces[p], :]
        for g in range(1, ngrp):
            gmin = jnp.minimum(gmin, gm_sc[g, pieces[p], :])
            ghi = jnp.maximum(ghi, gm_sc[g, pieces[p], :])
        lo0 = jnp.maximum(jnp.min(gmin, axis=1, keepdims=True), INT_MIN + 1)
        hi0 = jnp.max(ghi, axis=1, keepdims=True)
        fw = limits[p] < n_sel
        few.append(fw)
        st0.append((jnp.where(fw, INT_MIN + 1, lo0), jnp.where(fw, INT_MIN + 1, hi0),
                    jnp.full((SEL_ROWS, 1), 2 ** 30, I32), jnp.zeros((SEL_ROWS, 1), I32)))

    def bis_cond(sts):
        act = [jnp.max((st[0] < st[1]).astype(I32)) for st in sts]
        return functools.reduce(jnp.maximum, act) > 0

    def midpoint(lo, hi):
        mid = (lo | hi) - ((lo ^ hi) >> 1)
        fin = lambda k: jnp.clip(k, -KEY_FMAX - 1, KEY_FMAX)
        vmid = _sortable(0.5 * _unsortable(fin(lo)) + 0.5 * _unsortable(fin(hi)))
        return jnp.where((vmid > lo) & (vmid <= hi), vmid, mid)

    def bis_body(sts):
        mids = [midpoint(st[0], st[1]) for st in sts]
        cs = count([(lambda kk, col, m=m: kk >= m) for m in mids])
        out = []
        for (lo, hi, c_lo, c_hi1), mid, c in zip(sts, mids, cs):
            active = lo < hi
            up = active & (c >= n_sel)
            dn = active & (c < n_sel)
            hit = active & (c == n_sel)
            out.append((jnp.where(up, mid, lo), jnp.where(dn, mid - 1, jnp.where(hit, mid, hi)),
                        jnp.where(up, c, c_lo), jnp.where(dn, c, c_hi1)))
        return tuple(out)

    sts = lax.while_loop(bis_cond, bis_body, tuple(st0))
    thr = [st[0] for st in sts]

    excess = [(st[2] > n_sel) & jnp.logical_not(fw) for st, fw in zip(sts, few)]
    need = [n_sel - st[3] for st in sts]
    ncol = nkt * tk

    def tie_cut():
        nbits = max(1, int(math.ceil(math.log2(ncol))))

        def cut_body(b, ps):
            cands = [p | jnp.left_shift(jnp.int32(1), nbits - 1 - b) for p in ps]
            cnts = count([(lambda kk, col, t=t, cd=cd: (kk == t) & (col < cd)) for t, cd in zip(thr, cands)])
            return tuple(jnp.where(cnt < nd, cd, p) for cnt, nd, cd, p in zip(cnts, need, cands, ps))

        ps = lax.fori_loop(0, nbits, cut_body, tuple(jnp.zeros((SEL_ROWS, 1), I32) for _ in range(npc)))
        return tuple(p + 1 for p in ps)

    any_excess = functools.reduce(jnp.maximum, [jnp.max(e.astype(I32)) for e in excess]) > 0
    cuts = lax.cond(any_excess, tie_cut, lambda: tuple(jnp.full((SEL_ROWS, 1), ncol, I32) for _ in range(npc)))
    cuts = [jnp.where(e, ct, ncol) for e, ct in zip(excess, cuts)]

    def write_tile(kt, carry):
        col = lax.broadcasted_iota(I32, (SEL_ROWS, tk), 1) + kt * tk
        for p in range(npc):
            kk = key_sc[kt, pieces[p], :]
            sel = (kk > thr[p]) | ((kk == thr[p]) & (col < cuts[p]))
            mask_ref[kt, pieces[p], :] = jnp.where(sel, 0.0, NEG).astype(mask_ref.dtype)
        return carry

    lax.fori_loop(0, nvis, write_tile, 0)

    def fill_tile(kt, carry):
        mask_ref[kt] = jnp.full((tq, tk), NEG, mask_ref.dtype)
        return carry

    lax.fori_loop(nvis, nkt, fill_tile, 0)


def _select(proj, wi, kit3, n_sel, qi_col_block):
    s = proj.shape[0]
    nkt, _, tk = kit3.shape
    tq = SEL_TQ
    nqb = s // tq
    ngrp = -(-n_sel // LANES)
    assert ngrp <= tk // LANES and tq % SEL_ROWS == 0
    return pl.pallas_call(
        functools.partial(_select_kernel, n_sel=n_sel),
        grid=(nqb,),
        in_specs=[pl.BlockSpec((tq, W_IDX_Q), lambda i: (i, qi_col_block)),
                  pl.BlockSpec((tq, H_IDX), lambda i: (i, 0)),
                  pl.BlockSpec((nkt, D_IDX, tk), lambda i: (0, 0, 0))],
        out_specs=pl.BlockSpec((None, nkt, tq, tk), lambda i: (i, 0, 0, 0)),
        out_shape=jax.ShapeDtypeStruct((nqb, nkt, tq, tk), BF16),
        scratch_shapes=[pltpu.VMEM((nkt, tq, tk), I32),
                        pltpu.VMEM((H_IDX * tq, D_IDX), BF16),
                        pltpu.VMEM((H_IDX, tq, LANES), F32),
                        pltpu.VMEM((ngrp, tq, LANES), I32)],
        compiler_params=_params(("arbitrary",)),
        name="dsa_select",
    )(proj, wi, kit3)


def _dsa_attn_kernel(qb_ref, kb_ref, first_ref, last_ref, slab_ref,
                     q_ref, kt_ref, v_ref, mask_ref, gen_ref, far_ref, o_ref,
                     m_sc, l_sc, acc_sc, band_ref):
    step = pl.program_id(0)
    tq = q_ref.shape[0]
    tk = kt_ref.shape[1]
    nslab = tk // LANES
    slab0 = slab_ref[step]

    @pl.when(step == 0)
    def _():
        win = LANES + tq
        for h in range(H_A):
            for u in range(band_ref.shape[1]):
                rows = jnp.broadcast_to(gen_ref[h:h + 1, u * LANES:u * LANES + win], (tq, win))
                rows = pltpu.roll(rows, win - (tq - 1), 1, stride=1, stride_axis=0)
                band_ref[h, u] = rows[:, 0:LANES]

    @pl.when(first_ref[step] == 1)
    def _():
        m_sc[...] = jnp.full(m_sc.shape, NEG, F32)
        l_sc[...] = jnp.zeros(l_sc.shape, F32)
        acc_sc[...] = jnp.zeros(acc_sc.shape, F32)

    def tile(near):
        mask = jnp.concatenate(
            [jnp.concatenate([mask_ref[r, j] for j in range(mask_ref.shape[1])], axis=1)
             for r in range(mask_ref.shape[0])], axis=0).astype(F32)
        ones = jnp.ones((tk, HD_A), BF16)
        for h in range(H_A):
            hs = slice(h * HD_A, (h + 1) * HD_A)
            s = jnp.dot(q_ref[:, hs], kt_ref[hs, :], preferred_element_type=F32) + mask
            if near:
                s = s + jnp.concatenate([band_ref[h, slab0 + a] for a in range(nslab)], axis=1)
                shift = 0.0
            else:
                shift = far_ref[h]
            smax = s[:, 0:LANES]
            for a in range(1, nslab):
                smax = jnp.maximum(smax, s[:, a * LANES:(a + 1) * LANES])
            m_old = m_sc[h]
            m_new = jnp.maximum(m_old, jnp.max(smax, axis=1, keepdims=True) + shift)
            alpha = jnp.exp2(m_old - m_new)
            mm = m_new - shift
            p = jnp.concatenate([jnp.exp2(s[:, a * LANES:(a + 1) * LANES] - mm) for a in range(nslab)],
                                axis=1).astype(BF16)
            pv = jnp.dot(p, jnp.concatenate([v_ref[:, hs], ones], axis=1), preferred_element_type=F32)
            acc_sc[:, hs] = alpha * acc_sc[:, hs] + pv[:, 0:HD_A]
            l_sc[h] = alpha * l_sc[h] + pv[:, HD_A:2 * HD_A]
            m_sc[h] = m_new

    @pl.when(slab0 >= 0)
    def _():
        tile(True)

    @pl.when(slab0 < 0)
    def _():
        tile(False)

    @pl.when(last_ref[step] == 1)
    def _():
        for h in range(H_A):
            hs = slice(h * HD_A, (h + 1) * HD_A)
            o_ref[:, hs] = (acc_sc[:, hs] / l_sc[h]).astype(o_ref.dtype)


def _t5_bucket_np(rel):
    nb = N_T5_BUCKETS // 2
    ret = (rel > 0).astype(np.int32) * nb
    n = np.abs(rel)
    max_exact = nb // 2
    nf = np.maximum(n, 1).astype(np.float32)
    large = max_exact + (np.log(nf / np.float32(max_exact)) / np.float32(math.log(T5_MAX_DIST / max_exact))
                         * np.float32(nb - max_exact)).astype(np.int32)
    large = np.minimum(large, nb - 1)
    return ret + np.where(n < max_exact, n, large)


def _dsa_plan(s, tq, tk):
    nqb = s // tq
    d_all = np.arange(-(s - 1), CHUNK, dtype=np.int64)
    b_all = _t5_bucket_np(d_all)
    far_bucket = int(b_all[0])
    varying = np.nonzero(b_all != far_bucket)[0]
    d_lo = int(d_all[varying[0]])
    o_min = min(int(math.ceil((d_lo - (tk - 1)) / LANES)) * LANES, 0)
    n_slabs = (-o_min + tk) // LANES
    qb, kb, first, last, slab = [], [], [], [], []
    for i in range(nqb):
        q0 = i * tq
        nvis = (q0 + tq + tk - 1) // tk
        for j in range(nvis):
            o = j * tk - q0
            qb.append(i)
            kb.append(j)
            first.append(1 if j == 0 else 0)
            last.append(1 if j == nvis - 1 else 0)
            slab.append(-1 if o < o_min else (o - o_min) // LANES)
    u = np.arange(n_slabs * LANES + tq - 1)
    g_bucket = _t5_bucket_np(np.clip(o_min + u - (tq - 1), -(s - 1), None))
    plan = tuple(np.asarray(a, np.int32) for a in (qb, kb, first, last, slab))
    return plan, g_bucket, far_bucket, n_slabs


def _dsa_attn(proj, kt, mask4, t5_table, q_col, v_col):
    s = proj.shape[0]
    tq, tk = ATT_TQ, ATT_TK
    (qb, kb, first, last, slab), g_bucket, far_bucket, n_slabs = _dsa_plan(s, tq, tk)
    nsteps = len(qb)
    t5l = t5_table * LOG2E
    gen = jnp.transpose(t5l[g_bucket], (1, 0))
    gen = jnp.pad(gen, ((0, 0), (0, n_slabs * LANES + tq - gen.shape[1])))
    far = t5l[far_bucket]
    rq = tq // SEL_TQ
    grid_spec = pltpu.PrefetchScalarGridSpec(
        num_scalar_prefetch=5,
        grid=(nsteps,),
        in_specs=[
            pl.BlockSpec((tq, W_A), lambda t, qb, kb, f, l, sl: (qb[t], q_col)),
            pl.BlockSpec((W_A, tk), lambda t, qb, kb, f, l, sl: (0, kb[t])),
            pl.BlockSpec((tk, W_A), lambda t, qb, kb, f, l, sl: (kb[t], v_col)),
            pl.BlockSpec((rq, tk // SEL_TK, SEL_TQ, SEL_TK), lambda t, qb, kb, f, l, sl: (qb[t], kb[t], 0, 0)),
            pl.BlockSpec((H_A, n_slabs * LANES + tq), lambda t, qb, kb, f, l, sl: (0, 0)),
            pl.BlockSpec(memory_space=pltpu.SMEM),
        ],
        out_specs=pl.BlockSpec((tq, W_A), lambda t, qb, kb, f, l, sl: (qb[t], 0)),
        scratch_shapes=[pltpu.VMEM((H_A, tq, LANES), F32),
                        pltpu.VMEM((H_A, tq, LANES), F32),
                        pltpu.VMEM((tq, W_A), F32),
                        pltpu.VMEM((H_A, n_slabs, tq, LANES), F32)],
    )
    return pl.pallas_call(
        _dsa_attn_kernel,
        grid_spec=grid_spec,
        out_shape=jax.ShapeDtypeStruct((s, W_A), BF16),
        compiler_params=_params(("arbitrary",)),
        name="dsa_attn",
    )(jnp.asarray(qb), jnp.asarray(kb), jnp.asarray(first), jnp.asarray(last), jnp.asarray(slab),
      proj, kt, proj, mask4, gen, far)


def _band_kernel(q_ref, k0_ref, k1_ref, k2_ref, v0_ref, v1_ref, v2_ref, gen_ref, o_ref, bias_ref):
    i = pl.program_id(0)
    tq = q_ref.shape[0]
    wk = 3 * tq

    @pl.when(i == 0)
    def _():
        r = lax.broadcasted_iota(I32, (tq, wk), 0)
        c = lax.broadcasted_iota(I32, (tq, wk), 1)
        dchunk = (c // CHUNK - 2 * tq // CHUNK) - r // CHUNK
        in_band = (dchunk <= 0) & (dchunk >= -N_LEFT_CHUNKS)
        win = wk + tq
        for h in range(H_B):
            rows = jnp.broadcast_to(gen_ref[h:h + 1, :], (tq, win))
            rows = pltpu.roll(rows, win - (tq - 1), 1, stride=1, stride_axis=0)
            bias_ref[h] = jnp.where(in_band, rows[:, 0:wk], NEG)
    col = lax.broadcasted_iota(I32, (tq, wk), 1)
    start_mask = jnp.where(col + (i - 2) * tq >= 0, 0.0, NEG)
    for h in range(H_B):
        hs = slice(h * HD_B, (h + 1) * HD_B)
        kc = jnp.concatenate([k0_ref[hs, :], k1_ref[hs, :], k2_ref[hs, :]], axis=1)
        vc = jnp.concatenate([v0_ref[:, hs], v1_ref[:, hs], v2_ref[:, hs]], axis=0)
        s = jnp.dot(q_ref[:, hs], kc, preferred_element_type=F32) + bias_ref[h] + start_mask
        m = jnp.max(s, axis=1, keepdims=True)
        p = jnp.exp2(s - m)
        l = jnp.sum(p, axis=1, keepdims=True)
        pv = jnp.dot(p.astype(BF16), vc, preferred_element_type=F32)
        o_ref[:, hs] = (pv / l).astype(o_ref.dtype)


def _band_gen(rel_table, tq):
    assert 2 * tq >= N_LEFT_CHUNKS * CHUNK and tq % CHUNK == 0
    x = np.arange(4 * tq)
    idx = np.clip(2 * tq + (tq - 1) - x, -REL_CLIP, REL_CLIP) + REL_CLIP
    return rel_table[:, idx] * LOG2E


def _band_attn(proj, kt, rel_table, q_col, v_col):
    s = proj.shape[0]
    tq = BAND_TQ
    gen = _band_gen(rel_table, tq)

    def kspec(back):
        return pl.BlockSpec((W_B, tq), lambda i: (0, jnp.maximum(i - back, 0)))

    def vspec(back):
        return pl.BlockSpec((tq, W_B), lambda i: (jnp.maximum(i - back, 0), v_col))

    return pl.pallas_call(
        _band_kernel,
        grid=(s // tq,),
        in_specs=[pl.BlockSpec((tq, W_B), lambda i: (i, q_col)),
                  kspec(2), kspec(1), kspec(0), vspec(2), vspec(1), vspec(0),
                  pl.BlockSpec((H_B, 4 * tq), lambda i: (0, 0))],
        out_specs=pl.BlockSpec((tq, W_B), lambda i: (i, 0)),
        out_shape=jax.ShapeDtypeStruct((s, W_B), BF16),
        scratch_shapes=[pltpu.VMEM((H_B, tq, 3 * tq), F32)],
        compiler_params=_params(("arbitrary",)),
        name="band_attn",
    )(proj, kt, kt, kt, proj, proj, proj, gen)


def _merge_kernel(ya_ref, yb_ref, ga_ref, gb_ref, wa_ref, wb_ref, o_ref):
    ua = jnp.dot(ya_ref[...], wa_ref[...], preferred_element_type=F32)
    ub = jnp.dot(yb_ref[...], wb_ref[...], preferred_element_type=F32)
    o_ref[...] = (jax.nn.sigmoid(ga_ref[...]) * ua + jax.nn.sigmoid(gb_ref[...]) * ub).astype(o_ref.dtype)


def _merge(ya, yb, gates, wa, wb, tm=256):
    s, d = ya.shape[0], wa.shape[1]
    return pl.pallas_call(
        _merge_kernel,
        grid=(s // tm,),
        in_specs=[pl.BlockSpec((tm, W_A), lambda i: (i, 0)),
                  pl.BlockSpec((tm, W_B), lambda i: (i, 0)),
                  pl.BlockSpec((tm, d), lambda i: (i, 0)),
                  pl.BlockSpec((tm, d), lambda i: (i, 1)),
                  pl.BlockSpec((W_A, d), lambda i: (0, 0)),
                  pl.BlockSpec((W_B, d), lambda i: (0, 0))],
        out_specs=pl.BlockSpec((tm, d), lambda i: (i, 0)),
        out_shape=jax.ShapeDtypeStruct((s, d), BF16),
        compiler_params=_params(("parallel",)),
        name="merge",
    )(ya, yb, gates, gates, wa, wb)


def _post_kernel(x_ref, m_ref, wo_ref, g1_ref, gn_ref, sc_ref, sh_ref, x1_ref, h2_ref):
    x1 = x_ref[...] + g1_ref[...] * jnp.dot(m_ref[...], wo_ref[...], preferred_element_type=F32)
    x1_ref[...] = x1
    y = x1 * lax.rsqrt(jnp.mean(x1 * x1, axis=-1, keepdims=True) + EPS)
    _store_rows(h2_ref, y * gn_ref[...] * (1.0 + sc_ref[...]) + sh_ref[...])


def _post(x2, merged, wo, g1, gn, sc, sh, tm=256):
    t, d = x2.shape
    row = pl.BlockSpec((1, d), lambda i: (0, 0))
    tile = pl.BlockSpec((tm, d), lambda i: (i, 0))
    return pl.pallas_call(
        _post_kernel,
        grid=(t // tm,),
        in_specs=[tile, tile, pl.BlockSpec((d, d), lambda i: (0, 0)), row, row, row, row],
        out_specs=[tile, pl.BlockSpec((tm * (d // LANES), LANES), lambda i: (i, 0))],
        out_shape=[jax.ShapeDtypeStruct((t, d), F32), jax.ShapeDtypeStruct((t * (d // LANES), LANES), F32)],
        compiler_params=_params(("parallel",)),
        name="post",
    )(x2, merged, wo, g1, gn, sc, sh)


def _router_kernel(h_ref, wr_ref, br_ref, lg_ref):
    lg_ref[...] = jnp.dot(_load_rows(h_ref, wr_ref.shape[0] // LANES), wr_ref[...], preferred_element_type=F32,
                          precision=lax.Precision.HIGHEST) + br_ref[...]


def _router(h2, w_router, b_router, tm=512):
    d, nr = w_router.shape
    t = h2.shape[0] // (d // LANES)
    return pl.pallas_call(
        _router_kernel,
        grid=(t // tm,),
        in_specs=[pl.BlockSpec((tm * (d // LANES), LANES), lambda i: (i, 0)),
                  pl.BlockSpec((d, nr), lambda i: (0, 0)), pl.BlockSpec((1, nr), lambda i: (0, 0))],
        out_specs=pl.BlockSpec((tm, nr), lambda i: (i, 0)),
        out_shape=jax.ShapeDtypeStruct((t, nr), F32),
        compiler_params=_params(("parallel",)),
        name="router",
    )(h2, w_router, b_router)


def _moe_kernel(tok_ref, dst_ref, be_ref, nu_ref,
                h_hbm, w1_ref, w3_ref, w2_ref, z_hbm,
                x0, x1, y0, y1, w1b, w3b, w2b, gsem, ssem):
    b = pl.program_id(0)
    c = w1b.shape[0] // LANES
    bm = x0.shape[0] // c
    nu = nu_ref[0]
    xs, ys = (x0, x1), (y0, y1)

    def gather_copy(sl, r, t):
        return pltpu.make_async_copy(h_hbm.at[pl.ds(t * c, c), :], xs[sl].at[pl.ds(r * c, c), :], gsem.at[sl])

    def scatter_copy(sl, r, d):
        return pltpu.make_async_copy(ys[sl].at[pl.ds(r * c, c), :], z_hbm.at[pl.ds(d * c, c), :], ssem.at[sl])

    def gather_start(blk, sl):
        for r in range(bm):
            gather_copy(sl, r, tok_ref[blk * bm + r]).start()

    def scatter_start(blk, sl):
        for r in range(bm):
            scatter_copy(sl, r, dst_ref[(blk + 2) * bm + r]).start()

    def gather_wait(sl):
        for r in range(bm):
            gather_copy(sl, r, 0).wait()

    def scatter_wait(sl):
        for r in range(bm):
            scatter_copy(sl, r, 0).wait()

    @pl.when(b == 0)
    def _():
        y0[...] = jnp.zeros(y0.shape, F32)
        y1[...] = jnp.zeros(y1.shape, F32)
        scatter_start(-2, 0)
        gather_start(0, 0)

    @pl.when((b < nu) & ((b == 0) | (be_ref[b] != be_ref[jnp.maximum(b - 1, 0)])))
    def _():
        w1b[...] = w1_ref[0].astype(BF16)
        w3b[...] = w3_ref[0].astype(BF16)
        w2b[...] = w2_ref[0].astype(BF16)

    def main(sl):
        gather_wait(sl)
        scatter_wait(sl)
        gather_start(b + 1, 1 - sl)
        scatter_start(b - 1, 1 - sl)
        x = _load_rows(xs[sl], c).astype(BF16)
        a1 = jnp.dot(x, w1b[...], preferred_element_type=F32)
        a3 = jnp.dot(x, w3b[...], preferred_element_type=F32)
        a = (a1 * jax.nn.sigmoid(a1)) * a3
        y = jnp.dot(a.astype(BF16), w2b[...], preferred_element_type=F32)
        _store_rows(ys[sl], y)

    def drain(sl):
        gather_wait(sl)
        scatter_wait(sl)
        scatter_start(b - 1, 1 - sl)
        scatter_wait(1 - sl)

    for sl in range(2):
        @pl.when((b < nu) & (b % 2 == sl))
        def _():
            main(sl)

        @pl.when((b == nu) & (b % 2 == sl))
        def _():
            drain(sl)


def _moe(h2, tok_buf, dst_buf, blk_e, n_used, w1, w3, w2):
    d, dff = w1.shape[1:]
    c = d // LANES
    t = h2.shape[0] // c
    bm = MOE_BM
    nsteps = blk_e.shape[0]
    assert tok_buf.shape[0] == nsteps * bm and dst_buf.shape[0] == (nsteps + 2) * bm
    grid_spec = pltpu.PrefetchScalarGridSpec(
        num_scalar_prefetch=4,
        grid=(nsteps,),
        in_specs=[
            pl.BlockSpec(memory_space=pl.ANY),
            pl.BlockSpec((1, d, dff), lambda b, tok, dst, be, nu: (be[b], 0, 0)),
            pl.BlockSpec((1, d, dff), lambda b, tok, dst, be, nu: (be[b], 0, 0)),
            pl.BlockSpec((1, dff, d), lambda b, tok, dst, be, nu: (be[b], 0, 0)),
        ],
        out_specs=pl.BlockSpec(memory_space=pl.ANY),
        scratch_shapes=[pltpu.VMEM((bm * c, LANES), F32), pltpu.VMEM((bm * c, LANES), F32),
                        pltpu.VMEM((bm * c, LANES), F32), pltpu.VMEM((bm * c, LANES), F32),
                        pltpu.VMEM((d, dff), BF16), pltpu.VMEM((d, dff), BF16), pltpu.VMEM((dff, d), BF16),
                        pltpu.SemaphoreType.DMA((2,)), pltpu.SemaphoreType.DMA((2,))],
    )
    return pl.pallas_call(
        _moe_kernel,
        grid_spec=grid_spec,
        out_shape=jax.ShapeDtypeStruct(((TOPK_IN_GROUP * t + 2 * bm) * c, LANES), F32),
        compiler_params=pltpu.CompilerParams(dimension_semantics=("arbitrary",),
                                             vmem_limit_bytes=VMEM_LIMIT, has_side_effects=True),
        name="moe",
    )(tok_buf, dst_buf, blk_e, n_used, h2, w1, w3, w2)


def _final_kernel(x1_ref, z0_ref, z1_ref, w_ref, g2_ref, gn_ref, o_ref):
    c = x1_ref.shape[1] // LANES
    moe = w_ref[:, 0:1] * _load_rows(z0_ref, c) + w_ref[:, 1:2] * _load_rows(z1_ref, c)
    x2 = x1_ref[...] + g2_ref[...] * moe
    y = x2 * lax.rsqrt(jnp.mean(x2 * x2, axis=-1, keepdims=True) + EPS)
    o_ref[...] = y * gn_ref[...]


def _final(x1, z, weight, g2, gn, tm=256):
    t, d = x1.shape
    row = pl.BlockSpec((1, d), lambda i: (0, 0))
    return pl.pallas_call(
        _final_kernel,
        grid=(t // tm,),
        in_specs=[pl.BlockSpec((tm, d), lambda i: (i, 0)),
                  pl.BlockSpec((tm * (d // LANES), LANES), lambda i: (i, 0)),
                  pl.BlockSpec((tm * (d // LANES), LANES), lambda i: (t // tm + i, 0)),
                  pl.BlockSpec((tm, TOPK_IN_GROUP), lambda i: (i, 0)), row, row],
        out_specs=pl.BlockSpec((tm, d), lambda i: (i, 0)),
        out_shape=jax.ShapeDtypeStruct((t, d), F32),
        compiler_params=_params(("parallel",)),
        name="final",
    )(x1, z, z, weight, g2, gn)


def _route(logits, t):
    gl = logits[:, :N_GROUPS]
    el = logits[:, N_GROUPS:N_GROUPS + N_EXPERTS].reshape(t, N_GROUPS, EXP_PER_GROUP)
    g_prob = jax.nn.softmax(gl, axis=-1)
    grp = jnp.argmax(gl, axis=-1).astype(I32)
    p_grp = jnp.take_along_axis(g_prob, grp[:, None], axis=-1)[:, 0]
    e_in = jnp.take_along_axis(el, grp[:, None, None], axis=1)[:, 0]
    top_v, top_i = lax.top_k(e_in, TOPK_IN_GROUP)
    p_in = jax.nn.softmax(top_v, axis=-1)
    expert = grp[:, None] * EXP_PER_GROUP + top_i.astype(I32)
    weight = p_grp[:, None] * p_in

    bm = MOE_BM
    m = t * TOPK_IN_GROUP
    e_flat = expert.reshape(m)
    order = jnp.argsort(e_flat, stable=True).astype(I32)
    counts = jnp.bincount(e_flat, length=N_EXPERTS).astype(I32)
    start = jnp.cumsum(counts) - counts
    padded = ((counts + bm - 1) // bm) * bm
    pend = jnp.cumsum(padded)
    pstart = pend - padded
    nb = m // bm + N_EXPERTS + 1
    blk_e = jnp.minimum(jnp.searchsorted(pend, jnp.arange(nb, dtype=I32) * bm, side='right'),
                        N_EXPERTS - 1).astype(I32)
    n_used = (pend[-1] // bm).astype(I32).reshape(1)
    pos = jnp.arange(nb * bm, dtype=I32).reshape(nb, bm)
    src = (start[blk_e] - pstart[blk_e])[:, None] + pos
    valid = (src < (start + counts)[blk_e][:, None]) & (pos < pend[-1])
    a = order[jnp.clip(src, 0, m - 1)]
    tok = a // TOPK_IN_GROUP
    tok_buf = jnp.where(valid, tok, 0).reshape(-1)
    spare = m + pos % (2 * bm)
    dst_buf = jnp.where(valid, (a % TOPK_IN_GROUP) * t + tok, spare).reshape(-1)
    dst_buf = jnp.concatenate([m + jnp.arange(2 * bm, dtype=I32), dst_buf])
    return tok_buf, dst_buf, blk_e, n_used, weight


def kernel(x, c, w_ada, b_ada, norm_mix, w_in, t5_table, rel_table, w_up_a, w_up_b, w_o, norm_ffn,
           w_rg, b_rg, w_re, b_re, w1, w3, w2, norm_final):
    bn, s, d = x.shape
    assert bn == 1 and w_ada.shape[0] == 1
    assert s % 1024 == 0
    t = bn * s
    x2 = x.reshape(t, d)
    n_sel = min(TOPK_MAX, s // 4)

    mod = _ada(c.reshape(d, 1), w_ada[0], b_ada[0].reshape(1, 6 * d))
    sh1, sc1, g1, sh2, sc2, g2 = [mod[:, i * d:(i + 1) * d] for i in range(6)]

    h = _rms_mod(x2, norm_mix[0].reshape(1, d), sc1, sh1, BF16)

    cols = np.cumsum([0, W_A, W_A, W_A, W_IDX_Q, D_IDX, H_IDX, W_B, W_B, W_B, d, d])
    wsl = [w_in[0][:, cols[i]:cols[i + 1]] for i in range(11)]
    wqa, wka, wva, wqi, wki, wwi, wqb, wkb, wvb, wga, wgb = wsl
    w_main = jnp.concatenate([wqa * (LOG2E / math.sqrt(HD_A)), wka, wva, wqi,
                              wqb * (LOG2E / math.sqrt(HD_B)), wkb, wvb], axis=1).astype(BF16)
    w_gate = jnp.concatenate([wga, wgb], axis=1).astype(BF16)
    w_idx = jnp.concatenate([wki, wwi, jnp.zeros((d, LANES - D_IDX - H_IDX), F32)], axis=1).astype(BF16)

    proj = _matmul(h, w_main, BF16, 1024, 1024, "proj_main")
    gates = _matmul(h, w_gate, F32, 1024, 1024, "proj_gate")
    idx = _matmul(h, w_idx, F32, 1024, LANES, "proj_idx")
    nkt = s // SEL_TK
    kit3 = jnp.transpose(idx[:, :D_IDX].astype(BF16).reshape(nkt, SEL_TK, D_IDX), (0, 2, 1))
    wi = idx[:, D_IDX:D_IDX + H_IDX] * ((H_IDX ** -0.5) * (D_IDX ** -0.5))
    kt_a = jnp.transpose(proj[:, W_A:2 * W_A])
    kt_b = jnp.transpose(proj[:, 5 * W_A:6 * W_A])

    mask4 = _select(proj, wi, kit3, n_sel, qi_col_block=3)
    y_a = _dsa_attn(proj, kt_a, mask4, t5_table, q_col=0, v_col=2)
    y_b = _band_attn(proj, kt_b, rel_table[0], q_col=4, v_col=6)

    merged = _merge(y_a, y_b, gates, w_up_a[0].astype(BF16), w_up_b[0].astype(BF16))

    nr = LANES
    w_router = jnp.concatenate([w_rg[0], w_re[0], jnp.zeros((d, nr - N_GROUPS - N_EXPERTS), F32)], axis=1)
    b_router = jnp.concatenate([b_rg[0], b_re[0], jnp.zeros((nr - N_GROUPS - N_EXPERTS,), F32)]).reshape(1, nr)
    x1, h2 = _post(x2, merged, w_o[0].astype(BF16), g1, norm_ffn[0].reshape(1, d), sc2, sh2)
    logits = _router(h2, w_router, b_router)
    tok_buf, dst_buf, blk_e, n_used, weight = _route(logits, t)
    z = _moe(h2, tok_buf, dst_buf, blk_e, n_used, w1[0], w3[0], w2[0])
    out = _final(x1, z, weight, g2, norm_final.reshape(1, d))
    return out.reshape(bn, s, d)
```

```python
import functools
import math

import numpy as np
import jax
import jax.numpy as jnp
from jax import lax
from jax.experimental import pallas as pl
from jax.experimental.pallas import tpu as pltpu

F32 = jnp.float32
BF16 = jnp.bfloat16
I32 = jnp.int32

CHUNK = 64
EPS = 1e-6
H_A, HD_A = 8, 128
H_IDX, D_IDX = 16, 64
TOPK_MAX = 256
N_T5_BUCKETS = 32
T5_MAX_DIST = 1024
H_B, HD_B = 8, 128
N_LEFT_CHUNKS = 8
REL_CLIP = 128
N_GROUPS = 8
EXP_PER_GROUP = 8
N_EXPERTS = N_GROUPS * EXP_PER_GROUP
TOPK_IN_GROUP = 2

W_A = H_A * HD_A
W_B = H_B * HD_B
W_IDX_Q = H_IDX * D_IDX

NEG = -1e30
INT_MIN = -(2 ** 31)
KEY_FMAX = 0x7F7FFFFF
LOG2E = math.log2(math.e)

LANES = 128
VMEM_LIMIT = 56 * 1024 * 1024

SEL_TQ = 256
SEL_ROWS = 128
SEL_TK = 512
ATT_TQ = 512
ATT_ROWS = 256
ATT_TK = 1024
BAND_TQ = 256
MOE_BM = 256


def _params(sem, vmem=VMEM_LIMIT):
    return pltpu.CompilerParams(dimension_semantics=sem, vmem_limit_bytes=vmem)


def _load_rows(ref, c):
    rows = ref.shape[0] // c
    return jnp.concatenate([ref[pl.ds(k, rows, stride=c), :] for k in range(c)], axis=1)


def _store_rows(ref, val):
    rows = val.shape[0]
    c = ref.shape[0] // rows
    for k in range(c):
        ref[pl.ds(k, rows, stride=c), :] = val[:, k * LANES:(k + 1) * LANES]


def _ada_kernel(c_ref, w_ref, b_ref, o_ref, *, kc):
    d = w_ref.shape[0]
    tn = w_ref.shape[1]

    def body(k, acc):
        r0 = pl.multiple_of(k * kc, kc)
        cc = c_ref[pl.ds(r0, kc), :]
        ca = cc * jax.nn.sigmoid(cc)
        return acc + jnp.sum(w_ref[pl.ds(r0, kc), :] * ca, axis=0, keepdims=True)

    acc = lax.fori_loop(0, d // kc, body, jnp.zeros((1, tn), F32))
    o_ref[...] = acc + b_ref[...]


def _ada(c_col, w, b_row, tn=1024, kc=256):
    d, n = w.shape
    return pl.pallas_call(
        functools.partial(_ada_kernel, kc=kc),
        grid=(n // tn,),
        in_specs=[pl.BlockSpec((d, 1), lambda j: (0, 0)),
                  pl.BlockSpec((d, tn), lambda j: (0, j)),
                  pl.BlockSpec((1, tn), lambda j: (0, j))],
        out_specs=pl.BlockSpec((1, tn), lambda j: (0, j)),
        out_shape=jax.ShapeDtypeStruct((1, n), F32),
        compiler_params=_params(("arbitrary",)),
        name="ada",
    )(c_col, w, b_row)


def _rms_mod_kernel(x_ref, g_ref, sc_ref, sh_ref, o_ref):
    x = x_ref[...]
    y = x * lax.rsqrt(jnp.mean(x * x, axis=-1, keepdims=True) + EPS)
    o_ref[...] = (y * g_ref[...] * (1.0 + sc_ref[...]) + sh_ref[...]).astype(o_ref.dtype)


def _rms_mod(x2, g, sc, sh, out_dtype, tm=512):
    t, d = x2.shape
    row = pl.BlockSpec((1, d), lambda i: (0, 0))
    return pl.pallas_call(
        _rms_mod_kernel,
        grid=(t // tm,),
        in_specs=[pl.BlockSpec((tm, d), lambda i: (i, 0)), row, row, row],
        out_specs=pl.BlockSpec((tm, d), lambda i: (i, 0)),
        out_shape=jax.ShapeDtypeStruct((t, d), out_dtype),
        compiler_params=_params(("parallel",)),
        name="rms_mod",
    )(x2, g, sc, sh)


def _mm_kernel(a_ref, b_ref, o_ref):
    o_ref[...] = jnp.dot(a_ref[...], b_ref[...], preferred_element_type=F32).astype(o_ref.dtype)


def _matmul(a, b, out_dtype, tm, tn, name):
    m, k = a.shape
    n = b.shape[1]
    return pl.pallas_call(
        _mm_kernel,
        grid=(m // tm, n // tn),
        in_specs=[pl.BlockSpec((tm, k), lambda i, j: (i, 0)),
                  pl.BlockSpec((k, tn), lambda i, j: (0, j))],
        out_specs=pl.BlockSpec((tm, tn), lambda i, j: (i, j)),
        out_shape=jax.ShapeDtypeStruct((m, n), out_dtype),
        compiler_params=_params(("parallel", "arbitrary")),
        name=name,
    )(a, b)


def _sortable(x):
    bits = pltpu.bitcast(x, I32)
    return bits ^ ((bits >> 31) & 0x7FFFFFFF)


def _unsortable(k):
    return pltpu.bitcast(k ^ ((k >> 31) & 0x7FFFFFFF), F32)


def _select_kernel(qi_ref, wi_ref, kit_ref, mask_ref, key_sc, qh_sc, wb_sc, gm_sc, *, n_sel):
    nkt, tq, tk = key_sc.shape
    nsub = tk // LANES
    ngrp = gm_sc.shape[0]
    npc = tq // SEL_ROWS
    pieces = [slice(p * SEL_ROWS, (p + 1) * SEL_ROWS) for p in range(npc)]
    i = pl.program_id(0)
    q0 = i * tq
    nvis = (q0 + tq + tk - 1) // tk

    for h in range(H_IDX):
        qh_sc[h * tq:(h + 1) * tq, :] = qi_ref[:, h * D_IDX:(h + 1) * D_IDX]
        wb_sc[h] = jnp.broadcast_to(wi_ref[:, h:h + 1], (tq, LANES))

    row = lax.broadcasted_iota(I32, (tq, 1), 0) + q0
    limit = (row // CHUNK + 1) * CHUNK
    lane = lax.broadcasted_iota(I32, (tq, LANES), 1)
    lane_p = lax.broadcasted_iota(I32, (SEL_ROWS, LANES), 1)
    limits = [((lax.broadcasted_iota(I32, (SEL_ROWS, 1), 0) + (q0 + p * SEL_ROWS)) // CHUNK + 1) * CHUNK
              for p in range(npc)]

    def score_tile(kt, gmax):
        k_t = kit_ref[kt]
        accs = [jnp.zeros((tq, LANES), F32) for _ in range(nsub)]
        s_all = jnp.dot(qh_sc[...], k_t, preferred_element_type=F32)
        for h in range(H_IDX):
            s = s_all[h * tq:(h + 1) * tq, :]
            wb = wb_sc[h]
            for a in range(nsub):
                accs[a] = accs[a] + wb * jnp.maximum(s[:, a * LANES:(a + 1) * LANES], 0.0)
        gmax = list(gmax)
        for a in range(nsub):
            col = lane + (kt * tk + a * LANES)
            key = jnp.where(col < limit, _sortable(accs[a]), INT_MIN)
            key_sc[kt, :, a * LANES:(a + 1) * LANES] = key
            gmax[a % ngrp] = jnp.maximum(gmax[a % ngrp], key)
        return tuple(gmax)

    gmax = lax.fori_loop(0, nvis, score_tile,
                         tuple(jnp.full((tq, LANES), INT_MIN, I32) for _ in range(ngrp)))
    for g in range(ngrp):
        gm_sc[g] = gmax[g]

    def count(preds):
        def body(kt, cs):
            cs = list(cs)
            for p in range(npc):
                for a in range(nsub):
                    kk = key_sc[kt, pieces[p], a * LANES:(a + 1) * LANES]
                    cs[p] = cs[p] + preds[p](kk, lane_p + (kt * tk + a * LANES)).astype(I32)
            return tuple(cs)
        cs = lax.fori_loop(0, nvis, body, tuple(jnp.zeros((SEL_ROWS, LANES), I32) for _ in range(npc)))
        return [jnp.sum(c, axis=1, keepdims=True) for c in cs]

    few, st0 = [], []
    for p in range(npc):
        gmin, ghi = gm_sc[0, pieces[p], :], gm_sc[0, pieces[p], :]
        for g in range(1, ngrp):
            gmin = jnp.minimum(gmin, gm_sc[g, pieces[p], :])
            ghi = jnp.maximum(ghi, gm_sc[g, pieces[p], :])
        lo0 = jnp.maximum(jnp.min(gmin, axis=1, keepdims=True), INT_MIN + 1)
        hi0 = jnp.max(ghi, axis=1, keepdims=True)
        fw = limits[p] < n_sel
        few.append(fw)
        st0.append((jnp.where(fw, INT_MIN + 1, lo0), jnp.where(fw, INT_MIN + 1, hi0),
                    jnp.full((SEL_ROWS, 1), 2 ** 30, I32), jnp.zeros((SEL_ROWS, 1), I32)))

    def bis_cond(sts):
        act = [jnp.max((st[0] < st[1]).astype(I32)) for st in sts]
        return functools.reduce(jnp.maximum, act) > 0

    def midpoint(lo, hi):
        mid = (lo | hi) - ((lo ^ hi) >> 1)
        fin = lambda k: jnp.clip(k, -KEY_FMAX - 1, KEY_FMAX)
        vmid = _sortable(0.5 * _unsortable(fin(lo)) + 0.5 * _unsortable(fin(hi)))
        return jnp.where((vmid > lo) & (vmid <= hi), vmid, mid)

    def bis_body(sts):
        mids = [midpoint(st[0], st[1]) for st in sts]
        cs = count([(lambda kk, col, m=m: kk >= m) for m in mids])
        out = []
        for (lo, hi, c_lo, c_hi1), mid, c in zip(sts, mids, cs):
            active = lo < hi
            up = active & (c >= n_sel)
            dn = active & (c < n_sel)
            hit = active & (c == n_sel)
            out.append((jnp.where(up, mid, lo), jnp.where(dn, mid - 1, jnp.where(hit, mid, hi)),
                        jnp.where(up, c, c_lo), jnp.where(dn, c, c_hi1)))
        return tuple(out)

    sts = lax.while_loop(bis_cond, bis_body, tuple(st0))
    thr = [st[0] for st in sts]

    excess = [(st[2] > n_sel) & jnp.logical_not(fw) for st, fw in zip(sts, few)]
    need = [n_sel - st[3] for st in sts]
    ncol = nkt * tk

    def tie_cut():
        nbits = max(1, int(math.ceil(math.log2(ncol))))

        def cut_body(b, ps):
            cands = [p | jnp.left_shift(jnp.int32(1), nbits - 1 - b) for p in ps]
            cnts = count([(lambda kk, col, t=t, cd=cd: (kk == t) & (col < cd)) for t, cd in zip(thr, cands)])
            return tuple(jnp.where(cnt < nd, cd, p) for cnt, nd, cd, p in zip(cnts, need, cands, ps))

        ps = lax.fori_loop(0, nbits, cut_body, tuple(jnp.zeros((SEL_ROWS, 1), I32) for _ in range(npc)))
        return tuple(p + 1 for p in ps)

    any_excess = functools.reduce(jnp.maximum, [jnp.max(e.astype(I32)) for e in excess]) > 0
    cuts = lax.cond(any_excess, tie_cut, lambda: tuple(jnp.full((SEL_ROWS, 1), ncol, I32) for _ in range(npc)))
    cuts = [jnp.where(e, ct, ncol) for e, ct in zip(excess, cuts)]

    def write_tile(kt, carry):
        col = lax.broadcasted_iota(I32, (SEL_ROWS, tk), 1) + kt * tk
        for p in range(npc):
            kk = key_sc[kt, pieces[p], :]
            sel = (kk > thr[p]) | ((kk == thr[p]) & (col < cuts[p]))
            mask_ref[kt, pieces[p], :] = jnp.where(sel, 0.0, NEG).astype(mask_ref.dtype)
        return carry

    lax.fori_loop(0, nvis, write_tile, 0)

    def fill_tile(kt, carry):
        mask_ref[kt] = jnp.full((tq, tk), NEG, mask_ref.dtype)
        return carry

    lax.fori_loop(nvis, nkt, fill_tile, 0)


def _select(proj, wi, kit3, n_sel, qi_col_block):
    s = proj.shape[0]
    nkt, _, tk = kit3.shape
    tq = SEL_TQ
    nqb = s // tq
    ngrp = -(-n_sel // LANES)
    assert ngrp <= tk // LANES and tq % SEL_ROWS == 0
    return pl.pallas_call(
        functools.partial(_select_kernel, n_sel=n_sel),
        grid=(nqb,),
        in_specs=[pl.BlockSpec((tq, W_IDX_Q), lambda i: (i, qi_col_block)),
                  pl.BlockSpec((tq, H_IDX), lambda i: (i, 0)),
                  pl.BlockSpec((nkt, D_IDX, tk), lambda i: (0, 0, 0))],
        out_specs=pl.BlockSpec((None, nkt, tq, tk), lambda i: (i, 0, 0, 0)),
        out_shape=jax.ShapeDtypeStruct((nqb, nkt, tq, tk), BF16),
        scratch_shapes=[pltpu.VMEM((nkt, tq, tk), I32),
                        pltpu.VMEM((H_IDX * tq, D_IDX), BF16),
                        pltpu.VMEM((H_IDX, tq, LANES), F32),
                        pltpu.VMEM((ngrp, tq, LANES), I32)],
        compiler_params=_params(("arbitrary",)),
        name="dsa_select",
    )(proj, wi, kit3)


def _dsa_attn_kernel(qb_ref, kb_ref, first_ref, last_ref, slab_ref,
                     q_ref, kt_ref, v_ref, mask_ref, gen_ref, far_ref, o_ref,
                     m_sc, l_sc, acc_sc, band_ref):
    step = pl.program_id(0)
    tq = q_ref.shape[0]
    tk = kt_ref.shape[1]
    nslab = tk // LANES
    slab0 = slab_ref[step]

    npc = tq // ATT_ROWS

    @pl.when(step == 0)
    def _():
        win = LANES + ATT_ROWS
        for h in range(H_A):
            for v in range(band_ref.shape[1]):
                rows = jnp.broadcast_to(gen_ref[h:h + 1, v * LANES:v * LANES + win], (ATT_ROWS, win))
                rows = pltpu.roll(rows, win - (ATT_ROWS - 1), 1, stride=1, stride_axis=0)
                band_ref[h, v] = rows[:, 0:LANES]

    @pl.when(first_ref[step] == 1)
    def _():
        m_sc[...] = jnp.full(m_sc.shape, NEG, F32)
        l_sc[...] = jnp.zeros(l_sc.shape, F32)
        acc_sc[...] = jnp.zeros(acc_sc.shape, F32)

    def tile(near):
        mask = jnp.concatenate(
            [jnp.concatenate([mask_ref[r, j] for j in range(mask_ref.shape[1])], axis=1)
             for r in range(mask_ref.shape[0])], axis=0).astype(F32)
        ones = jnp.ones((tk, HD_A), BF16)
        for h in range(H_A):
            hs = slice(h * HD_A, (h + 1) * HD_A)
            s = jnp.dot(q_ref[:, hs], kt_ref[hs, :], preferred_element_type=F32) + mask
            if near:
                back = ATT_ROWS // LANES
                s = s + jnp.concatenate(
                    [jnp.concatenate([band_ref[h, slab0 + a + (npc - 1 - p) * back] for a in range(nslab)], axis=1)
                     for p in range(npc)], axis=0)
                shift = 0.0
            else:
                shift = far_ref[h]
            smax = s[:, 0:LANES]
            for a in range(1, nslab):
                smax = jnp.maximum(smax, s[:, a * LANES:(a + 1) * LANES])
            m_old = m_sc[h]
            m_new = jnp.maximum(m_old, jnp.max(smax, axis=1, keepdims=True) + shift)
            alpha = jnp.exp2(m_old - m_new)
            mm = m_new - shift
            p = jnp.concatenate([jnp.exp2(s[:, a * LANES:(a + 1) * LANES] - mm) for a in range(nslab)],
                                axis=1).astype(BF16)
            pv = jnp.dot(p, jnp.concatenate([v_ref[:, hs], ones], axis=1), preferred_element_type=F32)
            acc_sc[:, hs] = alpha * acc_sc[:, hs] + pv[:, 0:HD_A]
            l_sc[h] = alpha * l_sc[h] + pv[:, HD_A:2 * HD_A]
            m_sc[h] = m_new

    @pl.when(slab0 >= 0)
    def _():
        tile(True)

    @pl.when(slab0 < 0)
    def _():
        tile(False)

    @pl.when(last_ref[step] == 1)
    def _():
        for h in range(H_A):
            hs = slice(h * HD_A, (h + 1) * HD_A)
            o_ref[:, hs] = (acc_sc[:, hs] / l_sc[h]).astype(o_ref.dtype)


def _t5_bucket_np(rel):
    nb = N_T5_BUCKETS // 2
    ret = (rel > 0).astype(np.int32) * nb
    n = np.abs(rel)
    max_exact = nb // 2
    nf = np.maximum(n, 1).astype(np.float32)
    large = max_exact + (np.log(nf / np.float32(max_exact)) / np.float32(math.log(T5_MAX_DIST / max_exact))
                         * np.float32(nb - max_exact)).astype(np.int32)
    large = np.minimum(large, nb - 1)
    return ret + np.where(n < max_exact, n, large)


def _dsa_plan(s, tq, tk):
    nqb = s // tq
    d_all = np.arange(-(s - 1), CHUNK, dtype=np.int64)
    b_all = _t5_bucket_np(d_all)
    far_bucket = int(b_all[0])
    varying = np.nonzero(b_all != far_bucket)[0]
    d_lo = int(d_all[varying[0]])
    o_min = min(int(math.ceil((d_lo - (tk - 1)) / LANES)) * LANES, 0)
    n_slabs = (-o_min + tk) // LANES
    qb, kb, first, last, slab = [], [], [], [], []
    for i in range(nqb):
        q0 = i * tq
        nvis = (q0 + tq + tk - 1) // tk
        for j in range(nvis):
            o = j * tk - q0
            qb.append(i)
            kb.append(j)
            first.append(1 if j == 0 else 0)
            last.append(1 if j == nvis - 1 else 0)
            slab.append(-1 if o < o_min else (o - o_min) // LANES)
    n_tab = n_slabs + (tq - ATT_ROWS) // LANES
    o_lo = o_min - (tq - ATT_ROWS)
    x = np.arange(n_tab * LANES + ATT_ROWS)
    g_bucket = _t5_bucket_np(np.clip(o_lo + x - (ATT_ROWS - 1), -(s - 1), CHUNK))
    plan = tuple(np.asarray(a, np.int32) for a in (qb, kb, first, last, slab))
    return plan, g_bucket, far_bucket, n_tab


def _dsa_attn(proj, kt, mask4, t5_table, q_col, v_col):
    s = proj.shape[0]
    tq, tk = ATT_TQ, ATT_TK
    (qb, kb, first, last, slab), g_bucket, far_bucket, n_tab = _dsa_plan(s, tq, tk)
    nsteps = len(qb)
    t5l = t5_table * LOG2E
    gen = jnp.transpose(t5l[g_bucket], (1, 0))
    far = t5l[far_bucket]
    rq = tq // SEL_TQ
    grid_spec = pltpu.PrefetchScalarGridSpec(
        num_scalar_prefetch=5,
        grid=(nsteps,),
        in_specs=[
            pl.BlockSpec((tq, W_A), lambda t, qb, kb, f, l, sl: (qb[t], q_col)),
            pl.BlockSpec((W_A, tk), lambda t, qb, kb, f, l, sl: (0, kb[t])),
            pl.BlockSpec((tk, W_A), lambda t, qb, kb, f, l, sl: (kb[t], v_col)),
            pl.BlockSpec((rq, tk // SEL_TK, SEL_TQ, SEL_TK), lambda t, qb, kb, f, l, sl: (qb[t], kb[t], 0, 0)),
            pl.BlockSpec((H_A, n_tab * LANES + ATT_ROWS), lambda t, qb, kb, f, l, sl: (0, 0)),
            pl.BlockSpec(memory_space=pltpu.SMEM),
        ],
        out_specs=pl.BlockSpec((tq, W_A), lambda t, qb, kb, f, l, sl: (qb[t], 0)),
        scratch_shapes=[pltpu.VMEM((H_A, tq, LANES), F32),
                        pltpu.VMEM((H_A, tq, LANES), F32),
                        pltpu.VMEM((tq, W_A), F32),
                        pltpu.VMEM((H_A, n_tab, ATT_ROWS, LANES), F32)],
    )
    return pl.pallas_call(
        _dsa_attn_kernel,
        grid_spec=grid_spec,
        out_shape=jax.ShapeDtypeStruct((s, W_A), BF16),
        compiler_params=_params(("arbitrary",)),
        name="dsa_attn",
    )(jnp.asarray(qb), jnp.asarray(kb), jnp.asarray(first), jnp.asarray(last), jnp.asarray(slab),
      proj, kt, proj, mask4, gen, far)


def _band_kernel(q_ref, k0_ref, k1_ref, k2_ref, v0_ref, v1_ref, v2_ref, gen_ref, o_ref, bias_ref):
    i = pl.program_id(0)
    tq = q_ref.shape[0]
    wk = 3 * tq

    @pl.when(i == 0)
    def _():
        r = lax.broadcasted_iota(I32, (tq, wk), 0)
        c = lax.broadcasted_iota(I32, (tq, wk), 1)
        dchunk = (c // CHUNK - 2 * tq // CHUNK) - r // CHUNK
        in_band = (dchunk <= 0) & (dchunk >= -N_LEFT_CHUNKS)
        win = wk + tq
        for h in range(H_B):
            rows = jnp.broadcast_to(gen_ref[h:h + 1, :], (tq, win))
            rows = pltpu.roll(rows, win - (tq - 1), 1, stride=1, stride_axis=0)
            bias_ref[h] = jnp.where(in_band, rows[:, 0:wk], NEG)
    col = lax.broadcasted_iota(I32, (tq, wk), 1)
    start_mask = jnp.where(col + (i - 2) * tq >= 0, 0.0, NEG)
    for h in range(H_B):
        hs = slice(h * HD_B, (h + 1) * HD_B)
        kc = jnp.concatenate([k0_ref[hs, :], k1_ref[hs, :], k2_ref[hs, :]], axis=1)
        vc = jnp.concatenate([v0_ref[:, hs], v1_ref[:, hs], v2_ref[:, hs]], axis=0)
        s = jnp.dot(q_ref[:, hs], kc, preferred_element_type=F32) + bias_ref[h] + start_mask
        m = jnp.max(s, axis=1, keepdims=True)
        p = jnp.exp2(s - m)
        l = jnp.sum(p, axis=1, keepdims=True)
        pv = jnp.dot(p.astype(BF16), vc, preferred_element_type=F32)
        o_ref[:, hs] = (pv / l).astype(o_ref.dtype)


def _band_gen(rel_table, tq):
    assert 2 * tq >= N_LEFT_CHUNKS * CHUNK and tq % CHUNK == 0
    x = np.arange(4 * tq)
    idx = np.clip(2 * tq + (tq - 1) - x, -REL_CLIP, REL_CLIP) + REL_CLIP
    return rel_table[:, idx] * LOG2E


def _band_attn(proj, kt, rel_table, q_col, v_col):
    s = proj.shape[0]
    tq = BAND_TQ
    gen = _band_gen(rel_table, tq)

    def kspec(back):
        return pl.BlockSpec((W_B, tq), lambda i: (0, jnp.maximum(i - back, 0)))

    def vspec(back):
        return pl.BlockSpec((tq, W_B), lambda i: (jnp.maximum(i - back, 0), v_col))

    return pl.pallas_call(
        _band_kernel,
        grid=(s // tq,),
        in_specs=[pl.BlockSpec((tq, W_B), lambda i: (i, q_col)),
                  kspec(2), kspec(1), kspec(0), vspec(2), vspec(1), vspec(0),
                  pl.BlockSpec((H_B, 4 * tq), lambda i: (0, 0))],
        out_specs=pl.BlockSpec((tq, W_B), lambda i: (i, 0)),
        out_shape=jax.ShapeDtypeStruct((s, W_B), BF16),
        scratch_shapes=[pltpu.VMEM((H_B, tq, 3 * tq), F32)],
        compiler_params=_params(("arbitrary",)),
        name="band_attn",
    )(proj, kt, kt, kt, proj, proj, proj, gen)


def _merge_kernel(ya_ref, yb_ref, ga_ref, gb_ref, wa_ref, wb_ref, o_ref):
    ua = jnp.dot(ya_ref[...], wa_ref[...], preferred_element_type=F32)
    ub = jnp.dot(yb_ref[...], wb_ref[...], preferred_element_type=F32)
    o_ref[...] = (jax.nn.sigmoid(ga_ref[...]) * ua + jax.nn.sigmoid(gb_ref[...]) * ub).astype(o_ref.dtype)


def _merge(ya, yb, gates, wa, wb, tm=256):
    s, d = ya.shape[0], wa.shape[1]
    return pl.pallas_call(
        _merge_kernel,
        grid=(s // tm,),
        in_specs=[pl.BlockSpec((tm, W_A), lambda i: (i, 0)),
                  pl.BlockSpec((tm, W_B), lambda i: (i, 0)),
                  pl.BlockSpec((tm, d), lambda i: (i, 0)),
                  pl.BlockSpec((tm, d), lambda i: (i, 1)),
                  pl.BlockSpec((W_A, d), lambda i: (0, 0)),
                  pl.BlockSpec((W_B, d), lambda i: (0, 0))],
        out_specs=pl.BlockSpec((tm, d), lambda i: (i, 0)),
        out_shape=jax.ShapeDtypeStruct((s, d), BF16),
        compiler_params=_params(("parallel",)),
        name="merge",
    )(ya, yb, gates, gates, wa, wb)


def _post_kernel(x_ref, m_ref, wo_ref, g1_ref, gn_ref, sc_ref, sh_ref, x1_ref, h2_ref):
    x1 = x_ref[...] + g1_ref[...] * jnp.dot(m_ref[...], wo_ref[...], preferred_element_type=F32)
    x1_ref[...] = x1
    y = x1 * lax.rsqrt(jnp.mean(x1 * x1, axis=-1, keepdims=True) + EPS)
    _store_rows(h2_ref, y * gn_ref[...] * (1.0 + sc_ref[...]) + sh_ref[...])


def _post(x2, merged, wo, g1, gn, sc, sh, tm=256):
    t, d = x2.shape
    row = pl.BlockSpec((1, d), lambda i: (0, 0))
    tile = pl.BlockSpec((tm, d), lambda i: (i, 0))
    return pl.pallas_call(
        _post_kernel,
        grid=(t // tm,),
        in_specs=[tile, tile, pl.BlockSpec((d, d), lambda i: (0, 0)), row, row, row, row],
        out_specs=[tile, pl.BlockSpec((tm * (d // LANES), LANES), lambda i: (i, 0))],
        out_shape=[jax.ShapeDtypeStruct((t, d), F32), jax.ShapeDtypeStruct((t * (d // LANES), LANES), F32)],
        compiler_params=_params(("parallel",)),
        name="post",
    )(x2, merged, wo, g1, gn, sc, sh)


def _router_kernel(h_ref, wr_ref, br_ref, lg_ref):
    lg_ref[...] = jnp.dot(_load_rows(h_ref, wr_ref.shape[0] // LANES), wr_ref[...], preferred_element_type=F32,
                          precision=lax.Precision.HIGHEST) + br_ref[...]


def _router(h2, w_router, b_router, tm=512):
    d, nr = w_router.shape
    t = h2.shape[0] // (d // LANES)
    return pl.pallas_call(
        _router_kernel,
        grid=(t // tm,),
        in_specs=[pl.BlockSpec((tm * (d // LANES), LANES), lambda i: (i, 0)),
                  pl.BlockSpec((d, nr), lambda i: (0, 0)), pl.BlockSpec((1, nr), lambda i: (0, 0))],
        out_specs=pl.BlockSpec((tm, nr), lambda i: (i, 0)),
        out_shape=jax.ShapeDtypeStruct((t, nr), F32),
        compiler_params=_params(("parallel",)),
        name="router",
    )(h2, w_router, b_router)


def _moe_kernel(tok_ref, dst_ref, be_ref, nu_ref,
                h_hbm, w1_ref, w3_ref, w2_ref, z_hbm,
                x0, x1, y0, y1, w1b, w3b, w2b, gsem, ssem):
    b = pl.program_id(0)
    c = w1b.shape[0] // LANES
    bm = x0.shape[0] // c
    nu = nu_ref[0]
    xs, ys = (x0, x1), (y0, y1)

    def gather_copy(sl, r, t):
        return pltpu.make_async_copy(h_hbm.at[pl.ds(t * c, c), :], xs[sl].at[pl.ds(r * c, c), :], gsem.at[sl])

    def scatter_copy(sl, r, d):
        return pltpu.make_async_copy(ys[sl].at[pl.ds(r * c, c), :], z_hbm.at[pl.ds(d * c, c), :], ssem.at[sl])

    def gather_start(blk, sl):
        for r in range(bm):
            gather_copy(sl, r, tok_ref[blk * bm + r]).start()

    def scatter_start(blk, sl):
        for r in range(bm):
            scatter_copy(sl, r, dst_ref[(blk + 2) * bm + r]).start()

    def gather_wait(sl):
        for r in range(bm):
            gather_copy(sl, r, 0).wait()

    def scatter_wait(sl):
        for r in range(bm):
            scatter_copy(sl, r, 0).wait()

    @pl.when(b == 0)
    def _():
        y0[...] = jnp.zeros(y0.shape, F32)
        y1[...] = jnp.zeros(y1.shape, F32)
        scatter_start(-2, 0)
        gather_start(0, 0)

    @pl.when((b < nu) & ((b == 0) | (be_ref[b] != be_ref[jnp.maximum(b - 1, 0)])))
    def _():
        w1b[...] = w1_ref[0].astype(BF16)
        w3b[...] = w3_ref[0].astype(BF16)
        w2b[...] = w2_ref[0].astype(BF16)

    def main(sl):
        gather_wait(sl)
        scatter_wait(sl)
        gather_start(b + 1, 1 - sl)
        scatter_start(b - 1, 1 - sl)
        x = _load_rows(xs[sl], c).astype(BF16)
        a1 = jnp.dot(x, w1b[...], preferred_element_type=F32)
        a3 = jnp.dot(x, w3b[...], preferred_element_type=F32)
        a = (a1 * jax.nn.sigmoid(a1)) * a3
        y = jnp.dot(a.astype(BF16), w2b[...], preferred_element_type=F32)
        _store_rows(ys[sl], y)

    def drain(sl):
        gather_wait(sl)
        scatter_wait(sl)
        scatter_start(b - 1, 1 - sl)
        scatter_wait(1 - sl)

    for sl in range(2):
        @pl.when((b < nu) & (b % 2 == sl))
        def _():
            main(sl)

        @pl.when((b == nu) & (b % 2 == sl))
        def _():
            drain(sl)


def _moe(h2, tok_buf, dst_buf, blk_e, n_used, w1, w3, w2):
    d, dff = w1.shape[1:]
    c = d // LANES
    t = h2.shape[0] // c
    bm = MOE_BM
    nsteps = blk_e.shape[0]
    assert tok_buf.shape[0] == nsteps * bm and dst_buf.shape[0] == (nsteps + 2) * bm
    grid_spec = pltpu.PrefetchScalarGridSpec(
        num_scalar_prefetch=4,
        grid=(nsteps,),
        in_specs=[
            pl.BlockSpec(memory_space=pl.ANY),
            pl.BlockSpec((1, d, dff), lambda b, tok, dst, be, nu: (be[b], 0, 0)),
            pl.BlockSpec((1, d, dff), lambda b, tok, dst, be, nu: (be[b], 0, 0)),
            pl.BlockSpec((1, dff, d), lambda b, tok, dst, be, nu: (be[b], 0, 0)),
        ],
        out_specs=pl.BlockSpec(memory_space=pl.ANY),
        scratch_shapes=[pltpu.VMEM((bm * c, LANES), F32), pltpu.VMEM((bm * c, LANES), F32),
                        pltpu.VMEM((bm * c, LANES), F32), pltpu.VMEM((bm * c, LANES), F32),
                        pltpu.VMEM((d, dff), BF16), pltpu.VMEM((d, dff), BF16), pltpu.VMEM((dff, d), BF16),
                        pltpu.SemaphoreType.DMA((2,)), pltpu.SemaphoreType.DMA((2,))],
    )
    return pl.pallas_call(
        _moe_kernel,
        grid_spec=grid_spec,
        out_shape=jax.ShapeDtypeStruct(((TOPK_IN_GROUP * t + 2 * bm) * c, LANES), F32),
        compiler_params=pltpu.CompilerParams(dimension_semantics=("arbitrary",),
                                             vmem_limit_bytes=VMEM_LIMIT, has_side_effects=True),
        name="moe",
    )(tok_buf, dst_buf, blk_e, n_used, h2, w1, w3, w2)


def _final_kernel(x1_ref, z0_ref, z1_ref, w_ref, g2_ref, gn_ref, o_ref):
    c = x1_ref.shape[1] // LANES
    moe = w_ref[:, 0:1] * _load_rows(z0_ref, c) + w_ref[:, 1:2] * _load_rows(z1_ref, c)
    x2 = x1_ref[...] + g2_ref[...] * moe
    y = x2 * lax.rsqrt(jnp.mean(x2 * x2, axis=-1, keepdims=True) + EPS)
    o_ref[...] = y * gn_ref[...]


def _final(x1, z, weight, g2, gn, tm=256):
    t, d = x1.shape
    row = pl.BlockSpec((1, d), lambda i: (0, 0))
    return pl.pallas_call(
        _final_kernel,
        grid=(t // tm,),
        in_specs=[pl.BlockSpec((tm, d), lambda i: (i, 0)),
                  pl.BlockSpec((tm * (d // LANES), LANES), lambda i: (i, 0)),
                  pl.BlockSpec((tm * (d // LANES), LANES), lambda i: (t // tm + i, 0)),
                  pl.BlockSpec((tm, TOPK_IN_GROUP), lambda i: (i, 0)), row, row],
        out_specs=pl.BlockSpec((tm, d), lambda i: (i, 0)),
        out_shape=jax.ShapeDtypeStruct((t, d), F32),
        compiler_params=_params(("parallel",)),
        name="final",
    )(x1, z, z, weight, g2, gn)


def _route(logits, t):
    gl = logits[:, :N_GROUPS]
    el = logits[:, N_GROUPS:N_GROUPS + N_EXPERTS].reshape(t, N_GROUPS, EXP_PER_GROUP)
    g_prob = jax.nn.softmax(gl, axis=-1)
    grp = jnp.argmax(gl, axis=-1).astype(I32)
    p_grp = jnp.take_along_axis(g_prob, grp[:, None], axis=-1)[:, 0]
    e_in = jnp.take_along_axis(el, grp[:, None, None], axis=1)[:, 0]
    top_v, top_i = lax.top_k(e_in, TOPK_IN_GROUP)
    p_in = jax.nn.softmax(top_v, axis=-1)
    expert = grp[:, None] * EXP_PER_GROUP + top_i.astype(I32)
    weight = p_grp[:, None] * p_in

    bm = MOE_BM
    m = t * TOPK_IN_GROUP
    e_flat = expert.reshape(m)
    order = jnp.argsort(e_flat, stable=True).astype(I32)
    counts = jnp.bincount(e_flat, length=N_EXPERTS).astype(I32)
    start = jnp.cumsum(counts) - counts
    padded = ((counts + bm - 1) // bm) * bm
    pend = jnp.cumsum(padded)
    pstart = pend - padded
    nb = m // bm + N_EXPERTS + 1
    blk_e = jnp.minimum(jnp.searchsorted(pend, jnp.arange(nb, dtype=I32) * bm, side='right'),
                        N_EXPERTS - 1).astype(I32)
    n_used = (pend[-1] // bm).astype(I32).reshape(1)
    pos = jnp.arange(nb * bm, dtype=I32).reshape(nb, bm)
    src = (start[blk_e] - pstart[blk_e])[:, None] + pos
    valid = (src < (start + counts)[blk_e][:, None]) & (pos < pend[-1])
    a = order[jnp.clip(src, 0, m - 1)]
    tok = a // TOPK_IN_GROUP
    tok_buf = jnp.where(valid, tok, 0).reshape(-1)
    spare = m + pos % (2 * bm)
    dst_buf = jnp.where(valid, (a % TOPK_IN_GROUP) * t + tok, spare).reshape(-1)
    dst_buf = jnp.concatenate([m + jnp.arange(2 * bm, dtype=I32), dst_buf])
    return tok_buf, dst_buf, blk_e, n_used, weight


def kernel(x, c, w_ada, b_ada, norm_mix, w_in, t5_table, rel_table, w_up_a, w_up_b, w_o, norm_ffn,
           w_rg, b_rg, w_re, b_re, w1, w3, w2, norm_final):
    bn, s, d = x.shape
    assert bn == 1 and w_ada.shape[0] == 1
    assert s % 1024 == 0
    t = bn * s
    x2 = x.reshape(t, d)
    n_sel = min(TOPK_MAX, s // 4)

    mod = _ada(c.reshape(d, 1), w_ada[0], b_ada[0].reshape(1, 6 * d))
    sh1, sc1, g1, sh2, sc2, g2 = [mod[:, i * d:(i + 1) * d] for i in range(6)]

    h = _rms_mod(x2, norm_mix[0].reshape(1, d), sc1, sh1, BF16)

    cols = np.cumsum([0, W_A, W_A, W_A, W_IDX_Q, D_IDX, H_IDX, W_B, W_B, W_B, d, d])
    wsl = [w_in[0][:, cols[i]:cols[i + 1]] for i in range(11)]
    wqa, wka, wva, wqi, wki, wwi, wqb, wkb, wvb, wga, wgb = wsl
    w_main = jnp.concatenate([wqa * (LOG2E / math.sqrt(HD_A)), wka, wva, wqi,
                              wqb * (LOG2E / math.sqrt(HD_B)), wkb, wvb], axis=1).astype(BF16)
    w_gate = jnp.concatenate([wga, wgb], axis=1).astype(BF16)
    w_idx = jnp.concatenate([wki, wwi, jnp.zeros((d, LANES - D_IDX - H_IDX), F32)], axis=1).astype(BF16)

    proj = _matmul(h, w_main, BF16, 1024, 1024, "proj_main")
    gates = _matmul(h, w_gate, F32, 1024, 1024, "proj_gate")
    idx = _matmul(h, w_idx, F32, 1024, LANES, "proj_idx")
    nkt = s // SEL_TK
    kit3 = jnp.transpose(idx[:, :D_IDX].astype(BF16).reshape(nkt, SEL_TK, D_IDX), (0, 2, 1))
    wi = idx[:, D_IDX:D_IDX + H_IDX] * ((H_IDX ** -0.5) * (D_IDX ** -0.5))
    kt_a = jnp.transpose(proj[:, W_A:2 * W_A])
    kt_b = jnp.transpose(proj[:, 5 * W_A:6 * W_A])

    mask4 = _select(proj, wi, kit3, n_sel, qi_col_block=3)
    y_a = _dsa_attn(proj, kt_a, mask4, t5_table, q_col=0, v_col=2)
    y_b = _band_attn(proj, kt_b, rel_table[0], q_col=4, v_col=6)

    merged = _merge(y_a, y_b, gates, w_up_a[0].astype(BF16), w_up_b[0].astype(BF16))

    nr = LANES
    w_router = jnp.concatenate([w_rg[0], w_re[0], jnp.zeros((d, nr - N_GROUPS - N_EXPERTS), F32)], axis=1)
    b_router = jnp.concatenate([b_rg[0], b_re[0], jnp.zeros((nr - N_GROUPS - N_EXPERTS,), F32)]).reshape(1, nr)
    x1, h2 = _post(x2, merged, w_o[0].astype(BF16), g1, norm_ffn[0].reshape(1, d), sc2, sh2)
    logits = _router(h2, w_router, b_router)
    tok_buf, dst_buf, blk_e, n_used, weight = _route(logits, t)
    z = _moe(h2, tok_buf, dst_buf, blk_e, n_used, w1[0], w3[0], w2[0])
    out = _final(x1, z, weight, g2, norm_final.reshape(1, d))
    return out.reshape(bn, s, d)
```

```python
import functools
import math

import numpy as np
import jax
import jax.numpy as jnp
from jax import lax
from jax.experimental import pallas as pl
from jax.experimental.pallas import tpu as pltpu

F32 = jnp.float32
BF16 = jnp.bfloat16
I32 = jnp.int32

CHUNK = 64
EPS = 1e-6
H_A, HD_A = 8, 128
H_IDX, D_IDX = 16, 64
TOPK_MAX = 256
N_T5_BUCKETS = 32
T5_MAX_DIST = 1024
H_B, HD_B = 8, 128
N_LEFT_CHUNKS = 8
REL_CLIP = 128
N_GROUPS = 8
EXP_PER_GROUP = 8
N_EXPERTS = N_GROUPS * EXP_PER_GROUP
TOPK_IN_GROUP = 2

W_A = H_A * HD_A
W_B = H_B * HD_B
W_IDX_Q = H_IDX * D_IDX

NEG = -1e30
INT_MIN = -(2 ** 31)
KEY_FMAX = 0x7F7FFFFF
LOG2E = math.log2(math.e)

LANES = 128
VMEM_LIMIT = 56 * 1024 * 1024

SEL_TQ = 256
SEL_ROWS = 128
SEL_TK = 512
ATT_TQ = 512
ATT_ROWS = 256
ATT_TK = 1024
BAND_TQ = 256
MOE_BM = 128


def _params(sem, vmem=VMEM_LIMIT):
    return pltpu.CompilerParams(dimension_semantics=sem, vmem_limit_bytes=vmem)


def _load_rows(ref, c):
    rows = ref.shape[0] // c
    return jnp.concatenate([ref[pl.ds(k, rows, stride=c), :] for k in range(c)], axis=1)


def _store_rows(ref, val):
    rows = val.shape[0]
    c = ref.shape[0] // rows
    for k in range(c):
        ref[pl.ds(k, rows, stride=c), :] = val[:, k * LANES:(k + 1) * LANES]


def _ada_kernel(c_ref, w_ref, b_ref, o_ref, *, kc):
    d = w_ref.shape[0]
    tn = w_ref.shape[1]

    def body(k, acc):
        r0 = pl.multiple_of(k * kc, kc)
        cc = c_ref[pl.ds(r0, kc), :]
        ca = cc * jax.nn.sigmoid(cc)
        return acc + jnp.sum(w_ref[pl.ds(r0, kc), :] * ca, axis=0, keepdims=True)

    acc = lax.fori_loop(0, d // kc, body, jnp.zeros((1, tn), F32))
    o_ref[...] = acc + b_ref[...]


def _ada(c_col, w, b_row, tn=1024, kc=256):
    d, n = w.shape
    return pl.pallas_call(
        functools.partial(_ada_kernel, kc=kc),
        grid=(n // tn,),
        in_specs=[pl.BlockSpec((d, 1), lambda j: (0, 0)),
                  pl.BlockSpec((d, tn), lambda j: (0, j)),
                  pl.BlockSpec((1, tn), lambda j: (0, j))],
        out_specs=pl.BlockSpec((1, tn), lambda j: (0, j)),
        out_shape=jax.ShapeDtypeStruct((1, n), F32),
        compiler_params=_params(("arbitrary",)),
        name="ada",
    )(c_col, w, b_row)


def _rms_mod_kernel(x_ref, g_ref, sc_ref, sh_ref, o_ref):
    x = x_ref[...]
    y = x * lax.rsqrt(jnp.mean(x * x, axis=-1, keepdims=True) + EPS)
    o_ref[...] = (y * g_ref[...] * (1.0 + sc_ref[...]) + sh_ref[...]).astype(o_ref.dtype)


def _rms_mod(x2, g, sc, sh, out_dtype, tm=512):
    t, d = x2.shape
    row = pl.BlockSpec((1, d), lambda i: (0, 0))
    return pl.pallas_call(
        _rms_mod_kernel,
        grid=(t // tm,),
        in_specs=[pl.BlockSpec((tm, d), lambda i: (i, 0)), row, row, row],
        out_specs=pl.BlockSpec((tm, d), lambda i: (i, 0)),
        out_shape=jax.ShapeDtypeStruct((t, d), out_dtype),
        compiler_params=_params(("parallel",)),
        name="rms_mod",
    )(x2, g, sc, sh)


def _mm_kernel(a_ref, b_ref, o_ref):
    o_ref[...] = jnp.dot(a_ref[...], b_ref[...], preferred_element_type=F32).astype(o_ref.dtype)


def _matmul(a, b, out_dtype, tm, tn, name):
    m, k = a.shape
    n = b.shape[1]
    return pl.pallas_call(
        _mm_kernel,
        grid=(m // tm, n // tn),
        in_specs=[pl.BlockSpec((tm, k), lambda i, j: (i, 0)),
                  pl.BlockSpec((k, tn), lambda i, j: (0, j))],
        out_specs=pl.BlockSpec((tm, tn), lambda i, j: (i, j)),
        out_shape=jax.ShapeDtypeStruct((m, n), out_dtype),
        compiler_params=_params(("parallel", "arbitrary")),
        name=name,
    )(a, b)


def _sortable(x):
    bits = pltpu.bitcast(x, I32)
    return bits ^ ((bits >> 31) & 0x7FFFFFFF)


def _unsortable(k):
    return pltpu.bitcast(k ^ ((k >> 31) & 0x7FFFFFFF), F32)


def _select_kernel(qi_ref, wi_ref, kit_ref, mask_ref, key_sc, qh_sc, wb_sc, gm_sc, *, n_sel):
    nkt, tq, tk = key_sc.shape
    nsub = tk // LANES
    ngrp = gm_sc.shape[0]
    npc = tq // SEL_ROWS
    pieces = [slice(p * SEL_ROWS, (p + 1) * SEL_ROWS) for p in range(npc)]
    i = pl.program_id(0)
    q0 = i * tq
    nvis = (q0 + tq + tk - 1) // tk

    for h in range(H_IDX):
        qh_sc[h * tq:(h + 1) * tq, :] = qi_ref[:, h * D_IDX:(h + 1) * D_IDX]
        wb_sc[h] = jnp.broadcast_to(wi_ref[:, h:h + 1], (tq, LANES))

    row = lax.broadcasted_iota(I32, (tq, 1), 0) + q0
    limit = (row // CHUNK + 1) * CHUNK
    lane = lax.broadcasted_iota(I32, (tq, LANES), 1)
    lane_p = lax.broadcasted_iota(I32, (SEL_ROWS, LANES), 1)
    limits = [((lax.broadcasted_iota(I32, (SEL_ROWS, 1), 0) + (q0 + p * SEL_ROWS)) // CHUNK + 1) * CHUNK
              for p in range(npc)]

    def score_tile(kt, gmax):
        k_t = kit_ref[kt]
        accs = [jnp.zeros((tq, LANES), F32) for _ in range(nsub)]
        s_all = jnp.dot(qh_sc[...], k_t, preferred_element_type=F32)
        for h in range(H_IDX):
            s = s_all[h * tq:(h + 1) * tq, :]
            wb = wb_sc[h]
            for a in range(nsub):
                accs[a] = accs[a] + wb * jnp.maximum(s[:, a * LANES:(a + 1) * LANES], 0.0)
        gmax = list(gmax)
        for a in range(nsub):
            col = lane + (kt * tk + a * LANES)
            key = jnp.where(col < limit, _sortable(accs[a]), INT_MIN)
            key_sc[kt, :, a * LANES:(a + 1) * LANES] = key
            gmax[a % ngrp] = jnp.maximum(gmax[a % ngrp], key)
        return tuple(gmax)

    gmax = lax.fori_loop(0, nvis, score_tile,
                         tuple(jnp.full((tq, LANES), INT_MIN, I32) for _ in range(ngrp)))
    for g in range(ngrp):
        gm_sc[g] = gmax[g]

    def count(preds):
        def body(kt, cs):
            cs = list(cs)
            for p in range(npc):
                for a in range(nsub):
                    kk = key_sc[kt, pieces[p], a * LANES:(a + 1) * LANES]
                    cs[p] = cs[p] + preds[p](kk, lane_p + (kt * tk + a * LANES)).astype(I32)
            return tuple(cs)
        cs = lax.fori_loop(0, nvis, body, tuple(jnp.zeros((SEL_ROWS, LANES), I32) for _ in range(npc)))
        return [jnp.sum(c, axis=1, keepdims=True) for c in cs]

    few, st0 = [], []
    for p in range(npc):
        gmin, ghi = gm_sc[0, pieces[p], :], gm_sc[0, pieces[p], :]
        for g in range(1, ngrp):
            gmin = jnp.minimum(gmin, gm_sc[g, pieces[p], :])
            ghi = jnp.maximum(ghi, gm_sc[g, pieces[p], :])
        lo0 = jnp.maximum(jnp.min(gmin, axis=1, keepdims=True), INT_MIN + 1)
        hi0 = jnp.max(ghi, axis=1, keepdims=True)
        fw = limits[p] < n_sel
        few.append(fw)
        st0.append((jnp.where(fw, INT_MIN + 1, lo0), jnp.where(fw, INT_MIN + 1, hi0),
                    jnp.full((SEL_ROWS, 1), 2 ** 30, I32), jnp.zeros((SEL_ROWS, 1), I32)))

    def bis_cond(sts):
        act = [jnp.max((st[0] < st[1]).astype(I32)) for st in sts]
        return functools.reduce(jnp.maximum, act) > 0

    def midpoint(lo, hi):
        mid = (lo | hi) - ((lo ^ hi) >> 1)
        fin = lambda k: jnp.clip(k, -KEY_FMAX - 1, KEY_FMAX)
        vmid = _sortable(0.5 * _unsortable(fin(lo)) + 0.5 * _unsortable(fin(hi)))
        return jnp.where((vmid > lo) & (vmid <= hi), vmid, mid)

    def bis_body(sts):
        mids = [midpoint(st[0], st[1]) for st in sts]
        cs = count([(lambda kk, col, m=m: kk >= m) for m in mids])
        out = []
        for (lo, hi, c_lo, c_hi1), mid, c in zip(sts, mids, cs):
            active = lo < hi
            up = active & (c >= n_sel)
            dn = active & (c < n_sel)
            hit = active & (c == n_sel)
            out.append((jnp.where(up, mid, lo), jnp.where(dn, mid - 1, jnp.where(hit, mid, hi)),
                        jnp.where(up, c, c_lo), jnp.where(dn, c, c_hi1)))
        return tuple(out)

    sts = lax.while_loop(bis_cond, bis_body, tuple(st0))
    thr = [st[0] for st in sts]

    excess = [(st[2] > n_sel) & jnp.logical_not(fw) for st, fw in zip(sts, few)]
    need = [n_sel - st[3] for st in sts]
    ncol = nkt * tk

    def tie_cut():
        nbits = max(1, int(math.ceil(math.log2(ncol))))

        def cut_body(b, ps):
            cands = [p | jnp.left_shift(jnp.int32(1), nbits - 1 - b) for p in ps]
            cnts = count([(lambda kk, col, t=t, cd=cd: (kk == t) & (col < cd)) for t, cd in zip(thr, cands)])
            return tuple(jnp.where(cnt < nd, cd, p) for cnt, nd, cd, p in zip(cnts, need, cands, ps))

        ps = lax.fori_loop(0, nbits, cut_body, tuple(jnp.zeros((SEL_ROWS, 1), I32) for _ in range(npc)))
        return tuple(p + 1 for p in ps)

    any_excess = functools.reduce(jnp.maximum, [jnp.max(e.astype(I32)) for e in excess]) > 0
    cuts = lax.cond(any_excess, tie_cut, lambda: tuple(jnp.full((SEL_ROWS, 1), ncol, I32) for _ in range(npc)))
    cuts = [jnp.where(e, ct, ncol) for e, ct in zip(excess, cuts)]

    def write_tile(kt, carry):
        col = lax.broadcasted_iota(I32, (SEL_ROWS, tk), 1) + kt * tk
        for p in range(npc):
            kk = key_sc[kt, pieces[p], :]
            sel = (kk > thr[p]) | ((kk == thr[p]) & (col < cuts[p]))
            mask_ref[kt, pieces[p], :] = jnp.where(sel, 0.0, NEG).astype(mask_ref.dtype)
        return carry

    lax.fori_loop(0, nvis, write_tile, 0)

    def fill_tile(kt, carry):
        mask_ref[kt] = jnp.full((tq, tk), NEG, mask_ref.dtype)
        return carry

    lax.fori_loop(nvis, nkt, fill_tile, 0)


def _select(proj, wi, kit3, n_sel, qi_col_block):
    s = proj.shape[0]
    nkt, _, tk = kit3.shape
    tq = SEL_TQ
    nqb = s // tq
    ngrp = -(-n_sel // LANES)
    assert ngrp <= tk // LANES and tq % SEL_ROWS == 0
    return pl.pallas_call(
        functools.partial(_select_kernel, n_sel=n_sel),
        grid=(nqb,),
        in_specs=[pl.BlockSpec((tq, W_IDX_Q), lambda i: (i, qi_col_block)),
                  pl.BlockSpec((tq, H_IDX), lambda i: (i, 0)),
                  pl.BlockSpec((nkt, D_IDX, tk), lambda i: (0, 0, 0))],
        out_specs=pl.BlockSpec((None, nkt, tq, tk), lambda i: (i, 0, 0, 0)),
        out_shape=jax.ShapeDtypeStruct((nqb, nkt, tq, tk), BF16),
        scratch_shapes=[pltpu.VMEM((nkt, tq, tk), I32),
                        pltpu.VMEM((H_IDX * tq, D_IDX), BF16),
                        pltpu.VMEM((H_IDX, tq, LANES), F32),
                        pltpu.VMEM((ngrp, tq, LANES), I32)],
        compiler_params=_params(("arbitrary",)),
        name="dsa_select",
    )(proj, wi, kit3)


def _dsa_attn_kernel(qb_ref, kb_ref, first_ref, last_ref, slab_ref,
                     q_ref, kt_ref, v_ref, mask_ref, gen_ref, far_ref, o_ref,
                     m_sc, l_sc, acc_sc, band_ref):
    step = pl.program_id(0)
    tq = q_ref.shape[0]
    tk = kt_ref.shape[1]
    nslab = tk // LANES
    slab0 = slab_ref[step]

    npc = tq // ATT_ROWS

    @pl.when(step == 0)
    def _():
        win = LANES + ATT_ROWS
        for h in range(H_A):
            for v in range(band_ref.shape[1]):
                rows = jnp.broadcast_to(gen_ref[h:h + 1, v * LANES:v * LANES + win], (ATT_ROWS, win))
                rows = pltpu.roll(rows, win - (ATT_ROWS - 1), 1, stride=1, stride_axis=0)
                band_ref[h, v] = rows[:, 0:LANES]

    @pl.when(first_ref[step] == 1)
    def _():
        m_sc[...] = jnp.full(m_sc.shape, NEG, F32)
        l_sc[...] = jnp.zeros(l_sc.shape, F32)
        acc_sc[...] = jnp.zeros(acc_sc.shape, F32)

    def tile(near):
        mask = jnp.concatenate(
            [jnp.concatenate([mask_ref[r, j] for j in range(mask_ref.shape[1])], axis=1)
             for r in range(mask_ref.shape[0])], axis=0).astype(F32)
        ones = jnp.ones((tk, HD_A), BF16)
        for h in range(H_A):
            hs = slice(h * HD_A, (h + 1) * HD_A)
            s = jnp.dot(q_ref[:, hs], kt_ref[hs, :], preferred_element_type=F32) + mask
            if near:
                back = ATT_ROWS // LANES
                s = s + jnp.concatenate(
                    [jnp.concatenate([band_ref[h, slab0 + a + (npc - 1 - p) * back] for a in range(nslab)], axis=1)
                     for p in range(npc)], axis=0)
                shift = 0.0
            else:
                shift = far_ref[h]
            smax = s[:, 0:LANES]
            for a in range(1, nslab):
                smax = jnp.maximum(smax, s[:, a * LANES:(a + 1) * LANES])
            m_old = m_sc[h]
            m_new = jnp.maximum(m_old, jnp.max(smax, axis=1, keepdims=True) + shift)
            alpha = jnp.exp2(m_old - m_new)
            mm = m_new - shift
            p = jnp.concatenate([jnp.exp2(s[:, a * LANES:(a + 1) * LANES] - mm) for a in range(nslab)],
                                axis=1).astype(BF16)
            pv = jnp.dot(p, jnp.concatenate([v_ref[:, hs], ones], axis=1), preferred_element_type=F32)
            acc_sc[:, hs] = alpha * acc_sc[:, hs] + pv[:, 0:HD_A]
            l_sc[h] = alpha * l_sc[h] + pv[:, HD_A:2 * HD_A]
            m_sc[h] = m_new

    @pl.when(slab0 >= 0)
    def _():
        tile(True)

    @pl.when(slab0 < 0)
    def _():
        tile(False)

    @pl.when(last_ref[step] == 1)
    def _():
        for h in range(H_A):
            hs = slice(h * HD_A, (h + 1) * HD_A)
            o_ref[:, hs] = (acc_sc[:, hs] / l_sc[h]).astype(o_ref.dtype)


def _t5_bucket_np(rel):
    nb = N_T5_BUCKETS // 2
    ret = (rel > 0).astype(np.int32) * nb
    n = np.abs(rel)
    max_exact = nb // 2
    nf = np.maximum(n, 1).astype(np.float32)
    large = max_exact + (np.log(nf / np.float32(max_exact)) / np.float32(math.log(T5_MAX_DIST / max_exact))
                         * np.float32(nb - max_exact)).astype(np.int32)
    large = np.minimum(large, nb - 1)
    return ret + np.where(n < max_exact, n, large)


def _dsa_plan(s, tq, tk):
    nqb = s // tq
    d_all = np.arange(-(s - 1), CHUNK, dtype=np.int64)
    b_all = _t5_bucket_np(d_all)
    far_bucket = int(b_all[0])
    varying = np.nonzero(b_all != far_bucket)[0]
    d_lo = int(d_all[varying[0]])
    o_min = min(int(math.ceil((d_lo - (tk - 1)) / LANES)) * LANES, 0)
    n_slabs = (-o_min + tk) // LANES
    qb, kb, first, last, slab = [], [], [], [], []
    for i in range(nqb):
        q0 = i * tq
        nvis = (q0 + tq + tk - 1) // tk
        for j in range(nvis):
            o = j * tk - q0
            qb.append(i)
            kb.append(j)
            first.append(1 if j == 0 else 0)
            last.append(1 if j == nvis - 1 else 0)
            slab.append(-1 if o < o_min else (o - o_min) // LANES)
    n_tab = n_slabs + (tq - ATT_ROWS) // LANES
    o_lo = o_min - (tq - ATT_ROWS)
    x = np.arange(n_tab * LANES + ATT_ROWS)
    g_bucket = _t5_bucket_np(np.clip(o_lo + x - (ATT_ROWS - 1), -(s - 1), CHUNK))
    plan = tuple(np.asarray(a, np.int32) for a in (qb, kb, first, last, slab))
    return plan, g_bucket, far_bucket, n_tab


def _dsa_attn(proj, kt, mask4, t5_table, q_col, v_col):
    s = proj.shape[0]
    tq, tk = ATT_TQ, ATT_TK
    (qb, kb, first, last, slab), g_bucket, far_bucket, n_tab = _dsa_plan(s, tq, tk)
    nsteps = len(qb)
    t5l = t5_table * LOG2E
    gen = jnp.transpose(t5l[g_bucket], (1, 0))
    far = t5l[far_bucket]
    rq = tq // SEL_TQ
    grid_spec = pltpu.PrefetchScalarGridSpec(
        num_scalar_prefetch=5,
        grid=(nsteps,),
        in_specs=[
            pl.BlockSpec((tq, W_A), lambda t, qb, kb, f, l, sl: (qb[t], q_col)),
            pl.BlockSpec((W_A, tk), lambda t, qb, kb, f, l, sl: (0, kb[t])),
            pl.BlockSpec((tk, W_A), lambda t, qb, kb, f, l, sl: (kb[t], v_col)),
            pl.BlockSpec((rq, tk // SEL_TK, SEL_TQ, SEL_TK), lambda t, qb, kb, f, l, sl: (qb[t], kb[t], 0, 0)),
            pl.BlockSpec((H_A, n_tab * LANES + ATT_ROWS), lambda t, qb, kb, f, l, sl: (0, 0)),
            pl.BlockSpec(memory_space=pltpu.SMEM),
        ],
        out_specs=pl.BlockSpec((tq, W_A), lambda t, qb, kb, f, l, sl: (qb[t], 0)),
        scratch_shapes=[pltpu.VMEM((H_A, tq, LANES), F32),
                        pltpu.VMEM((H_A, tq, LANES), F32),
                        pltpu.VMEM((tq, W_A), F32),
                        pltpu.VMEM((H_A, n_tab, ATT_ROWS, LANES), F32)],
    )
    return pl.pallas_call(
        _dsa_attn_kernel,
        grid_spec=grid_spec,
        out_shape=jax.ShapeDtypeStruct((s, W_A), BF16),
        compiler_params=_params(("arbitrary",)),
        name="dsa_attn",
    )(jnp.asarray(qb), jnp.asarray(kb), jnp.asarray(first), jnp.asarray(last), jnp.asarray(slab),
      proj, kt, proj, mask4, gen, far)


def _band_kernel(q_ref, k0_ref, k1_ref, k2_ref, v0_ref, v1_ref, v2_ref, gen_ref, o_ref, bias_ref):
    i = pl.program_id(0)
    tq = q_ref.shape[0]
    wk = 3 * tq

    @pl.when(i == 0)
    def _():
        r = lax.broadcasted_iota(I32, (tq, wk), 0)
        c = lax.broadcasted_iota(I32, (tq, wk), 1)
        dchunk = (c // CHUNK - 2 * tq // CHUNK) - r // CHUNK
        in_band = (dchunk <= 0) & (dchunk >= -N_LEFT_CHUNKS)
        win = wk + tq
        for h in range(H_B):
            rows = jnp.broadcast_to(gen_ref[h:h + 1, :], (tq, win))
            rows = pltpu.roll(rows, win - (tq - 1), 1, stride=1, stride_axis=0)
            bias_ref[h] = jnp.where(in_band, rows[:, 0:wk], NEG)
    col = lax.broadcasted_iota(I32, (tq, wk), 1)
    start_mask = jnp.where(col + (i - 2) * tq >= 0, 0.0, NEG)
    for h in range(H_B):
        hs = slice(h * HD_B, (h + 1) * HD_B)
        kc = jnp.concatenate([k0_ref[hs, :], k1_ref[hs, :], k2_ref[hs, :]], axis=1)
        vc = jnp.concatenate([v0_ref[:, hs], v1_ref[:, hs], v2_ref[:, hs]], axis=0)
        s = jnp.dot(q_ref[:, hs], kc, preferred_element_type=F32) + bias_ref[h] + start_mask
        m = jnp.max(s, axis=1, keepdims=True)
        p = jnp.exp2(s - m)
        l = jnp.sum(p, axis=1, keepdims=True)
        pv = jnp.dot(p.astype(BF16), vc, preferred_element_type=F32)
        o_ref[:, hs] = (pv / l).astype(o_ref.dtype)


def _band_gen(rel_table, tq):
    assert 2 * tq >= N_LEFT_CHUNKS * CHUNK and tq % CHUNK == 0
    x = np.arange(4 * tq)
    idx = np.clip(2 * tq + (tq - 1) - x, -REL_CLIP, REL_CLIP) + REL_CLIP
    return rel_table[:, idx] * LOG2E


def _band_attn(proj, kt, rel_table, q_col, v_col):
    s = proj.shape[0]
    tq = BAND_TQ
    gen = _band_gen(rel_table, tq)

    def kspec(back):
        return pl.BlockSpec((W_B, tq), lambda i: (0, jnp.maximum(i - back, 0)))

    def vspec(back):
        return pl.BlockSpec((tq, W_B), lambda i: (jnp.maximum(i - back, 0), v_col))

    return pl.pallas_call(
        _band_kernel,
        grid=(s // tq,),
        in_specs=[pl.BlockSpec((tq, W_B), lambda i: (i, q_col)),
                  kspec(2), kspec(1), kspec(0), vspec(2), vspec(1), vspec(0),
                  pl.BlockSpec((H_B, 4 * tq), lambda i: (0, 0))],
        out_specs=pl.BlockSpec((tq, W_B), lambda i: (i, 0)),
        out_shape=jax.ShapeDtypeStruct((s, W_B), BF16),
        scratch_shapes=[pltpu.VMEM((H_B, tq, 3 * tq), F32)],
        compiler_params=_params(("arbitrary",)),
        name="band_attn",
    )(proj, kt, kt, kt, proj, proj, proj, gen)


def _merge_kernel(ya_ref, yb_ref, ga_ref, gb_ref, wa_ref, wb_ref, o_ref):
    ua = jnp.dot(ya_ref[...], wa_ref[...], preferred_element_type=F32)
    ub = jnp.dot(yb_ref[...], wb_ref[...], preferred_element_type=F32)
    o_ref[...] = (jax.nn.sigmoid(ga_ref[...]) * ua + jax.nn.sigmoid(gb_ref[...]) * ub).astype(o_ref.dtype)


def _merge(ya, yb, gates, wa, wb, tm=256):
    s, d = ya.shape[0], wa.shape[1]
    return pl.pallas_call(
        _merge_kernel,
        grid=(s // tm,),
        in_specs=[pl.BlockSpec((tm, W_A), lambda i: (i, 0)),
                  pl.BlockSpec((tm, W_B), lambda i: (i, 0)),
                  pl.BlockSpec((tm, d), lambda i: (i, 0)),
                  pl.BlockSpec((tm, d), lambda i: (i, 1)),
                  pl.BlockSpec((W_A, d), lambda i: (0, 0)),
                  pl.BlockSpec((W_B, d), lambda i: (0, 0))],
        out_specs=pl.BlockSpec((tm, d), lambda i: (i, 0)),
        out_shape=jax.ShapeDtypeStruct((s, d), BF16),
        compiler_params=_params(("parallel",)),
        name="merge",
    )(ya, yb, gates, gates, wa, wb)


def _post_kernel(x_ref, m_ref, wo_ref, g1_ref, gn_ref, sc_ref, sh_ref, x1_ref, h2_ref):
    x1 = x_ref[...] + g1_ref[...] * jnp.dot(m_ref[...], wo_ref[...], preferred_element_type=F32)
    x1_ref[...] = x1
    y = x1 * lax.rsqrt(jnp.mean(x1 * x1, axis=-1, keepdims=True) + EPS)
    _store_rows(h2_ref, y * gn_ref[...] * (1.0 + sc_ref[...]) + sh_ref[...])


def _post(x2, merged, wo, g1, gn, sc, sh, tm=256):
    t, d = x2.shape
    row = pl.BlockSpec((1, d), lambda i: (0, 0))
    tile = pl.BlockSpec((tm, d), lambda i: (i, 0))
    return pl.pallas_call(
        _post_kernel,
        grid=(t // tm,),
        in_specs=[tile, tile, pl.BlockSpec((d, d), lambda i: (0, 0)), row, row, row, row],
        out_specs=[tile, pl.BlockSpec((tm * (d // LANES), LANES), lambda i: (i, 0))],
        out_shape=[jax.ShapeDtypeStruct((t, d), F32), jax.ShapeDtypeStruct((t * (d // LANES), LANES), F32)],
        compiler_params=_params(("parallel",)),
        name="post",
    )(x2, merged, wo, g1, gn, sc, sh)


def _router_kernel(h_ref, wr_ref, br_ref, lg_ref):
    lg_ref[...] = jnp.dot(_load_rows(h_ref, wr_ref.shape[0] // LANES), wr_ref[...], preferred_element_type=F32,
                          precision=lax.Precision.HIGHEST) + br_ref[...]


def _router(h2, w_router, b_router, tm=512):
    d, nr = w_router.shape
    t = h2.shape[0] // (d // LANES)
    return pl.pallas_call(
        _router_kernel,
        grid=(t // tm,),
        in_specs=[pl.BlockSpec((tm * (d // LANES), LANES), lambda i: (i, 0)),
                  pl.BlockSpec((d, nr), lambda i: (0, 0)), pl.BlockSpec((1, nr), lambda i: (0, 0))],
        out_specs=pl.BlockSpec((tm, nr), lambda i: (i, 0)),
        out_shape=jax.ShapeDtypeStruct((t, nr), F32),
        compiler_params=_params(("parallel",)),
        name="router",
    )(h2, w_router, b_router)


def _moe_kernel(tok_ref, dst_ref, be_ref, nu_ref,
                h_hbm, w1_ref, w3_ref, w2_ref, z_hbm,
                x0, x1, y0, y1, w1b, w3b, w2b, gsem, ssem):
    b = pl.program_id(0)
    c = w1b.shape[0] // LANES
    bm = x0.shape[0] // c
    nu = nu_ref[0]
    xs, ys = (x0, x1), (y0, y1)

    def gather_copy(sl, r, t):
        return pltpu.make_async_copy(h_hbm.at[pl.ds(t * c, c), :], xs[sl].at[pl.ds(r * c, c), :], gsem.at[sl])

    def scatter_copy(sl, r, d):
        return pltpu.make_async_copy(ys[sl].at[pl.ds(r * c, c), :], z_hbm.at[pl.ds(d * c, c), :], ssem.at[sl])

    def gather_start(blk, sl):
        for r in range(bm):
            gather_copy(sl, r, tok_ref[blk * bm + r]).start()

    def scatter_start(blk, sl):
        for r in range(bm):
            scatter_copy(sl, r, dst_ref[(blk + 2) * bm + r]).start()

    def gather_wait(sl):
        for r in range(bm):
            gather_copy(sl, r, 0).wait()

    def scatter_wait(sl):
        for r in range(bm):
            scatter_copy(sl, r, 0).wait()

    @pl.when(b == 0)
    def _():
        y0[...] = jnp.zeros(y0.shape, F32)
        y1[...] = jnp.zeros(y1.shape, F32)
        scatter_start(-2, 0)
        gather_start(0, 0)

    @pl.when((b < nu) & ((b == 0) | (be_ref[b] != be_ref[jnp.maximum(b - 1, 0)])))
    def _():
        w1b[...] = w1_ref[0].astype(BF16)
        w3b[...] = w3_ref[0].astype(BF16)
        w2b[...] = w2_ref[0].astype(BF16)

    def main(sl):
        gather_wait(sl)
        scatter_wait(sl)
        gather_start(b + 1, 1 - sl)
        scatter_start(b - 1, 1 - sl)
        x = _load_rows(xs[sl], c).astype(BF16)
        a1 = jnp.dot(x, w1b[...], preferred_element_type=F32)
        a3 = jnp.dot(x, w3b[...], preferred_element_type=F32)
        a = (a1 * jax.nn.sigmoid(a1)) * a3
        y = jnp.dot(a.astype(BF16), w2b[...], preferred_element_type=F32)
        _store_rows(ys[sl], y)

    def drain(sl):
        gather_wait(sl)
        scatter_wait(sl)
        scatter_start(b - 1, 1 - sl)
        scatter_wait(1 - sl)

    for sl in range(2):
        @pl.when((b < nu) & (b % 2 == sl))
        def _():
            main(sl)

        @pl.when((b == nu) & (b % 2 == sl))
        def _():
            drain(sl)


def _moe(h2, tok_buf, dst_buf, blk_e, n_used, w1, w3, w2):
    d, dff = w1.shape[1:]
    c = d // LANES
    t = h2.shape[0] // c
    bm = MOE_BM
    nsteps = blk_e.shape[0]
    assert tok_buf.shape[0] == nsteps * bm and dst_buf.shape[0] == (nsteps + 2) * bm
    grid_spec = pltpu.PrefetchScalarGridSpec(
        num_scalar_prefetch=4,
        grid=(nsteps,),
        in_specs=[
            pl.BlockSpec(memory_space=pl.ANY),
            pl.BlockSpec((1, d, dff), lambda b, tok, dst, be, nu: (be[b], 0, 0)),
            pl.BlockSpec((1, d, dff), lambda b, tok, dst, be, nu: (be[b], 0, 0)),
            pl.BlockSpec((1, dff, d), lambda b, tok, dst, be, nu: (be[b], 0, 0)),
        ],
        out_specs=pl.BlockSpec(memory_space=pl.ANY),
        scratch_shapes=[pltpu.VMEM((bm * c, LANES), F32), pltpu.VMEM((bm * c, LANES), F32),
                        pltpu.VMEM((bm * c, LANES), F32), pltpu.VMEM((bm * c, LANES), F32),
                        pltpu.VMEM((d, dff), BF16), pltpu.VMEM((d, dff), BF16), pltpu.VMEM((dff, d), BF16),
                        pltpu.SemaphoreType.DMA((2,)), pltpu.SemaphoreType.DMA((2,))],
    )
    return pl.pallas_call(
        _moe_kernel,
        grid_spec=grid_spec,
        out_shape=jax.ShapeDtypeStruct(((TOPK_IN_GROUP * t + 2 * bm) * c, LANES), F32),
        compiler_params=pltpu.CompilerParams(dimension_semantics=("arbitrary",),
                                             vmem_limit_bytes=VMEM_LIMIT, has_side_effects=True),
        name="moe",
    )(tok_buf, dst_buf, blk_e, n_used, h2, w1, w3, w2)


def _final_kernel(x1_ref, z0_ref, z1_ref, w_ref, g2_ref, gn_ref, o_ref):
    c = x1_ref.shape[1] // LANES
    moe = w_ref[:, 0:1] * _load_rows(z0_ref, c) + w_ref[:, 1:2] * _load_rows(z1_ref, c)
    x2 = x1_ref[...] + g2_ref[...] * moe
    y = x2 * lax.rsqrt(jnp.mean(x2 * x2, axis=-1, keepdims=True) + EPS)
    o_ref[...] = y * gn_ref[...]


def _final(x1, z, weight, g2, gn, tm=256):
    t, d = x1.shape
    row = pl.BlockSpec((1, d), lambda i: (0, 0))
    return pl.pallas_call(
        _final_kernel,
        grid=(t // tm,),
        in_specs=[pl.BlockSpec((tm, d), lambda i: (i, 0)),
                  pl.BlockSpec((tm * (d // LANES), LANES), lambda i: (i, 0)),
                  pl.BlockSpec((tm * (d // LANES), LANES), lambda i: (t // tm + i, 0)),
                  pl.BlockSpec((tm, TOPK_IN_GROUP), lambda i: (i, 0)), row, row],
        out_specs=pl.BlockSpec((tm, d), lambda i: (i, 0)),
        out_shape=jax.ShapeDtypeStruct((t, d), F32),
        compiler_params=_params(("parallel",)),
        name="final",
    )(x1, z, z, weight, g2, gn)


def _route(logits, t):
    gl = logits[:, :N_GROUPS]
    el = logits[:, N_GROUPS:N_GROUPS + N_EXPERTS].reshape(t, N_GROUPS, EXP_PER_GROUP)
    g_prob = jax.nn.softmax(gl, axis=-1)
    grp = jnp.argmax(gl, axis=-1).astype(I32)
    p_grp = jnp.take_along_axis(g_prob, grp[:, None], axis=-1)[:, 0]
    e_in = jnp.take_along_axis(el, grp[:, None, None], axis=1)[:, 0]
    top_v, top_i = lax.top_k(e_in, TOPK_IN_GROUP)
    p_in = jax.nn.softmax(top_v, axis=-1)
    expert = grp[:, None] * EXP_PER_GROUP + top_i.astype(I32)
    weight = p_grp[:, None] * p_in

    bm = MOE_BM
    m = t * TOPK_IN_GROUP
    e_flat = expert.reshape(m)
    order = jnp.argsort(e_flat, stable=True).astype(I32)
    counts = jnp.bincount(e_flat, length=N_EXPERTS).astype(I32)
    start = jnp.cumsum(counts) - counts
    padded = ((counts + bm - 1) // bm) * bm
    pend = jnp.cumsum(padded)
    pstart = pend - padded
    nb = m // bm + N_EXPERTS + 1
    blk_e = jnp.minimum(jnp.searchsorted(pend, jnp.arange(nb, dtype=I32) * bm, side='right'),
                        N_EXPERTS - 1).astype(I32)
    n_used = (pend[-1] // bm).astype(I32).reshape(1)
    pos = jnp.arange(nb * bm, dtype=I32).reshape(nb, bm)
    src = (start[blk_e] - pstart[blk_e])[:, None] + pos
    valid = (src < (start + counts)[blk_e][:, None]) & (pos < pend[-1])
    a = order[jnp.clip(src, 0, m - 1)]
    tok = a // TOPK_IN_GROUP
    tok_buf = jnp.where(valid, tok, 0).reshape(-1)
    spare = m + pos % (2 * bm)
    dst_buf = jnp.where(valid, (a % TOPK_IN_GROUP) * t + tok, spare).reshape(-1)
    dst_buf = jnp.concatenate([m + jnp.arange(2 * bm, dtype=I32), dst_buf])
    return tok_buf, dst_buf, blk_e, n_used, weight


def kernel(x, c, w_ada, b_ada, norm_mix, w_in, t5_table, rel_table, w_up_a, w_up_b, w_o, norm_ffn,
           w_rg, b_rg, w_re, b_re, w1, w3, w2, norm_final):
    bn, s, d = x.shape
    assert bn == 1 and w_ada.shape[0] == 1
    assert s % 1024 == 0
    t = bn * s
    x2 = x.reshape(t, d)
    n_sel = min(TOPK_MAX, s // 4)

    mod = _ada(c.reshape(d, 1), w_ada[0], b_ada[0].reshape(1, 6 * d))
    sh1, sc1, g1, sh2, sc2, g2 = [mod[:, i * d:(i + 1) * d] for i in range(6)]

    h = _rms_mod(x2, norm_mix[0].reshape(1, d), sc1, sh1, BF16)

    cols = np.cumsum([0, W_A, W_A, W_A, W_IDX_Q, D_IDX, H_IDX, W_B, W_B, W_B, d, d])
    wsl = [w_in[0][:, cols[i]:cols[i + 1]] for i in range(11)]
    wqa, wka, wva, wqi, wki, wwi, wqb, wkb, wvb, wga, wgb = wsl
    w_main = jnp.concatenate([wqa * (LOG2E / math.sqrt(HD_A)), wka, wva, wqi,
                              wqb * (LOG2E / math.sqrt(HD_B)), wkb, wvb], axis=1).astype(BF16)
    w_gate = jnp.concatenate([wga, wgb], axis=1).astype(BF16)
    w_idx = jnp.concatenate([wki, wwi, jnp.zeros((d, LANES - D_IDX - H_IDX), F32)], axis=1).astype(BF16)

    proj = _matmul(h, w_main, BF16, 1024, 1024, "proj_main")
    gates = _matmul(h, w_gate, F32, 1024, 1024, "proj_gate")
    idx = _matmul(h, w_idx, F32, 1024, LANES, "proj_idx")
    nkt = s // SEL_TK
    kit3 = jnp.transpose(idx[:, :D_IDX].astype(BF16).reshape(nkt, SEL_TK, D_IDX), (0, 2, 1))
    wi = idx[:, D_IDX:D_IDX + H_IDX] * ((H_IDX ** -0.5) * (D_IDX ** -0.5))
    kt_a = jnp.transpose(proj[:, W_A:2 * W_A])
    kt_b = jnp.transpose(proj[:, 5 * W_A:6 * W_A])

    mask4 = _select(proj, wi, kit3, n_sel, qi_col_block=3)
    y_a = _dsa_attn(proj, kt_a, mask4, t5_table, q_col=0, v_col=2)
    y_b = _band_attn(proj, kt_b, rel_table[0], q_col=4, v_col=6)

    merged = _merge(y_a, y_b, gates, w_up_a[0].astype(BF16), w_up_b[0].astype(BF16))

    nr = LANES
    w_router = jnp.concatenate([w_rg[0], w_re[0], jnp.zeros((d, nr - N_GROUPS - N_EXPERTS), F32)], axis=1)
    b_router = jnp.concatenate([b_rg[0], b_re[0], jnp.zeros((nr - N_GROUPS - N_EXPERTS,), F32)]).reshape(1, nr)
    x1, h2 = _post(x2, merged, w_o[0].astype(BF16), g1, norm_ffn[0].reshape(1, d), sc2, sh2)
    logits = _router(h2, w_router, b_router)
    tok_buf, dst_buf, blk_e, n_used, weight = _route(logits, t)
    z = _moe(h2, tok_buf, dst_buf, blk_e, n_used, w1[0], w3[0], w2[0])
    out = _final(x1, z, weight, g2, norm_final.reshape(1, d))
    return out.reshape(bn, s, d)
```

```python
import functools
import math

import numpy as np
import jax
import jax.numpy as jnp
from jax import lax
from jax.experimental import pallas as pl
from jax.experimental.pallas import tpu as pltpu

F32 = jnp.float32
BF16 = jnp.bfloat16
I32 = jnp.int32

CHUNK = 64
EPS = 1e-6
H_A, HD_A = 8, 128
H_IDX, D_IDX = 16, 64
TOPK_MAX = 256
N_T5_BUCKETS = 32
T5_MAX_DIST = 1024
H_B, HD_B = 8, 128
N_LEFT_CHUNKS = 8
REL_CLIP = 128
N_GROUPS = 8
EXP_PER_GROUP = 8
N_EXPERTS = N_GROUPS * EXP_PER_GROUP
TOPK_IN_GROUP = 2

W_A = H_A * HD_A
W_B = H_B * HD_B
W_IDX_Q = H_IDX * D_IDX

NEG = -1e30
INT_MIN = -(2 ** 31)
KEY_FMAX = 0x7F7FFFFF
LOG2E = math.log2(math.e)

LANES = 128
VMEM_LIMIT = 56 * 1024 * 1024

SEL_TQ = 256
SEL_ROWS = 128
SEL_TK = 512
ATT_TQ = 512
ATT_ROWS = 256
ATT_TK = 1024
BAND_TQ = 256
MOE_BM = 128


def _params(sem, vmem=VMEM_LIMIT):
    return pltpu.CompilerParams(dimension_semantics=sem, vmem_limit_bytes=vmem)


def _load_rows(ref, c):
    rows = ref.shape[0] // c
    return jnp.concatenate([ref[pl.ds(k, rows, stride=c), :] for k in range(c)], axis=1)


def _store_rows(ref, val):
    rows = val.shape[0]
    c = ref.shape[0] // rows
    for k in range(c):
        ref[pl.ds(k, rows, stride=c), :] = val[:, k * LANES:(k + 1) * LANES]


def _ada_kernel(c_ref, w_ref, b_ref, o_ref, *, kc):
    d = w_ref.shape[0]
    tn = w_ref.shape[1]

    def body(k, acc):
        r0 = pl.multiple_of(k * kc, kc)
        cc = c_ref[pl.ds(r0, kc), :]
        ca = cc * jax.nn.sigmoid(cc)
        return acc + jnp.sum(w_ref[pl.ds(r0, kc), :] * ca, axis=0, keepdims=True)

    acc = lax.fori_loop(0, d // kc, body, jnp.zeros((1, tn), F32))
    o_ref[...] = acc + b_ref[...]


def _ada(c_col, w, b_row, tn=1024, kc=256):
    d, n = w.shape
    return pl.pallas_call(
        functools.partial(_ada_kernel, kc=kc),
        grid=(n // tn,),
        in_specs=[pl.BlockSpec((d, 1), lambda j: (0, 0)),
                  pl.BlockSpec((d, tn), lambda j: (0, j)),
                  pl.BlockSpec((1, tn), lambda j: (0, j))],
        out_specs=pl.BlockSpec((1, tn), lambda j: (0, j)),
        out_shape=jax.ShapeDtypeStruct((1, n), F32),
        compiler_params=_params(("arbitrary",)),
        name="ada",
    )(c_col, w, b_row)


def _rms_mod_kernel(x_ref, g_ref, sc_ref, sh_ref, o_ref):
    x = x_ref[...]
    y = x * lax.rsqrt(jnp.mean(x * x, axis=-1, keepdims=True) + EPS)
    o_ref[...] = (y * g_ref[...] * (1.0 + sc_ref[...]) + sh_ref[...]).astype(o_ref.dtype)


def _rms_mod(x2, g, sc, sh, out_dtype, tm=512):
    t, d = x2.shape
    row = pl.BlockSpec((1, d), lambda i: (0, 0))
    return pl.pallas_call(
        _rms_mod_kernel,
        grid=(t // tm,),
        in_specs=[pl.BlockSpec((tm, d), lambda i: (i, 0)), row, row, row],
        out_specs=pl.BlockSpec((tm, d), lambda i: (i, 0)),
        out_shape=jax.ShapeDtypeStruct((t, d), out_dtype),
        compiler_params=_params(("parallel",)),
        name="rms_mod",
    )(x2, g, sc, sh)


def _mm_kernel(a_ref, b_ref, o_ref):
    o_ref[...] = jnp.dot(a_ref[...], b_ref[...], preferred_element_type=F32).astype(o_ref.dtype)


def _matmul(a, b, out_dtype, tm, tn, name):
    m, k = a.shape
    n = b.shape[1]
    return pl.pallas_call(
        _mm_kernel,
        grid=(m // tm, n // tn),
        in_specs=[pl.BlockSpec((tm, k), lambda i, j: (i, 0)),
                  pl.BlockSpec((k, tn), lambda i, j: (0, j))],
        out_specs=pl.BlockSpec((tm, tn), lambda i, j: (i, j)),
        out_shape=jax.ShapeDtypeStruct((m, n), out_dtype),
        compiler_params=_params(("parallel", "arbitrary")),
        name=name,
    )(a, b)


def _sortable(x):
    bits = pltpu.bitcast(x, I32)
    return bits ^ ((bits >> 31) & 0x7FFFFFFF)


def _unsortable(k):
    return pltpu.bitcast(k ^ ((k >> 31) & 0x7FFFFFFF), F32)


def _select_kernel(qi_ref, wi_ref, kit_ref, mask_ref, key_sc, qh_sc, wb_sc, gm_sc, *, n_sel):
    nkt, tq, tk = key_sc.shape
    nsub = tk // LANES
    ngrp = gm_sc.shape[0]
    npc = tq // SEL_ROWS
    pieces = [slice(p * SEL_ROWS, (p + 1) * SEL_ROWS) for p in range(npc)]
    i = pl.program_id(0)
    q0 = i * tq
    nvis = (q0 + tq + tk - 1) // tk

    for h in range(H_IDX):
        qh_sc[h * tq:(h + 1) * tq, :] = qi_ref[:, h * D_IDX:(h + 1) * D_IDX]
        wb_sc[h] = jnp.broadcast_to(wi_ref[:, h:h + 1], (tq, LANES))

    row = lax.broadcasted_iota(I32, (tq, 1), 0) + q0
    limit = (row // CHUNK + 1) * CHUNK
    lane = lax.broadcasted_iota(I32, (tq, LANES), 1)
    lane_p = lax.broadcasted_iota(I32, (SEL_ROWS, LANES), 1)
    limits = [((lax.broadcasted_iota(I32, (SEL_ROWS, 1), 0) + (q0 + p * SEL_ROWS)) // CHUNK + 1) * CHUNK
              for p in range(npc)]

    def score_tile(kt, gmax):
        k_t = kit_ref[kt]
        accs = [jnp.zeros((tq, LANES), F32) for _ in range(nsub)]
        s_all = jnp.dot(qh_sc[...], k_t, preferred_element_type=F32)
        for h in range(H_IDX):
            s = s_all[h * tq:(h + 1) * tq, :]
            wb = wb_sc[h]
            for a in range(nsub):
                accs[a] = accs[a] + wb * jnp.maximum(s[:, a * LANES:(a + 1) * LANES], 0.0)
        gmax = list(gmax)
        for a in range(nsub):
            col = lane + (kt * tk + a * LANES)
            key = jnp.where(col < limit, _sortable(accs[a]), INT_MIN)
            key_sc[kt, :, a * LANES:(a + 1) * LANES] = key
            gmax[a % ngrp] = jnp.maximum(gmax[a % ngrp], key)
        return tuple(gmax)

    gmax = lax.fori_loop(0, nvis, score_tile,
                         tuple(jnp.full((tq, LANES), INT_MIN, I32) for _ in range(ngrp)))
    for g in range(ngrp):
        gm_sc[g] = gmax[g]

    def count(preds):
        def body(kt, cs):
            cs = list(cs)
            for p in range(npc):
                for a in range(nsub):
                    kk = key_sc[kt, pieces[p], a * LANES:(a + 1) * LANES]
                    cs[p] = cs[p] + preds[p](kk, lane_p + (kt * tk + a * LANES)).astype(I32)
            return tuple(cs)
        cs = lax.fori_loop(0, nvis, body, tuple(jnp.zeros((SEL_ROWS, LANES), I32) for _ in range(npc)))
        return [jnp.sum(c, axis=1, keepdims=True) for c in cs]

    few, st0 = [], []
    for p in range(npc):
        gmin, ghi = gm_sc[0, pieces[p], :], gm_sc[0, pieces[p], :]
        for g in range(1, ngrp):
            gmin = jnp.minimum(gmin, gm_sc[g, pieces[p], :])
            ghi = jnp.maximum(ghi, gm_sc[g, pieces[p], :])
        lo0 = jnp.maximum(jnp.min(gmin, axis=1, keepdims=True), INT_MIN + 1)
        hi0 = jnp.max(ghi, axis=1, keepdims=True)
        fw = limits[p] < n_sel
        few.append(fw)
        st0.append((jnp.where(fw, INT_MIN + 1, lo0), jnp.where(fw, INT_MIN + 1, hi0),
                    jnp.full((SEL_ROWS, 1), 2 ** 30, I32), jnp.zeros((SEL_ROWS, 1), I32)))

    def bis_cond(sts):
        act = [jnp.max((st[0] < st[1]).astype(I32)) for st in sts]
        return functools.reduce(jnp.maximum, act) > 0

    def midpoint(lo, hi):
        mid = (lo | hi) - ((lo ^ hi) >> 1)
        fin = lambda k: jnp.clip(k, -KEY_FMAX - 1, KEY_FMAX)
        vmid = _sortable(0.5 * _unsortable(fin(lo)) + 0.5 * _unsortable(fin(hi)))
        return jnp.where((vmid > lo) & (vmid <= hi), vmid, mid)

    def bis_body(sts):
        mids = [midpoint(st[0], st[1]) for st in sts]
        cs = count([(lambda kk, col, m=m: kk >= m) for m in mids])
        out = []
        for (lo, hi, c_lo, c_hi1), mid, c in zip(sts, mids, cs):
            active = lo < hi
            up = active & (c >= n_sel)
            dn = active & (c < n_sel)
            hit = active & (c == n_sel)
            out.append((jnp.where(up, mid, lo), jnp.where(dn, mid - 1, jnp.where(hit, mid, hi)),
                        jnp.where(up, c, c_lo), jnp.where(dn, c, c_hi1)))
        return tuple(out)

    sts = lax.while_loop(bis_cond, bis_body, tuple(st0))
    thr = [st[0] for st in sts]

    excess = [(st[2] > n_sel) & jnp.logical_not(fw) for st, fw in zip(sts, few)]
    need = [n_sel - st[3] for st in sts]
    ncol = nkt * tk

    def tie_cut():
        nbits = max(1, int(math.ceil(math.log2(ncol))))

        def cut_body(b, ps):
            cands = [p | jnp.left_shift(jnp.int32(1), nbits - 1 - b) for p in ps]
            cnts = count([(lambda kk, col, t=t, cd=cd: (kk == t) & (col < cd)) for t, cd in zip(thr, cands)])
            return tuple(jnp.where(cnt < nd, cd, p) for cnt, nd, cd, p in zip(cnts, need, cands, ps))

        ps = lax.fori_loop(0, nbits, cut_body, tuple(jnp.zeros((SEL_ROWS, 1), I32) for _ in range(npc)))
        return tuple(p + 1 for p in ps)

    any_excess = functools.reduce(jnp.maximum, [jnp.max(e.astype(I32)) for e in excess]) > 0
    cuts = lax.cond(any_excess, tie_cut, lambda: tuple(jnp.full((SEL_ROWS, 1), ncol, I32) for _ in range(npc)))
    cuts = [jnp.where(e, ct, ncol) for e, ct in zip(excess, cuts)]

    def write_tile(kt, carry):
        col = lax.broadcasted_iota(I32, (SEL_ROWS, tk), 1) + kt * tk
        for p in range(npc):
            kk = key_sc[kt, pieces[p], :]
            sel = (kk > thr[p]) | ((kk == thr[p]) & (col < cuts[p]))
            mask_ref[kt, pieces[p], :] = jnp.where(sel, 0.0, NEG).astype(mask_ref.dtype)
        return carry

    lax.fori_loop(0, nvis, write_tile, 0)

    def fill_tile(kt, carry):
        mask_ref[kt] = jnp.full((tq, tk), NEG, mask_ref.dtype)
        return carry

    lax.fori_loop(nvis, nkt, fill_tile, 0)


def _select(proj, wi, kit3, n_sel, qi_col_block):
    s = proj.shape[0]
    nkt, _, tk = kit3.shape
    tq = SEL_TQ
    nqb = s // tq
    ngrp = -(-n_sel // LANES)
    assert ngrp <= tk // LANES and tq % SEL_ROWS == 0
    return pl.pallas_call(
        functools.partial(_select_kernel, n_sel=n_sel),
        grid=(nqb,),
        in_specs=[pl.BlockSpec((tq, W_IDX_Q), lambda i: (i, qi_col_block)),
                  pl.BlockSpec((tq, H_IDX), lambda i: (i, 0)),
                  pl.BlockSpec((nkt, D_IDX, tk), lambda i: (0, 0, 0))],
        out_specs=pl.BlockSpec((None, nkt, tq, tk), lambda i: (i, 0, 0, 0)),
        out_shape=jax.ShapeDtypeStruct((nqb, nkt, tq, tk), BF16),
        scratch_shapes=[pltpu.VMEM((nkt, tq, tk), I32),
                        pltpu.VMEM((H_IDX * tq, D_IDX), BF16),
                        pltpu.VMEM((H_IDX, tq, LANES), F32),
                        pltpu.VMEM((ngrp, tq, LANES), I32)],
        compiler_params=_params(("arbitrary",)),
        name="dsa_select",
    )(proj, wi, kit3)


def _dsa_attn_kernel(qb_ref, kb_ref, first_ref, last_ref, slab_ref,
                     q_ref, kt_ref, v_ref, mask_ref, gen_ref, far_ref, o_ref,
                     m_sc, l_sc, acc_sc, band_ref):
    step = pl.program_id(0)
    tq = q_ref.shape[0]
    tk = kt_ref.shape[1]
    nslab = tk // LANES
    slab0 = slab_ref[step]

    npc = tq // ATT_ROWS

    @pl.when(step == 0)
    def _():
        win = LANES + ATT_ROWS
        for h in range(H_A):
            for v in range(band_ref.shape[1]):
                rows = jnp.broadcast_to(gen_ref[h:h + 1, v * LANES:v * LANES + win], (ATT_ROWS, win))
                rows = pltpu.roll(rows, win - (ATT_ROWS - 1), 1, stride=1, stride_axis=0)
                band_ref[h, v] = rows[:, 0:LANES]

    @pl.when(first_ref[step] == 1)
    def _():
        m_sc[...] = jnp.full(m_sc.shape, NEG, F32)
        l_sc[...] = jnp.zeros(l_sc.shape, F32)
        acc_sc[...] = jnp.zeros(acc_sc.shape, F32)

    def tile(near):
        mask = jnp.concatenate(
            [jnp.concatenate([mask_ref[r, j] for j in range(mask_ref.shape[1])], axis=1)
             for r in range(mask_ref.shape[0])], axis=0).astype(F32)
        ones = jnp.ones((tk, HD_A), BF16)
        for h in range(H_A):
            hs = slice(h * HD_A, (h + 1) * HD_A)
            s = jnp.dot(q_ref[:, hs], kt_ref[hs, :], preferred_element_type=F32) + mask
            if near:
                back = ATT_ROWS // LANES
                s = s + jnp.concatenate(
                    [jnp.concatenate([band_ref[h, slab0 + a + (npc - 1 - p) * back] for a in range(nslab)], axis=1)
                     for p in range(npc)], axis=0)
                shift = 0.0
            else:
                shift = far_ref[h]
            smax = s[:, 0:LANES]
            for a in range(1, nslab):
                smax = jnp.maximum(smax, s[:, a * LANES:(a + 1) * LANES])
            m_old = m_sc[h]
            m_new = jnp.maximum(m_old, jnp.max(smax, axis=1, keepdims=True) + shift)
            alpha = jnp.exp2(m_old - m_new)
            mm = m_new - shift
            p = jnp.concatenate([jnp.exp2(s[:, a * LANES:(a + 1) * LANES] - mm) for a in range(nslab)],
                                axis=1).astype(BF16)
            pv = jnp.dot(p, jnp.concatenate([v_ref[:, hs], ones], axis=1), preferred_element_type=F32)
            acc_sc[:, hs] = alpha * acc_sc[:, hs] + pv[:, 0:HD_A]
            l_sc[h] = alpha * l_sc[h] + pv[:, HD_A:2 * HD_A]
            m_sc[h] = m_new

    @pl.when(slab0 >= 0)
    def _():
        tile(True)

    @pl.when(slab0 < 0)
    def _():
        tile(False)

    @pl.when(last_ref[step] == 1)
    def _():
        for h in range(H_A):
            hs = slice(h * HD_A, (h + 1) * HD_A)
            o_ref[:, hs] = (acc_sc[:, hs] / l_sc[h]).astype(o_ref.dtype)


def _t5_bucket_np(rel):
    nb = N_T5_BUCKETS // 2
    ret = (rel > 0).astype(np.int32) * nb
    n = np.abs(rel)
    max_exact = nb // 2
    nf = np.maximum(n, 1).astype(np.float32)
    large = max_exact + (np.log(nf / np.float32(max_exact)) / np.float32(math.log(T5_MAX_DIST / max_exact))
                         * np.float32(nb - max_exact)).astype(np.int32)
    large = np.minimum(large, nb - 1)
    return ret + np.where(n < max_exact, n, large)


def _dsa_plan(s, tq, tk):
    nqb = s // tq
    d_all = np.arange(-(s - 1), CHUNK, dtype=np.int64)
    b_all = _t5_bucket_np(d_all)
    far_bucket = int(b_all[0])
    varying = np.nonzero(b_all != far_bucket)[0]
    d_lo = int(d_all[varying[0]])
    o_min = min(int(math.ceil((d_lo - (tk - 1)) / LANES)) * LANES, 0)
    n_slabs = (-o_min + tk) // LANES
    qb, kb, first, last, slab = [], [], [], [], []
    for i in range(nqb):
        q0 = i * tq
        nvis = (q0 + tq + tk - 1) // tk
        for j in range(nvis):
            o = j * tk - q0
            qb.append(i)
            kb.append(j)
            first.append(1 if j == 0 else 0)
            last.append(1 if j == nvis - 1 else 0)
            slab.append(-1 if o < o_min else (o - o_min) // LANES)
    n_tab = n_slabs + (tq - ATT_ROWS) // LANES
    o_lo = o_min - (tq - ATT_ROWS)
    x = np.arange(n_tab * LANES + ATT_ROWS)
    g_bucket = _t5_bucket_np(np.clip(o_lo + x - (ATT_ROWS - 1), -(s - 1), CHUNK))
    plan = tuple(np.asarray(a, np.int32) for a in (qb, kb, first, last, slab))
    return plan, g_bucket, far_bucket, n_tab


def _dsa_attn(proj, kt, mask4, t5_table, q_col, v_col):
    s = proj.shape[0]
    tq, tk = ATT_TQ, ATT_TK
    (qb, kb, first, last, slab), g_bucket, far_bucket, n_tab = _dsa_plan(s, tq, tk)
    nsteps = len(qb)
    t5l = t5_table * LOG2E
    gen = jnp.transpose(t5l[g_bucket], (1, 0))
    far = t5l[far_bucket]
    rq = tq // SEL_TQ
    grid_spec = pltpu.PrefetchScalarGridSpec(
        num_scalar_prefetch=5,
        grid=(nsteps,),
        in_specs=[
            pl.BlockSpec((tq, W_A), lambda t, qb, kb, f, l, sl: (qb[t], q_col)),
            pl.BlockSpec((W_A, tk), lambda t, qb, kb, f, l, sl: (0, kb[t])),
            pl.BlockSpec((tk, W_A), lambda t, qb, kb, f, l, sl: (kb[t], v_col)),
            pl.BlockSpec((rq, tk // SEL_TK, SEL_TQ, SEL_TK), lambda t, qb, kb, f, l, sl: (qb[t], kb[t], 0, 0)),
            pl.BlockSpec((H_A, n_tab * LANES + ATT_ROWS), lambda t, qb, kb, f, l, sl: (0, 0)),
            pl.BlockSpec(memory_space=pltpu.SMEM),
        ],
        out_specs=pl.BlockSpec((tq, W_A), lambda t, qb, kb, f, l, sl: (qb[t], 0)),
        scratch_shapes=[pltpu.VMEM((H_A, tq, LANES), F32),
                        pltpu.VMEM((H_A, tq, LANES), F32),
                        pltpu.VMEM((tq, W_A), F32),
                        pltpu.VMEM((H_A, n_tab, ATT_ROWS, LANES), F32)],
    )
    return pl.pallas_call(
        _dsa_attn_kernel,
        grid_spec=grid_spec,
        out_shape=jax.ShapeDtypeStruct((s, W_A), BF16),
        compiler_params=_params(("arbitrary",)),
        name="dsa_attn",
    )(jnp.asarray(qb), jnp.asarray(kb), jnp.asarray(first), jnp.asarray(last), jnp.asarray(slab),
      proj, kt, proj, mask4, gen, far)


def _band_kernel(q_ref, k0_ref, k1_ref, k2_ref, v0_ref, v1_ref, v2_ref, gen_ref, o_ref, bias_ref):
    i = pl.program_id(0)
    tq = q_ref.shape[0]
    wk = 3 * tq

    @pl.when(i == 0)
    def _():
        r = lax.broadcasted_iota(I32, (tq, wk), 0)
        c = lax.broadcasted_iota(I32, (tq, wk), 1)
        dchunk = (c // CHUNK - 2 * tq // CHUNK) - r // CHUNK
        in_band = (dchunk <= 0) & (dchunk >= -N_LEFT_CHUNKS)
        win = wk + tq
        for h in range(H_B):
            rows = jnp.broadcast_to(gen_ref[h:h + 1, :], (tq, win))
            rows = pltpu.roll(rows, win - (tq - 1), 1, stride=1, stride_axis=0)
            bias_ref[h] = jnp.where(in_band, rows[:, 0:wk], NEG)
    col = lax.broadcasted_iota(I32, (tq, wk), 1)
    start_mask = jnp.where(col + (i - 2) * tq >= 0, 0.0, NEG)
    for h in range(H_B):
        hs = slice(h * HD_B, (h + 1) * HD_B)
        kc = jnp.concatenate([k0_ref[hs, :], k1_ref[hs, :], k2_ref[hs, :]], axis=1)
        vc = jnp.concatenate([v0_ref[:, hs], v1_ref[:, hs], v2_ref[:, hs]], axis=0)
        s = jnp.dot(q_ref[:, hs], kc, preferred_element_type=F32) + bias_ref[h] + start_mask
        m = jnp.max(s, axis=1, keepdims=True)
        p = jnp.exp2(s - m)
        l = jnp.sum(p, axis=1, keepdims=True)
        pv = jnp.dot(p.astype(BF16), vc, preferred_element_type=F32)
        o_ref[:, hs] = (pv / l).astype(o_ref.dtype)


def _band_gen(rel_table, tq):
    assert 2 * tq >= N_LEFT_CHUNKS * CHUNK and tq % CHUNK == 0
    x = np.arange(4 * tq)
    idx = np.clip(2 * tq + (tq - 1) - x, -REL_CLIP, REL_CLIP) + REL_CLIP
    return rel_table[:, idx] * LOG2E


def _band_attn(proj, kt, rel_table, q_col, v_col):
    s = proj.shape[0]
    tq = BAND_TQ
    gen = _band_gen(rel_table, tq)

    def kspec(back):
        return pl.BlockSpec((W_B, tq), lambda i: (0, jnp.maximum(i - back, 0)))

    def vspec(back):
        return pl.BlockSpec((tq, W_B), lambda i: (jnp.maximum(i - back, 0), v_col))

    return pl.pallas_call(
        _band_kernel,
        grid=(s // tq,),
        in_specs=[pl.BlockSpec((tq, W_B), lambda i: (i, q_col)),
                  kspec(2), kspec(1), kspec(0), vspec(2), vspec(1), vspec(0),
                  pl.BlockSpec((H_B, 4 * tq), lambda i: (0, 0))],
        out_specs=pl.BlockSpec((tq, W_B), lambda i: (i, 0)),
        out_shape=jax.ShapeDtypeStruct((s, W_B), BF16),
        scratch_shapes=[pltpu.VMEM((H_B, tq, 3 * tq), F32)],
        compiler_params=_params(("arbitrary",)),
        name="band_attn",
    )(proj, kt, kt, kt, proj, proj, proj, gen)


def _merge_kernel(ya_ref, yb_ref, ga_ref, gb_ref, wa_ref, wb_ref, o_ref):
    ua = jnp.dot(ya_ref[...], wa_ref[...], preferred_element_type=F32)
    ub = jnp.dot(yb_ref[...], wb_ref[...], preferred_element_type=F32)
    o_ref[...] = (jax.nn.sigmoid(ga_ref[...]) * ua + jax.nn.sigmoid(gb_ref[...]) * ub).astype(o_ref.dtype)


def _merge(ya, yb, gates, wa, wb, tm=256):
    s, d = ya.shape[0], wa.shape[1]
    return pl.pallas_call(
        _merge_kernel,
        grid=(s // tm,),
        in_specs=[pl.BlockSpec((tm, W_A), lambda i: (i, 0)),
                  pl.BlockSpec((tm, W_B), lambda i: (i, 0)),
                  pl.BlockSpec((tm, d), lambda i: (i, 0)),
                  pl.BlockSpec((tm, d), lambda i: (i, 1)),
                  pl.BlockSpec((W_A, d), lambda i: (0, 0)),
                  pl.BlockSpec((W_B, d), lambda i: (0, 0))],
        out_specs=pl.BlockSpec((tm, d), lambda i: (i, 0)),
        out_shape=jax.ShapeDtypeStruct((s, d), BF16),
        compiler_params=_params(("parallel",)),
        name="merge",
    )(ya, yb, gates, gates, wa, wb)


def _post_kernel(x_ref, m_ref, wo_ref, g1_ref, gn_ref, sc_ref, sh_ref, x1_ref, h2_ref):
    x1 = x_ref[...] + g1_ref[...] * jnp.dot(m_ref[...], wo_ref[...], preferred_element_type=F32)
    x1_ref[...] = x1
    y = x1 * lax.rsqrt(jnp.mean(x1 * x1, axis=-1, keepdims=True) + EPS)
    _store_rows(h2_ref, y * gn_ref[...] * (1.0 + sc_ref[...]) + sh_ref[...])


def _post(x2, merged, wo, g1, gn, sc, sh, tm=256):
    t, d = x2.shape
    row = pl.BlockSpec((1, d), lambda i: (0, 0))
    tile = pl.BlockSpec((tm, d), lambda i: (i, 0))
    return pl.pallas_call(
        _post_kernel,
        grid=(t // tm,),
        in_specs=[tile, tile, pl.BlockSpec((d, d), lambda i: (0, 0)), row, row, row, row],
        out_specs=[tile, pl.BlockSpec((tm * (d // LANES), LANES), lambda i: (i, 0))],
        out_shape=[jax.ShapeDtypeStruct((t, d), F32), jax.ShapeDtypeStruct((t * (d // LANES), LANES), F32)],
        compiler_params=_params(("parallel",)),
        name="post",
    )(x2, merged, wo, g1, gn, sc, sh)


def _router_kernel(h_ref, wr_ref, br_ref, lg_ref):
    lg_ref[...] = jnp.dot(_load_rows(h_ref, wr_ref.shape[0] // LANES), wr_ref[...], preferred_element_type=F32,
                          precision=lax.Precision.HIGHEST) + br_ref[...]


def _router(h2, w_router, b_router, tm=512):
    d, nr = w_router.shape
    t = h2.shape[0] // (d // LANES)
    return pl.pallas_call(
        _router_kernel,
        grid=(t // tm,),
        in_specs=[pl.BlockSpec((tm * (d // LANES), LANES), lambda i: (i, 0)),
                  pl.BlockSpec((d, nr), lambda i: (0, 0)), pl.BlockSpec((1, nr), lambda i: (0, 0))],
        out_specs=pl.BlockSpec((tm, nr), lambda i: (i, 0)),
        out_shape=jax.ShapeDtypeStruct((t, nr), F32),
        compiler_params=_params(("parallel",)),
        name="router",
    )(h2, w_router, b_router)


def _moe_kernel(tok_ref, dst_ref, be_ref, nu_ref,
                h_hbm, w1_ref, w3_ref, w2_ref, z_hbm,
                x0, x1, y0, y1, w1b, w3b, w2b, gsem, ssem):
    b = pl.program_id(0)
    c = w1b.shape[0] // LANES
    bm = x0.shape[0] // c
    nu = nu_ref[0]
    xs, ys = (x0, x1), (y0, y1)

    def gather_copy(sl, r, t):
        return pltpu.make_async_copy(h_hbm.at[pl.ds(t * c, c), :], xs[sl].at[pl.ds(r * c, c), :], gsem.at[sl])

    def scatter_copy(sl, r, d):
        return pltpu.make_async_copy(ys[sl].at[pl.ds(r * c, c), :], z_hbm.at[pl.ds(d * c, c), :], ssem.at[sl])

    def gather_start(blk, sl):
        for r in range(bm):
            gather_copy(sl, r, tok_ref[blk * bm + r]).start()

    def scatter_start(blk, sl):
        for r in range(bm):
            scatter_copy(sl, r, dst_ref[(blk + 2) * bm + r]).start()

    def gather_wait(sl):
        for r in range(bm):
            gather_copy(sl, r, 0).wait()

    def scatter_wait(sl):
        for r in range(bm):
            scatter_copy(sl, r, 0).wait()

    @pl.when(b == 0)
    def _():
        y0[...] = jnp.zeros(y0.shape, F32)
        y1[...] = jnp.zeros(y1.shape, F32)
        scatter_start(-2, 0)
        gather_start(0, 0)

    @pl.when((b < nu) & ((b == 0) | (be_ref[b] != be_ref[jnp.maximum(b - 1, 0)])))
    def _():
        w1b[...] = w1_ref[0].astype(BF16)
        w3b[...] = w3_ref[0].astype(BF16)
        w2b[...] = w2_ref[0].astype(BF16)

    def main(sl):
        gather_wait(sl)
        scatter_wait(sl)
        gather_start(b + 1, 1 - sl)
        scatter_start(b - 1, 1 - sl)
        x = _load_rows(xs[sl], c).astype(BF16)
        a1 = jnp.dot(x, w1b[...], preferred_element_type=F32)
        a3 = jnp.dot(x, w3b[...], preferred_element_type=F32)
        a = (a1 * jax.nn.sigmoid(a1)) * a3
        y = jnp.dot(a.astype(BF16), w2b[...], preferred_element_type=F32)
        _store_rows(ys[sl], y)

    def drain(sl):
        gather_wait(sl)
        scatter_wait(sl)
        scatter_start(b - 1, 1 - sl)
        scatter_wait(1 - sl)

    for sl in range(2):
        @pl.when((b < nu) & (b % 2 == sl))
        def _():
            main(sl)

        @pl.when((b == nu) & (b % 2 == sl))
        def _():
            drain(sl)


def _moe(h2, tok_buf, dst_buf, blk_e, n_used, w1, w3, w2):
    d, dff = w1.shape[1:]
    c = d // LANES
    t = h2.shape[0] // c
    bm = MOE_BM
    nsteps = blk_e.shape[0]
    assert tok_buf.shape[0] == nsteps * bm and dst_buf.shape[0] == (nsteps + 2) * bm
    grid_spec = pltpu.PrefetchScalarGridSpec(
        num_scalar_prefetch=4,
        grid=(nsteps,),
        in_specs=[
            pl.BlockSpec(memory_space=pl.ANY),
            pl.BlockSpec((1, d, dff), lambda b, tok, dst, be, nu: (be[b], 0, 0)),
            pl.BlockSpec((1, d, dff), lambda b, tok, dst, be, nu: (be[b], 0, 0)),
            pl.BlockSpec((1, dff, d), lambda b, tok, dst, be, nu: (be[b], 0, 0)),
        ],
        out_specs=pl.BlockSpec(memory_space=pl.ANY),
        scratch_shapes=[pltpu.VMEM((bm * c, LANES), F32), pltpu.VMEM((bm * c, LANES), F32),
                        pltpu.VMEM((bm * c, LANES), F32), pltpu.VMEM((bm * c, LANES), F32),
                        pltpu.VMEM((d, dff), BF16), pltpu.VMEM((d, dff), BF16), pltpu.VMEM((dff, d), BF16),
                        pltpu.SemaphoreType.DMA((2,)), pltpu.SemaphoreType.DMA((2,))],
    )
    return pl.pallas_call(
        _moe_kernel,
        grid_spec=grid_spec,
        out_shape=jax.ShapeDtypeStruct(((TOPK_IN_GROUP * t + 2 * bm) * c, LANES), F32),
        compiler_params=pltpu.CompilerParams(dimension_semantics=("arbitrary",),
                                             vmem_limit_bytes=VMEM_LIMIT, has_side_effects=True),
        name="moe",
    )(tok_buf, dst_buf, blk_e, n_used, h2, w1, w3, w2)


def _final_kernel(x1_ref, z0_ref, z1_ref, w_ref, g2_ref, gn_ref, o_ref):
    c = x1_ref.shape[1] // LANES
    moe = w_ref[:, 0:1] * _load_rows(z0_ref, c) + w_ref[:, 1:2] * _load_rows(z1_ref, c)
    x2 = x1_ref[...] + g2_ref[...] * moe
    y = x2 * lax.rsqrt(jnp.mean(x2 * x2, axis=-1, keepdims=True) + EPS)
    o_ref[...] = y * gn_ref[...]


def _final(x1, z, weight, g2, gn, tm=256):
    t, d = x1.shape
    row = pl.BlockSpec((1, d), lambda i: (0, 0))
    return pl.pallas_call(
        _final_kernel,
        grid=(t // tm,),
        in_specs=[pl.BlockSpec((tm, d), lambda i: (i, 0)),
                  pl.BlockSpec((tm * (d // LANES), LANES), lambda i: (i, 0)),
                  pl.BlockSpec((tm * (d // LANES), LANES), lambda i: (t // tm + i, 0)),
                  pl.BlockSpec((tm, TOPK_IN_GROUP), lambda i: (i, 0)), row, row],
        out_specs=pl.BlockSpec((tm, d), lambda i: (i, 0)),
        out_shape=jax.ShapeDtypeStruct((t, d), F32),
        compiler_params=_params(("parallel",)),
        name="final",
    )(x1, z, z, weight, g2, gn)


def _route(logits, t):
    gl = logits[:, :N_GROUPS]
    el = logits[:, N_GROUPS:N_GROUPS + N_EXPERTS].reshape(t, N_GROUPS, EXP_PER_GROUP)
    g_prob = jax.nn.softmax(gl, axis=-1)
    grp = jnp.argmax(gl, axis=-1).astype(I32)
    p_grp = jnp.take_along_axis(g_prob, grp[:, None], axis=-1)[:, 0]
    e_in = jnp.take_along_axis(el, grp[:, None, None], axis=1)[:, 0]
    top_v, top_i = lax.top_k(e_in, TOPK_IN_GROUP)
    p_in = jax.nn.softmax(top_v, axis=-1)
    expert = grp[:, None] * EXP_PER_GROUP + top_i.astype(I32)
    weight = p_grp[:, None] * p_in

    bm = MOE_BM
    m = t * TOPK_IN_GROUP
    e_flat = expert.reshape(m)
    order = jnp.argsort(e_flat, stable=True).astype(I32)
    counts = jnp.bincount(e_flat, length=N_EXPERTS).astype(I32)
    start = jnp.cumsum(counts) - counts
    padded = ((counts + bm - 1) // bm) * bm
    pend = jnp.cumsum(padded)
    pstart = pend - padded
    nb = m // bm + N_EXPERTS + 1
    blk_first = jnp.arange(nb, dtype=I32) * bm
    blk_e = jnp.minimum(jnp.sum((pend[None, :] <= blk_first[:, None]).astype(I32), axis=1), N_EXPERTS - 1)
    n_used = (pend[-1] // bm).astype(I32).reshape(1)
    pos = jnp.arange(nb * bm, dtype=I32).reshape(nb, bm)
    src = (start[blk_e] - pstart[blk_e])[:, None] + pos
    valid = (src < (start + counts)[blk_e][:, None]) & (pos < pend[-1])
    a = order[jnp.clip(src, 0, m - 1)]
    tok = a // TOPK_IN_GROUP
    tok_buf = jnp.where(valid, tok, 0).reshape(-1)
    spare = m + pos % (2 * bm)
    dst_buf = jnp.where(valid, (a % TOPK_IN_GROUP) * t + tok, spare).reshape(-1)
    dst_buf = jnp.concatenate([m + jnp.arange(2 * bm, dtype=I32), dst_buf])
    return tok_buf, dst_buf, blk_e, n_used, weight


def kernel(x, c, w_ada, b_ada, norm_mix, w_in, t5_table, rel_table, w_up_a, w_up_b, w_o, norm_ffn,
           w_rg, b_rg, w_re, b_re, w1, w3, w2, norm_final):
    bn, s, d = x.shape
    assert bn == 1 and w_ada.shape[0] == 1
    assert s % 1024 == 0
    t = bn * s
    x2 = x.reshape(t, d)
    n_sel = min(TOPK_MAX, s // 4)

    mod = _ada(c.reshape(d, 1), w_ada[0], b_ada[0].reshape(1, 6 * d))
    sh1, sc1, g1, sh2, sc2, g2 = [mod[:, i * d:(i + 1) * d] for i in range(6)]

    h = _rms_mod(x2, norm_mix[0].reshape(1, d), sc1, sh1, BF16)

    cols = np.cumsum([0, W_A, W_A, W_A, W_IDX_Q, D_IDX, H_IDX, W_B, W_B, W_B, d, d])
    wsl = [w_in[0][:, cols[i]:cols[i + 1]] for i in range(11)]
    wqa, wka, wva, wqi, wki, wwi, wqb, wkb, wvb, wga, wgb = wsl
    w_main = jnp.concatenate([wqa * (LOG2E / math.sqrt(HD_A)), wka, wva, wqi,
                              wqb * (LOG2E / math.sqrt(HD_B)), wkb, wvb], axis=1).astype(BF16)
    w_gate = jnp.concatenate([wga, wgb], axis=1).astype(BF16)
    w_idx = jnp.concatenate([wki, wwi, jnp.zeros((d, LANES - D_IDX - H_IDX), F32)], axis=1).astype(BF16)

    proj = _matmul(h, w_main, BF16, 1024, 1024, "proj_main")
    gates = _matmul(h, w_gate, F32, 1024, 1024, "proj_gate")
    idx = _matmul(h, w_idx, F32, 1024, LANES, "proj_idx")
    nkt = s // SEL_TK
    kit3 = jnp.transpose(idx[:, :D_IDX].astype(BF16).reshape(nkt, SEL_TK, D_IDX), (0, 2, 1))
    wi = idx[:, D_IDX:D_IDX + H_IDX] * ((H_IDX ** -0.5) * (D_IDX ** -0.5))
    kt_a = jnp.transpose(proj[:, W_A:2 * W_A])
    kt_b = jnp.transpose(proj[:, 5 * W_A:6 * W_A])

    mask4 = _select(proj, wi, kit3, n_sel, qi_col_block=3)
    y_a = _dsa_attn(proj, kt_a, mask4, t5_table, q_col=0, v_col=2)
    y_b = _band_attn(proj, kt_b, rel_table[0], q_col=4, v_col=6)

    merged = _merge(y_a, y_b, gates, w_up_a[0].astype(BF16), w_up_b[0].astype(BF16))

    nr = LANES
    w_router = jnp.concatenate([w_rg[0], w_re[0], jnp.zeros((d, nr - N_GROUPS - N_EXPERTS), F32)], axis=1)
    b_router = jnp.concatenate([b_rg[0], b_re[0], jnp.zeros((nr - N_GROUPS - N_EXPERTS,), F32)]).reshape(1, nr)
    x1, h2 = _post(x2, merged, w_o[0].astype(BF16), g1, norm_ffn[0].reshape(1, d), sc2, sh2)
    logits = _router(h2, w_router, b_router)
    tok_buf, dst_buf, blk_e, n_used, weight = _route(logits, t)
    z = _moe(h2, tok_buf, dst_buf, blk_e, n_used, w1[0], w3[0], w2[0])
    out = _final(x1, z, weight, g2, norm_final.reshape(1, d))
    return out.reshape(bn, s, d)
```
